```python
import jax, jax.numpy as jnp
from jax import lax
import numpy as np

D_MODEL = 2048
BATCH = 1
SEQ = 8192
DEPTH = 1

EPS = 1e-6
Q_BLOCK = 128
NEG = -1e30

MLA_HEADS = 8
MLA_Q_RANK = 512
MLA_KV_RANK = 256
MLA_NOPE = 128
MLA_ROPE = 64
MLA_V = 128
ROPE_THETA = 10000.0

NSA_HEADS = 16
NSA_KV_GROUPS = 2
NSA_HPG = NSA_HEADS // NSA_KV_GROUPS
NSA_DK = 64
NSA_DV = 64
CMP_LEN = 32
CMP_STRIDE = 16
CMP_HIDDEN = 256
SEL_LEN = 64
SEL_TOPK = 16
WINDOW = 512
FORCE_BONUS = 1e4
NSA_KV_WIDTH = NSA_KV_GROUPS * (NSA_DK + NSA_DV)

MIX_WIDTH = MLA_HEADS * MLA_V + NSA_HEADS * NSA_DV
MLP_HIDDEN = 4 * D_MODEL

IN_SIZES = (MLA_Q_RANK, MLA_KV_RANK, MLA_ROPE, NSA_HEADS * NSA_DK,
            NSA_KV_WIDTH, NSA_KV_WIDTH, NSA_KV_WIDTH, 3 * NSA_HEADS)
IN_WIDTH = sum(IN_SIZES)

kernel_name = "hybrid_mla_nsa_block"


def rmsnorm(x, w):
    xf = x.astype(jnp.float32)
    y = xf * lax.rsqrt(jnp.mean(xf * xf, axis=-1, keepdims=True) + EPS)
    return (y * w.astype(jnp.float32)).astype(x.dtype)


def _split_cols(t, sizes):
    offs = np.cumsum(sizes)[:-1].tolist()
    return jnp.split(t, offs, axis=-1)


def rope_cos_sin(positions, dim):
    inv = 1.0 / (ROPE_THETA ** (jnp.arange(0, dim, 2, dtype=jnp.float32) / dim))
    ang = positions.astype(jnp.float32)[..., None] * inv
    return jnp.cos(ang), jnp.sin(ang)


def apply_rope(x, cos, sin):
    xf = x.astype(jnp.float32)
    half = xf.shape[-1] // 2
    x1, x2 = xf[..., :half], xf[..., half:]
    return jnp.concatenate([x1 * cos - x2 * sin, x2 * cos + x1 * sin], axis=-1).astype(x.dtype)


def alibi_slopes(n):
    start = 2.0 ** (-8.0 / n)
    return start ** jnp.arange(1, n + 1, dtype=jnp.float32)


def masked_softmax(s, mask):
    p = jax.nn.softmax(jnp.where(mask, s, NEG), axis=-1)
    return jnp.where(mask, p, 0.0)


def mla_attention(c_q, c_kv, k_rope, positions, q_norm_w, w_uq, kv_norm_w, w_ukv):
    B, T, _ = c_q.shape
    q = (rmsnorm(c_q, q_norm_w) @ w_uq).reshape(B, T, MLA_HEADS, MLA_NOPE + MLA_ROPE)
    q_nope, q_pe = q[..., :MLA_NOPE], q[..., MLA_NOPE:]
    kv = (rmsnorm(c_kv, kv_norm_w) @ w_ukv).reshape(B, T, MLA_HEADS, MLA_NOPE + MLA_V)
    k_nope, v = kv[..., :MLA_NOPE], kv[..., MLA_NOPE:]
    cos, sin = rope_cos_sin(positions, MLA_ROPE)
    q_pe = apply_rope(q_pe, cos[:, :, None], sin[:, :, None])
    k_pe = apply_rope(k_rope, cos, sin)
    scale = (MLA_NOPE + MLA_ROPE) ** -0.5
    nb = T // Q_BLOCK
    qn_b = jnp.moveaxis(q_nope.reshape(B, nb, Q_BLOCK, MLA_HEADS, MLA_NOPE), 1, 0)
    qp_b = jnp.moveaxis(q_pe.reshape(B, nb, Q_BLOCK, MLA_HEADS, MLA_ROPE), 1, 0)
    kpos = jnp.arange(T)

    def block(args):
        i, qn, qp = args
        s = (jnp.einsum('bqhd,bkhd->bhqk', qn, k_nope)
             + jnp.einsum('bqhd,bkd->bhqk', qp, k_pe)).astype(jnp.float32) * scale
        qpos = i * Q_BLOCK + jnp.arange(Q_BLOCK)
        p = masked_softmax(s, kpos[None, :] <= qpos[:, None])
        return jnp.einsum('bhqk,bkhd->bqhd', p.astype(v.dtype), v)

    out = lax.map(block, (jnp.arange(nb), qn_b, qp_b))
    return jnp.moveaxis(out, 0, 1).reshape(B, T, MLA_HEADS * MLA_V)


def nsa_attention(q, kv_cmp, kv_slc, kv_win, gate_logits,
                  cmp_pos_k, cmp_pos_v, cmp_w1_k, cmp_w2_k, cmp_w1_v, cmp_w2_v):
    B, T, _ = q.shape
    G, HG = NSA_KV_GROUPS, NSA_HPG
    q = q.reshape(B, T, G, HG, NSA_DK)
    gates = jax.nn.sigmoid(gate_logits.astype(jnp.float32)).reshape(B, T, G, HG, 3).astype(q.dtype)

    def kv_split(kv):
        kv = kv.reshape(B, T, G, NSA_DK + NSA_DV)
        return kv[..., :NSA_DK], kv[..., NSA_DK:]

    k_c, v_c = kv_split(kv_cmp)
    k_s, v_s = kv_split(kv_slc)
    k_w, v_w = kv_split(kv_win)

    n_cmp = (T - CMP_LEN) // CMP_STRIDE + 1
    cmp_start = jnp.arange(n_cmp) * CMP_STRIDE
    cmp_end = cmp_start + CMP_LEN - 1
    tok_idx = cmp_start[:, None] + jnp.arange(CMP_LEN)[None, :]

    def compress(t, pos, w1, w2):
        blk = t[:, tok_idx] + pos[None, None, :, None, :]
        blk = blk.transpose(0, 1, 3, 2, 4).reshape(B, n_cmp, G, CMP_LEN * t.shape[-1])
        return jax.nn.gelu(blk @ w1) @ w2

    k_cmp = compress(k_c, cmp_pos_k, cmp_w1_k, cmp_w2_k)
    v_cmp = compress(v_c, cmp_pos_v, cmp_w1_v, cmp_w2_v)

    n_sel = T // SEL_LEN
    top_k = min(SEL_TOPK, n_sel)
    sel_start = jnp.arange(n_sel) * SEL_LEN
    overlap = ((cmp_start[:, None] <= sel_start[None, :] + SEL_LEN - 1)
               & (cmp_end[:, None] >= sel_start[None, :])).astype(jnp.float32)
    slopes = alibi_slopes(NSA_HEADS).reshape(G, HG)[None, :, :, None, None]
    k_s_g = k_s.transpose(0, 2, 1, 3)
    v_s_g = v_s.transpose(0, 2, 1, 3)
    k_w_pad = jnp.pad(k_w, ((0, 0), (WINDOW, 0), (0, 0), (0, 0)))
    v_w_pad = jnp.pad(v_w, ((0, 0), (WINDOW, 0), (0, 0), (0, 0)))
    scale = NSA_DK ** -0.5
    gather = jax.vmap(jax.vmap(lambda src, idx: src[idx]))
    blk_ids = jnp.arange(n_sel)

    nb = T // Q_BLOCK
    q_b = jnp.moveaxis(q.reshape(B, nb, Q_BLOCK, G, HG, NSA_DK), 1, 0)
    g_b = jnp.moveaxis(gates.reshape(B, nb, Q_BLOCK, G, HG, 3), 1, 0)

    def block(args):
        i, qb, gb = args
        qpos = i * Q_BLOCK + jnp.arange(Q_BLOCK)
        s = jnp.einsum('bqghd,bngd->bghqn', qb, k_cmp).astype(jnp.float32) * scale
        dist = qpos[:, None] - cmp_end[None, :]
        s = s - slopes * dist.astype(jnp.float32)
        p_c = masked_softmax(s, dist >= 0)
        o_c = jnp.einsum('bghqn,bngd->bqghd', p_c.astype(v_cmp.dtype), v_cmp)
        imp = jnp.einsum('bghqn,ns->bgqs', p_c, overlap)
        cur = qpos // SEL_LEN
        forced = ((blk_ids[None, :] == 0) | (blk_ids[None, :] == cur[:, None])
                  | (blk_ids[None, :] == cur[:, None] - 1)).astype(jnp.float32)
        score = jnp.where(blk_ids[None, :] <= cur[:, None], imp + FORCE_BONUS * forced, NEG)
        top_val, top_idx = lax.top_k(score, top_k)
        tok = (top_idx[..., None] * SEL_LEN + jnp.arange(SEL_LEN)).reshape(B, G, Q_BLOCK, top_k * SEL_LEN)
        blk_ok = jnp.repeat(top_val > NEG * 0.5, SEL_LEN, axis=-1)
        k_sel = gather(k_s_g, tok)
        v_sel = gather(v_s_g, tok)
        s = jnp.einsum('bqghd,bgqmd->bghqm', qb, k_sel).astype(jnp.float32) * scale
        dist = qpos[None, None, :, None] - tok
        s = s - slopes * dist[:, :, None].astype(jnp.float32)
        p_s = masked_softmax(s, ((dist >= 0) & blk_ok)[:, :, None])
        o_s = jnp.einsum('bghqm,bgqmd->bqghd', p_s.astype(v_sel.dtype), v_sel)
        kw = lax.dynamic_slice_in_dim(k_w_pad, i * Q_BLOCK, WINDOW + Q_BLOCK, axis=1)
        vw = lax.dynamic_slice_in_dim(v_w_pad, i * Q_BLOCK, WINDOW + Q_BLOCK, axis=1)
        kpos = i * Q_BLOCK - WINDOW + jnp.arange(WINDOW + Q_BLOCK)
        dist = qpos[:, None] - kpos[None, :]
        mask = (dist >= 0) & (dist < WINDOW) & (kpos[None, :] >= 0)
        s = jnp.einsum('bqghd,bkgd->bghqk', qb, kw).astype(jnp.float32) * scale
        s = s - slopes * dist.astype(jnp.float32)
        p_w = masked_softmax(s, mask)
        o_w = jnp.einsum('bghqk,bkgd->bqghd', p_w.astype(vw.dtype), vw)
        return gb[..., 0:1] * o_c + gb[..., 1:2] * o_s + gb[..., 2:3] * o_w

    out = lax.map(block, (jnp.arange(nb), q_b, g_b))
    return jnp.moveaxis(out, 0, 1).reshape(B, T, NSA_HEADS * NSA_DV)


def setup_inputs(seed: int = 0) -> dict:
    key = jax.random.key(seed)
    ks = jax.random.split(key, 24)
    f32 = jnp.float32

    def w(k, shape, fan_in):
        return jax.random.normal(k, shape, f32) * (fan_in ** -0.5)

    def gain(k, shape):
        return 1.0 + 0.02 * jax.random.normal(k, shape, f32)

    L = DEPTH
    return {
        "x": jax.random.normal(ks[0], (BATCH, SEQ, D_MODEL), f32),
        "positions": jnp.broadcast_to(jnp.arange(SEQ, dtype=jnp.int32), (BATCH, SEQ)),
        "attn_norm_w": gain(ks[1], (L, D_MODEL)),
        "w_in": w(ks[2], (L, D_MODEL, IN_WIDTH), D_MODEL),
        "mla_q_norm_w": gain(ks[3], (L, MLA_Q_RANK)),
        "mla_w_uq": w(ks[4], (L, MLA_Q_RANK, MLA_HEADS * (MLA_NOPE + MLA_ROPE)), MLA_Q_RANK),
        "mla_kv_norm_w": gain(ks[5], (L, MLA_KV_RANK)),
        "mla_w_ukv": w(ks[6], (L, MLA_KV_RANK, MLA_HEADS * (MLA_NOPE + MLA_V)), MLA_KV_RANK),
        "cmp_pos_k": 0.1 * jax.random.normal(ks[7], (L, CMP_LEN, NSA_DK), f32),
        "cmp_pos_v": 0.1 * jax.random.normal(ks[8], (L, CMP_LEN, NSA_DV), f32),
        "cmp_w1_k": w(ks[9], (L, CMP_LEN * NSA_DK, CMP_HIDDEN), CMP_LEN * NSA_DK),
        "cmp_w2_k": w(ks[10], (L, CMP_HIDDEN, NSA_DK), CMP_HIDDEN),
        "cmp_w1_v": w(ks[11], (L, CMP_LEN * NSA_DV, CMP_HIDDEN), CMP_LEN * NSA_DV),
        "cmp_w2_v": w(ks[12], (L, CMP_HIDDEN, NSA_DV), CMP_HIDDEN),
        "mla_out_norm_w": gain(ks[13], (L, MLA_HEADS * MLA_V)),
        "nsa_out_norm_w": gain(ks[14], (L, NSA_HEADS * NSA_DV)),
        "w_o": w(ks[15], (L, MIX_WIDTH, D_MODEL), MIX_WIDTH),
        "mlp_norm_w": gain(ks[16], (L, D_MODEL)),
        "w_up": w(ks[17], (L, D_MODEL, MLP_HIDDEN), D_MODEL),
        "w_down": w(ks[18], (L, MLP_HIDDEN, D_MODEL), MLP_HIDDEN),
        "final_norm_w": gain(ks[19], (D_MODEL,)),
    }


def reference(x, positions, attn_norm_w, w_in, mla_q_norm_w, mla_w_uq, mla_kv_norm_w, mla_w_ukv,
              cmp_pos_k, cmp_pos_v, cmp_w1_k, cmp_w2_k, cmp_w1_v, cmp_w2_v,
              mla_out_norm_w, nsa_out_norm_w, w_o, mlp_norm_w, w_up, w_down, final_norm_w):
    h = x
    for l in range(DEPTH):
        n = rmsnorm(h, attn_norm_w[l])
        proj = n @ w_in[l]
        c_q, c_kv, k_rope, q_nsa, kv_cmp, kv_slc, kv_win, gate_logits = _split_cols(proj, IN_SIZES)
        y_mla = mla_attention(c_q, c_kv, k_rope, positions, mla_q_norm_w[l], mla_w_uq[l],
                              mla_kv_norm_w[l], mla_w_ukv[l])
        y_nsa = nsa_attention(q_nsa, kv_cmp, kv_slc, kv_win, gate_logits,
                              cmp_pos_k[l], cmp_pos_v[l], cmp_w1_k[l], cmp_w2_k[l],
                              cmp_w1_v[l], cmp_w2_v[l])
        y = jnp.concatenate([rmsnorm(y_mla, mla_out_norm_w[l]),
                             rmsnorm(y_nsa, nsa_out_norm_w[l])], axis=-1)
        h = h + y @ w_o[l]
        n = rmsnorm(h, mlp_norm_w[l])
        h = h + jnp.square(jax.nn.relu(n @ w_up[l])) @ w_down[l]
    return rmsnorm(h, final_norm_w)
```

```python
import functools

import numpy as np
import jax
import jax.numpy as jnp
from jax import lax
from jax.experimental import pallas as pl
from jax.experimental.pallas import tpu as pltpu

F32 = jnp.float32
BF16 = jnp.bfloat16

EPS = 1e-6
NEG = -1e30
REMOVED = -3e38

D_MODEL = 2048
MLA_HEADS = 8
MLA_Q_RANK = 512
MLA_KV_RANK = 256
MLA_NOPE = 128
MLA_ROPE = 64
MLA_V = 128
ROPE_THETA = 10000.0
MLA_QK_PAD = 256

NSA_HEADS = 16
NSA_G = 2
NSA_HPG = 8
NSA_D = 64
CMP_LEN = 32
CMP_STRIDE = 16
CMP_HIDDEN = 256
SEL_LEN = 64
SEL_SHIFT = 6
SEL_TOPK = 16
WINDOW = 512
FORCE_BONUS = 1e4
MLP_HIDDEN = 4 * D_MODEL

LANES = 128
NSA_TQ = 128
SEL_TK = 512
MLA_TQ = 512
MLA_TK = 512
VMEM_LIMIT = 56 * 1024 * 1024

SEC_CQ = 0
SEC_CKV = 512
SEC_HM = 768
SEC_ROPE = 2560
SEC_GATE = 2688
IN_PAD = 2816


def _cparams(sem):
    return pltpu.CompilerParams(dimension_semantics=sem, vmem_limit_bytes=VMEM_LIMIT)


def _rms(x, w):
    return x * lax.rsqrt(jnp.mean(x * x, axis=-1, keepdims=True) + EPS) * w


def _in_proj_kernel(x_ref, nw_ref, w_ref, o_ref):
    n = _rms(x_ref[...], nw_ref[...]).astype(BF16)
    o_ref[...] = jnp.dot(n, w_ref[...], preferred_element_type=F32)


def _in_proj(x2, attn_norm_w, w_in_p, tm=512):
    T = x2.shape[0]
    return pl.pallas_call(
        _in_proj_kernel,
        out_shape=jax.ShapeDtypeStruct((T, IN_PAD), F32),
        grid=(T // tm,),
        in_specs=[
            pl.BlockSpec((tm, D_MODEL), lambda i: (i, 0)),
            pl.BlockSpec((1, D_MODEL), lambda i: (0, 0)),
            pl.BlockSpec((D_MODEL, IN_PAD), lambda i: (0, 0)),
        ],
        out_specs=pl.BlockSpec((tm, IN_PAD), lambda i: (i, 0)),
        compiler_params=_cparams(("parallel",)),
        name="in_proj",
    )(x2, attn_norm_w.reshape(1, D_MODEL), w_in_p)


def _mla_prep_kernel(cq_ref, ckv_ref, kr_ref, pos_ref, qnw_ref, kvnw_ref, wuq_ref, wukv_ref, inv_ref,
                     q_ref, k_ref, v_ref):
    scale = (MLA_NOPE + MLA_ROPE) ** -0.5
    qn = _rms(cq_ref[...], qnw_ref[...]).astype(BF16)
    q = jnp.dot(qn, wuq_ref[...], preferred_element_type=F32)
    kvn = _rms(ckv_ref[...], kvnw_ref[...]).astype(BF16)
    kv = jnp.dot(kvn, wukv_ref[...], preferred_element_type=F32)

    ang = pos_ref[...].astype(F32) * inv_ref[...]
    c = jnp.cos(ang)
    s = jnp.sin(ang)
    lane = lax.broadcasted_iota(jnp.int32, ang.shape, 1)
    coef = jnp.where(lane < 64, c, jnp.where(lane < 96, -s, s))

    def rope(blk):
        t = blk * coef
        return t + pltpu.roll(t, 64, 1)

    kpe = jnp.where(lane < 64, rope(kr_ref[...]), 0.0)
    for h in range(MLA_HEADS):
        base = h * MLA_QK_PAD
        qh = jnp.concatenate([q[:, base:base + 128], rope(q[:, base + 128:base + 256])], axis=-1) * scale
        q_ref[h] = qh.astype(BF16)
        k_ref[h] = jnp.concatenate([kv[:, base:base + 128], kpe], axis=-1).astype(BF16)
        v_ref[h] = kv[:, base + 128:base + 256].astype(BF16)


def _mla_prep(proj, pos2, qnw, kvnw, wuq_p, wukv, inv128, tm=256):
    T = proj.shape[0]
    H = MLA_HEADS
    return pl.pallas_call(
        _mla_prep_kernel,
        out_shape=(
            jax.ShapeDtypeStruct((H, T, MLA_QK_PAD), BF16),
            jax.ShapeDtypeStruct((H, T, MLA_QK_PAD), BF16),
            jax.ShapeDtypeStruct((H, T, MLA_V), BF16),
        ),
        grid=(T // tm,),
        in_specs=[
            pl.BlockSpec((tm, MLA_Q_RANK), lambda i: (i, SEC_CQ // MLA_Q_RANK)),
            pl.BlockSpec((tm, MLA_KV_RANK), lambda i: (i, SEC_CKV // MLA_KV_RANK)),
            pl.BlockSpec((tm, LANES), lambda i: (i, SEC_ROPE // LANES)),
            pl.BlockSpec((tm, 1), lambda i: (i, 0)),
            pl.BlockSpec((1, MLA_Q_RANK), lambda i: (0, 0)),
            pl.BlockSpec((1, MLA_KV_RANK), lambda i: (0, 0)),
            pl.BlockSpec((MLA_Q_RANK, H * MLA_QK_PAD), lambda i: (0, 0)),
            pl.BlockSpec((MLA_KV_RANK, H * (MLA_NOPE + MLA_V)), lambda i: (0, 0)),
            pl.BlockSpec((1, LANES), lambda i: (0, 0)),
        ],
        out_specs=(
            pl.BlockSpec((H, tm, MLA_QK_PAD), lambda i: (0, i, 0)),
            pl.BlockSpec((H, tm, MLA_QK_PAD), lambda i: (0, i, 0)),
            pl.BlockSpec((H, tm, MLA_V), lambda i: (0, i, 0)),
        ),
        compiler_params=_cparams(("parallel",)),
        name="mla_prep",
    )(proj, proj, proj, pos2, qnw.reshape(1, -1), kvnw.reshape(1, -1), wuq_p, wukv, inv128)


def _mla_attn_kernel(qi_ref, ki_ref, q_ref, k_ref, v_ref, o_ref, m_scr, l_scr, acc_scr):
    p_id = pl.program_id(1)
    qi = qi_ref[p_id]
    ki = ki_ref[p_id]

    @pl.when(ki == 0)
    def _():
        m_scr[...] = jnp.full(m_scr.shape, NEG, F32)
        l_scr[...] = jnp.zeros(l_scr.shape, F32)
        acc_scr[...] = jnp.zeros(acc_scr.shape, F32)

    def step(diag):
        s = lax.dot_general(q_ref[0], k_ref[0], (((1,), (1,)), ((), ())), preferred_element_type=F32)
        if diag:
            row = lax.broadcasted_iota(jnp.int32, s.shape, 0)
            col = lax.broadcasted_iota(jnp.int32, s.shape, 1)
            s = jnp.where(col <= row, s, NEG)
        m_prev = m_scr[...]
        m_new = jnp.maximum(m_prev, jnp.max(s, axis=-1, keepdims=True))
        alpha = jnp.exp(m_prev - m_new)
        p = jnp.exp(s - m_new)
        l_scr[...] = alpha * l_scr[...] + jnp.sum(p, axis=-1, keepdims=True)
        acc_scr[...] = alpha * acc_scr[...] + jnp.dot(p.astype(BF16), v_ref[0], preferred_element_type=F32)
        m_scr[...] = m_new

    @pl.when(ki < qi)
    def _():
        step(False)

    @pl.when(ki == qi)
    def _():
        step(True)
        o_ref[...] = acc_scr[...] / l_scr[...]


def _tri_pairs(nq, per_q):
    qi, ki = [], []
    for i in range(nq):
        for c in range(per_q(i) + 1):
            qi.append(i)
            ki.append(c)
    return jnp.asarray(qi, jnp.int32), jnp.asarray(ki, jnp.int32)


def _mla_attn(q, k, v):
    H, T, _ = q.shape
    tq = min(MLA_TQ, T)
    nq = T // tq
    qi, ki = _tri_pairs(nq, lambda i: i)
    grid_spec = pltpu.PrefetchScalarGridSpec(
        num_scalar_prefetch=2,
        grid=(H, int(qi.shape[0])),
        in_specs=[
            pl.BlockSpec((1, tq, MLA_QK_PAD), lambda h, p, qi, ki: (h, qi[p], 0)),
            pl.BlockSpec((1, tq, MLA_QK_PAD), lambda h, p, qi, ki: (h, ki[p], 0)),
            pl.BlockSpec((1, tq, MLA_V), lambda h, p, qi, ki: (h, ki[p], 0)),
        ],
        out_specs=pl.BlockSpec((tq, MLA_V), lambda h, p, qi, ki: (qi[p], h)),
        scratch_shapes=[
            pltpu.VMEM((tq, 1), F32),
            pltpu.VMEM((tq, 1), F32),
            pltpu.VMEM((tq, MLA_V), F32),
        ],
    )
    return pl.pallas_call(
        _mla_attn_kernel,
        out_shape=jax.ShapeDtypeStruct((T, H * MLA_V), F32),
        grid_spec=grid_spec,
        compiler_params=_cparams(("parallel", "arbitrary")),
        name="mla_attn",
    )(qi, ki, q, k, v)


def _gelu_tanh(x):
    return 0.5 * x * (1.0 + jnp.tanh(np.sqrt(2.0 / np.pi).astype(np.float32) * (x + 0.044715 * (x * x * x))))


def _compress_kernel(r_ref, pos_ref, w1_ref, w2_ref, o_ref):
    r = r_ref[0]
    nr = r.shape[0]
    half = CMP_STRIDE * NSA_D
    hp = lax.Precision.HIGHEST
    a = jnp.dot(r + pos_ref[0, 0:1, :], w1_ref[0, :half, :], precision=hp, preferred_element_type=F32)
    b = jnp.dot(r + pos_ref[0, 1:2, :], w1_ref[0, half:, :], precision=hp, preferred_element_type=F32)
    hid = _gelu_tanh(a + pltpu.roll(b, nr - 1, 0))
    out = jnp.dot(hid, w2_ref[0], precision=hp, preferred_element_type=F32)
    row = lax.broadcasted_iota(jnp.int32, out.shape, 0)
    o_ref[0] = jnp.where(row < nr - 1, out, 0.0)


def _compress(r4, pos_kv, w1_kv, w2_kv):
    _, nr, width = r4.shape
    return pl.pallas_call(
        _compress_kernel,
        out_shape=jax.ShapeDtypeStruct((4, nr, NSA_D), F32),
        grid=(4,),
        in_specs=[
            pl.BlockSpec((1, nr, width), lambda a: (a, 0, 0)),
            pl.BlockSpec((1, 2, width), lambda a: (a % 2, 0, 0)),
            pl.BlockSpec((1, 2 * width, CMP_HIDDEN), lambda a: (a % 2, 0, 0)),
            pl.BlockSpec((1, CMP_HIDDEN, NSA_D), lambda a: (a % 2, 0, 0)),
        ],
        out_specs=pl.BlockSpec((1, nr, NSA_D), lambda a: (a, 0, 0)),
        compiler_params=_cparams(("parallel",)),
        name="nsa_compress",
    )(r4, pos_kv, w1_kv, w2_kv)


def _store_heads(o_ref, o):
    tq = o.shape[0] // NSA_HPG
    for h in range(NSA_HPG):
        o_ref[:, h * NSA_D:(h + 1) * NSA_D] = o[h * tq:(h + 1) * tq, :]


def _cmp_attn_kernel(slopes_ref, q_ref, kc_ref, vc_ref, o_ref, sel_ref, *, nsel_pad):
    g = pl.program_id(0)
    i = pl.program_id(1)
    tq = NSA_TQ
    nr = kc_ref.shape[1]
    qs = (q_ref[...].reshape(NSA_HPG * tq, NSA_D) * (NSA_D ** -0.5)).astype(BF16)
    s = lax.dot_general(qs, kc_ref[0].astype(BF16), (((1,), (1,)), ((), ())), preferred_element_type=F32)

    qpos = i * tq + lax.broadcasted_iota(jnp.int32, (tq, nr), 0)
    n_id = lax.broadcasted_iota(jnp.int32, (tq, nr), 1)
    dist_i = qpos - (n_id * CMP_STRIDE + (CMP_LEN - 1))
    valid = jnp.logical_and(dist_i >= 0, n_id < nr - 1)
    dist = dist_i.astype(F32)

    psum = jnp.zeros((tq, nr), F32)
    outs = []
    for h in range(NSA_HPG):
        sh = s[h * tq:(h + 1) * tq, :] - slopes_ref[g * NSA_HPG + h] * dist
        sh = jnp.where(valid, sh, NEG)
        m = jnp.max(sh, axis=-1, keepdims=True)
        e = jnp.where(valid, jnp.exp(sh - m), 0.0)
        l = jnp.sum(e, axis=-1, keepdims=True)
        p = e / jnp.where(l > 0.0, l, 1.0)
        psum = psum + p
        outs.append(p.astype(BF16))
    p_all = jnp.concatenate(outs, axis=0)
    o = jnp.dot(p_all, vc_ref[0].astype(BF16), preferred_element_type=F32)
    _store_heads(o_ref, o)

    n2 = lax.broadcasted_iota(jnp.int32, (nr, nsel_pad), 0)
    j2 = lax.broadcasted_iota(jnp.int32, (nr, nsel_pad), 1)
    ov = jnp.logical_and(n2 * CMP_STRIDE <= j2 * SEL_LEN + (SEL_LEN - 1),
                         n2 * CMP_STRIDE + (CMP_LEN - 1) >= j2 * SEL_LEN)
    ov = jnp.logical_and(ov, n2 < nr - 1)
    imp = jnp.dot(psum, jnp.where(ov, 1.0, 0.0).astype(F32), precision=lax.Precision.HIGHEST,
                  preferred_element_type=F32)

    blk = lax.broadcasted_iota(jnp.int32, (tq, nsel_pad), 1)
    cur = jnp.right_shift(i * tq + lax.broadcasted_iota(jnp.int32, (tq, nsel_pad), 0), SEL_SHIFT)
    forced = jnp.logical_or(blk == 0, jnp.logical_or(blk == cur, blk == cur - 1))
    score = jnp.where(blk <= cur, imp + jnp.where(forced, FORCE_BONUS, 0.0), NEG)

    blk_f = blk.astype(F32)

    def pick(_, carry):
        sc, sel = carry
        mx = jnp.max(sc, axis=-1, keepdims=True)
        first = jnp.min(jnp.where(sc == mx, blk_f, float(nsel_pad)), axis=-1, keepdims=True)
        hit = blk_f == first
        sel = jnp.where(jnp.logical_and(hit, mx > NEG * 0.5), 1.0, sel)
        return jnp.where(hit, REMOVED, sc), sel

    _, sel = lax.fori_loop(0, SEL_TOPK, pick, (score, jnp.zeros((tq, nsel_pad), F32)))
    sel_ref[0] = sel


def _cmp_attn(slopes, hm, kvc, nsel_pad):
    T = hm.shape[1]
    nr = kvc.shape[1]
    tq = NSA_TQ
    grid_spec = pltpu.PrefetchScalarGridSpec(
        num_scalar_prefetch=1,
        grid=(NSA_G, T // tq),
        in_specs=[
            pl.BlockSpec((NSA_HPG, tq, NSA_D), lambda g, i, sl: (g, i, 0)),
            pl.BlockSpec((1, nr, NSA_D), lambda g, i, sl: (2 * g, 0, 0)),
            pl.BlockSpec((1, nr, NSA_D), lambda g, i, sl: (2 * g + 1, 0, 0)),
        ],
        out_specs=(
            pl.BlockSpec((tq, NSA_HPG * NSA_D), lambda g, i, sl: (i, g)),
            pl.BlockSpec((1, tq, nsel_pad), lambda g, i, sl: (g, i, 0)),
        ),
    )
    return pl.pallas_call(
        functools.partial(_cmp_attn_kernel, nsel_pad=nsel_pad),
        out_shape=(
            jax.ShapeDtypeStruct((T, NSA_HEADS * NSA_D), F32),
            jax.ShapeDtypeStruct((NSA_G, T, nsel_pad), F32),
        ),
        grid_spec=grid_spec,
        compiler_params=_cparams(("parallel", "parallel")),
        name="nsa_cmp_attn",
    )(slopes, hm, kvc, kvc)


def _sel_attn_kernel(qi_ref, ki_ref, slopes_ref, q_ref, k_ref, v_ref, sel_ref, o_ref, m_scr, l_scr, acc_scr,
                     *, nsel_pad):
    g = pl.program_id(0)
    p_id = pl.program_id(1)
    i = qi_ref[p_id]
    c = ki_ref[p_id]
    tq = NSA_TQ
    tk = k_ref.shape[1]

    @pl.when(c == 0)
    def _():
        m_scr[...] = jnp.full(m_scr.shape, NEG, F32)
        l_scr[...] = jnp.zeros(l_scr.shape, F32)
        acc_scr[...] = jnp.zeros(acc_scr.shape, F32)

    qs = (q_ref[...].reshape(NSA_HPG * tq, NSA_D) * (NSA_D ** -0.5)).astype(BF16)
    s = lax.dot_general(qs, k_ref[0].astype(BF16), (((1,), (1,)), ((), ())), preferred_element_type=F32)

    jb = lax.broadcasted_iota(jnp.int32, (nsel_pad, tk), 0)
    kb = jnp.right_shift(c * tk + lax.broadcasted_iota(jnp.int32, (nsel_pad, tk), 1), SEL_SHIFT)
    expand = jnp.where(jb == kb, 1.0, 0.0).astype(BF16)
    selk = jnp.dot(sel_ref[0].astype(BF16), expand, preferred_element_type=F32)
    qpos = i * tq + lax.broadcasted_iota(jnp.int32, (tq, tk), 0)
    kpos = c * tk + lax.broadcasted_iota(jnp.int32, (tq, tk), 1)
    allowed = jnp.logical_and(selk > 0.5, kpos <= qpos)
    dist = (qpos - kpos).astype(F32)

    ps = []
    for h in range(NSA_HPG):
        rows = slice(h * tq, (h + 1) * tq)
        sh = jnp.where(allowed, s[rows, :] - slopes_ref[g * NSA_HPG + h] * dist, NEG)
        m_prev = m_scr[rows, :]
        m_new = jnp.maximum(m_prev, jnp.max(sh, axis=-1, keepdims=True))
        alpha = jnp.exp(m_prev - m_new)
        p = jnp.where(allowed, jnp.exp(sh - m_new), 0.0)
        l_scr[rows, :] = alpha * l_scr[rows, :] + jnp.sum(p, axis=-1, keepdims=True)
        acc_scr[rows, :] = alpha * acc_scr[rows, :]
        m_scr[rows, :] = m_new
        ps.append(p.astype(BF16))
    p_all = jnp.concatenate(ps, axis=0)
    acc_scr[...] += jnp.dot(p_all, v_ref[0].astype(BF16), preferred_element_type=F32)

    @pl.when(c == (i * tq + tq - 1) // tk)
    def _():
        _store_heads(o_ref, acc_scr[...] / l_scr[...])


def _sel_attn(slopes, hm, sel):
    T = hm.shape[1]
    tq = NSA_TQ
    tk = min(SEL_TK, T)
    nsel_pad = sel.shape[-1]
    qi, ki = _tri_pairs(T // tq, lambda i: (i * tq + tq - 1) // tk)
    grid_spec = pltpu.PrefetchScalarGridSpec(
        num_scalar_prefetch=3,
        grid=(NSA_G, int(qi.shape[0])),
        in_specs=[
            pl.BlockSpec((NSA_HPG, tq, NSA_D), lambda g, p, qi, ki, sl: (g, qi[p], 0)),
            pl.BlockSpec((1, tk, NSA_D), lambda g, p, qi, ki, sl: (20 + 2 * g, ki[p], 0)),
            pl.BlockSpec((1, tk, NSA_D), lambda g, p, qi, ki, sl: (21 + 2 * g, ki[p], 0)),
            pl.BlockSpec((1, tq, nsel_pad), lambda g, p, qi, ki, sl: (g, qi[p], 0)),
        ],
        out_specs=pl.BlockSpec((tq, NSA_HPG * NSA_D), lambda g, p, qi, ki, sl: (qi[p], g)),
        scratch_shapes=[
            pltpu.VMEM((NSA_HPG * tq, 1), F32),
            pltpu.VMEM((NSA_HPG * tq, 1), F32),
            pltpu.VMEM((NSA_HPG * tq, NSA_D), F32),
        ],
    )
    return pl.pallas_call(
        functools.partial(_sel_attn_kernel, nsel_pad=nsel_pad),
        out_shape=jax.ShapeDtypeStruct((T, NSA_HEADS * NSA_D), F32),
        grid_spec=grid_spec,
        compiler_params=_cparams(("parallel", "arbitrary")),
        name="nsa_sel_attn",
    )(qi, ki, slopes, hm, hm, hm, sel)


WIN_BLOCKS = WINDOW // NSA_TQ + 1


def _win_attn_kernel(slopes_ref, q_ref, *refs):
    k_refs = refs[:WIN_BLOCKS]
    v_refs = refs[WIN_BLOCKS:2 * WIN_BLOCKS]
    o_ref = refs[2 * WIN_BLOCKS]
    g = pl.program_id(0)
    i = pl.program_id(1)
    tq = NSA_TQ
    nk = WIN_BLOCKS * tq
    qs = (q_ref[...].reshape(NSA_HPG * tq, NSA_D) * (NSA_D ** -0.5)).astype(BF16)
    kcat = jnp.concatenate([r[0] for r in k_refs], axis=0).astype(BF16)
    vcat = jnp.concatenate([r[0] for r in v_refs], axis=0).astype(BF16)
    s = lax.dot_general(qs, kcat, (((1,), (1,)), ((), ())), preferred_element_type=F32)

    qpos = i * tq + lax.broadcasted_iota(jnp.int32, (tq, nk), 0)
    kpos = i * tq - WINDOW + lax.broadcasted_iota(jnp.int32, (tq, nk), 1)
    dist_i = qpos - kpos
    allowed = jnp.logical_and(jnp.logical_and(dist_i >= 0, dist_i < WINDOW), kpos >= 0)
    dist = dist_i.astype(F32)
    ps = []
    for h in range(NSA_HPG):
        sh = jnp.where(allowed, s[h * tq:(h + 1) * tq, :] - slopes_ref[g * NSA_HPG + h] * dist, NEG)
        m = jnp.max(sh, axis=-1, keepdims=True)
        e = jnp.where(allowed, jnp.exp(sh - m), 0.0)
        ps.append((e / jnp.sum(e, axis=-1, keepdims=True)).astype(BF16))
    o = jnp.dot(jnp.concatenate(ps, axis=0), vcat, preferred_element_type=F32)
    _store_heads(o_ref, o)


def _win_attn(slopes, hm):
    T = hm.shape[1]
    tq = NSA_TQ
    back = WIN_BLOCKS - 1

    def kv_spec(a0, j):
        return pl.BlockSpec((1, tq, NSA_D), lambda g, i, sl: (a0 + 2 * g, jnp.maximum(i + j - back, 0), 0))

    grid_spec = pltpu.PrefetchScalarGridSpec(
        num_scalar_prefetch=1,
        grid=(NSA_G, T // tq),
        in_specs=[pl.BlockSpec((NSA_HPG, tq, NSA_D), lambda g, i, sl: (g, i, 0))]
        + [kv_spec(24, j) for j in range(WIN_BLOCKS)]
        + [kv_spec(25, j) for j in range(WIN_BLOCKS)],
        out_specs=pl.BlockSpec((tq, NSA_HPG * NSA_D), lambda g, i, sl: (i, g)),
    )
    return pl.pallas_call(
        _win_attn_kernel,
        out_shape=jax.ShapeDtypeStruct((T, NSA_HEADS * NSA_D), F32),
        grid_spec=grid_spec,
        compiler_params=_cparams(("parallel", "parallel")),
        name="nsa_win_attn",
    )(slopes, hm, *([hm] * (2 * WIN_BLOCKS)))


def _out_proj_kernel(x_ref, ym_ref, oc_ref, os_ref, ow_ref, gl_ref, ex_ref, mnw_ref, nnw_ref, wo_ref, o_ref):
    gates = 1.0 / (1.0 + jnp.exp(-gl_ref[...]))
    hp = lax.Precision.HIGHEST
    y_nsa = (jnp.dot(gates, ex_ref[0], precision=hp, preferred_element_type=F32) * oc_ref[...]
             + jnp.dot(gates, ex_ref[1], precision=hp, preferred_element_type=F32) * os_ref[...]
             + jnp.dot(gates, ex_ref[2], precision=hp, preferred_element_type=F32) * ow_ref[...])
    half = MLA_HEADS * MLA_V
    ym = _rms(ym_ref[...], mnw_ref[...]).astype(BF16)
    yn = _rms(y_nsa, nnw_ref[...]).astype(BF16)
    o_ref[...] = (x_ref[...]
                  + jnp.dot(ym, wo_ref[:half, :], preferred_element_type=F32)
                  + jnp.dot(yn, wo_ref[half:, :], preferred_element_type=F32))


def _out_proj(x2, y_mla, o_c, o_s, o_w, proj, expand, mnw, nnw, wo, tm=256):
    T = x2.shape[0]
    half = MLA_HEADS * MLA_V
    row = lambda i: (i, 0)
    fixed2 = lambda i: (0, 0)
    return pl.pallas_call(
        _out_proj_kernel,
        out_shape=jax.ShapeDtypeStruct((T, D_MODEL), F32),
        grid=(T // tm,),
        in_specs=[
            pl.BlockSpec((tm, D_MODEL), row),
            pl.BlockSpec((tm, half), row),
            pl.BlockSpec((tm, half), row),
            pl.BlockSpec((tm, half), row),
            pl.BlockSpec((tm, half), row),
            pl.BlockSpec((tm, LANES), lambda i: (i, SEC_GATE // LANES)),
            pl.BlockSpec((3, LANES, half), lambda i: (0, 0, 0)),
            pl.BlockSpec((1, half), fixed2),
            pl.BlockSpec((1, half), fixed2),
            pl.BlockSpec((2 * half, D_MODEL), fixed2),
        ],
        out_specs=pl.BlockSpec((tm, D_MODEL), row),
        compiler_params=_cparams(("parallel",)),
        name="out_proj",
    )(x2, y_mla, o_c, o_s, o_w, proj, expand, mnw.reshape(1, -1), nnw.reshape(1, -1), wo)


def _mlp_kernel(h_ref, nw_ref, wu_ref, wd_ref, fw_ref, o_ref, n_scr, acc_scr, *, final):
    c = pl.program_id(1)

    @pl.when(c == 0)
    def _():
        n_scr[...] = _rms(h_ref[...], nw_ref[...]).astype(BF16)
        acc_scr[...] = jnp.zeros(acc_scr.shape, F32)

    u = jnp.dot(n_scr[...], wu_ref[...], preferred_element_type=F32)
    a = jnp.square(jnp.maximum(u, 0.0)).astype(BF16)
    acc_scr[...] += jnp.dot(a, wd_ref[...], preferred_element_type=F32)

    @pl.when(c == pl.num_programs(1) - 1)
    def _():
        h2 = h_ref[...] + acc_scr[...]
        o_ref[...] = _rms(h2, fw_ref[...]) if final else h2


def _mlp(h1, nw, wu, wd, fw, final, tm=512, tc=1024):
    T = h1.shape[0]
    tm = min(tm, T)
    return pl.pallas_call(
        functools.partial(_mlp_kernel, final=final),
        out_shape=jax.ShapeDtypeStruct((T, D_MODEL), F32),
        grid=(T // tm, MLP_HIDDEN // tc),
        in_specs=[
            pl.BlockSpec((tm, D_MODEL), lambda i, c: (i, 0)),
            pl.BlockSpec((1, D_MODEL), lambda i, c: (0, 0)),
            pl.BlockSpec((D_MODEL, tc), lambda i, c: (0, c)),
            pl.BlockSpec((tc, D_MODEL), lambda i, c: (c, 0)),
            pl.BlockSpec((1, D_MODEL), lambda i, c: (0, 0)),
        ],
        out_specs=pl.BlockSpec((tm, D_MODEL), lambda i, c: (i, 0)),
        scratch_shapes=[pltpu.VMEM((tm, D_MODEL), BF16), pltpu.VMEM((tm, D_MODEL), F32)],
        compiler_params=_cparams(("parallel", "arbitrary")),
        name="mlp",
    )(h1, nw.reshape(1, -1), wu, wd, fw.reshape(1, -1))


def _regroup_w_in(w_in):
    cq = w_in[:, 0:512]
    ckv = w_in[:, 512:768]
    kr = w_in[:, 768:832]
    rest = w_in[:, 832:2624]
    gates = w_in[:, 2624:2672]
    x1, x2 = kr[:, :32], kr[:, 32:]
    zeros = jnp.zeros((w_in.shape[0], IN_PAD - SEC_GATE - 48), w_in.dtype)
    return jnp.concatenate([cq, ckv, rest, x1, x2, x2, x1, gates, zeros], axis=1).astype(BF16)


def _regroup_w_uq(w_uq):
    w = w_uq.reshape(MLA_Q_RANK, MLA_HEADS, MLA_NOPE + MLA_ROPE)
    nope = w[:, :, :MLA_NOPE]
    x1 = w[:, :, MLA_NOPE:MLA_NOPE + 32]
    x2 = w[:, :, MLA_NOPE + 32:]
    return jnp.concatenate([nope, x1, x2, x2, x1], axis=-1).reshape(MLA_Q_RANK, MLA_HEADS * MLA_QK_PAD).astype(BF16)


def _gate_expand():
    ex = np.zeros((3, LANES, NSA_HEADS * NSA_D), np.float32)
    for b in range(3):
        for h in range(NSA_HEADS):
            ex[b, 3 * h + b, h * NSA_D:(h + 1) * NSA_D] = 1.0
    return jnp.asarray(ex)


def kernel(x, positions, attn_norm_w, w_in, mla_q_norm_w, mla_w_uq, mla_kv_norm_w, mla_w_ukv, cmp_pos_k,
           cmp_pos_v, cmp_w1_k, cmp_w2_k, cmp_w1_v, cmp_w2_v, mla_out_norm_w, nsa_out_norm_w, w_o, mlp_norm_w,
           w_up, w_down, final_norm_w):
    B, T, _ = x.shape
    depth = w_in.shape[0]
    inv = 1.0 / (ROPE_THETA ** (jnp.arange(0, MLA_ROPE, 2, dtype=F32) / MLA_ROPE))
    inv128 = jnp.tile(inv, 4).reshape(1, LANES)
    start = 2.0 ** (-8.0 / NSA_HEADS)
    slopes = start ** jnp.arange(1, NSA_HEADS + 1, dtype=F32)
    expand = _gate_expand()
    nsel_pad = -(-(T // SEL_LEN) // LANES) * LANES
    n_slabs = (SEC_ROPE - SEC_HM) // NSA_D

    outs = []
    for b in range(B):
        h = x[b]
        pos2 = positions[b].reshape(T, 1)
        for l in range(depth):
            proj = _in_proj(h, attn_norm_w[l], _regroup_w_in(w_in[l]))
            q, k, v = _mla_prep(proj, pos2, mla_q_norm_w[l], mla_kv_norm_w[l], _regroup_w_uq(mla_w_uq[l]),
                                mla_w_ukv[l].astype(BF16), inv128)
            y_mla = _mla_attn(q, k, v)

            hm = proj[:, SEC_HM:SEC_ROPE].reshape(T, n_slabs, NSA_D).transpose(1, 0, 2)
            r4 = hm[16:20].reshape(4, T // CMP_STRIDE, CMP_STRIDE * NSA_D)
            half = CMP_STRIDE * NSA_D
            pos_kv = jnp.stack([cmp_pos_k[l].reshape(2, half), cmp_pos_v[l].reshape(2, half)])
            kvc = _compress(r4, pos_kv, jnp.stack([cmp_w1_k[l], cmp_w1_v[l]]),
                            jnp.stack([cmp_w2_k[l], cmp_w2_v[l]]))
            o_c, sel = _cmp_attn(slopes, hm, kvc, nsel_pad)
            o_s = _sel_attn(slopes, hm, sel)
            o_w = _win_attn(slopes, hm)

            h = _out_proj(h, y_mla, o_c, o_s, o_w, proj, expand, mla_out_norm_w[l], nsa_out_norm_w[l],
                          w_o[l].astype(BF16))
            h = _mlp(h, mlp_norm_w[l], w_up[l].astype(BF16), w_down[l].astype(BF16), final_norm_w,
                     final=(l == depth - 1))
        outs.append(h)
    return jnp.stack(outs)
```

```python
import functools

import numpy as np
import jax
import jax.numpy as jnp
from jax import lax
from jax.experimental import pallas as pl
from jax.experimental.pallas import tpu as pltpu

F32 = jnp.float32
BF16 = jnp.bfloat16

EPS = 1e-6
NEG = -1e30
REMOVED = -3e38

D_MODEL = 2048
MLA_HEADS = 8
MLA_Q_RANK = 512
MLA_KV_RANK = 256
MLA_NOPE = 128
MLA_ROPE = 64
MLA_V = 128
ROPE_THETA = 10000.0
MLA_QK_PAD = 256

NSA_HEADS = 16
NSA_G = 2
NSA_HPG = 8
NSA_D = 64
CMP_LEN = 32
CMP_STRIDE = 16
CMP_HIDDEN = 256
SEL_LEN = 64
SEL_SHIFT = 6
SEL_TOPK = 16
WINDOW = 512
FORCE_BONUS = 1e4
MLP_HIDDEN = 4 * D_MODEL

LANES = 128
NSA_TQ = 128
SEL_TK = 512
MLA_TQ = 512
MLA_TK = 512
VMEM_LIMIT = 56 * 1024 * 1024

SEC_CQ = 0
SEC_CKV = 512
SEC_HM = 768
SEC_ROPE = 2560
SEC_GATE = 2688
IN_PAD = 2816


def _cparams(sem):
    return pltpu.CompilerParams(dimension_semantics=sem, vmem_limit_bytes=VMEM_LIMIT)


def _rms(x, w):
    return x * lax.rsqrt(jnp.mean(x * x, axis=-1, keepdims=True) + EPS) * w


def _in_proj_kernel(x_ref, nw_ref, w_ref, o_ref):
    n = _rms(x_ref[...], nw_ref[...]).astype(BF16)
    o_ref[...] = jnp.dot(n, w_ref[...], preferred_element_type=F32)


def _in_proj(x2, attn_norm_w, w_in_p, tm=512):
    T = x2.shape[0]
    return pl.pallas_call(
        _in_proj_kernel,
        out_shape=jax.ShapeDtypeStruct((T, IN_PAD), F32),
        grid=(T // tm,),
        in_specs=[
            pl.BlockSpec((tm, D_MODEL), lambda i: (i, 0)),
            pl.BlockSpec((1, D_MODEL), lambda i: (0, 0)),
            pl.BlockSpec((D_MODEL, IN_PAD), lambda i: (0, 0)),
        ],
        out_specs=pl.BlockSpec((tm, IN_PAD), lambda i: (i, 0)),
        compiler_params=_cparams(("parallel",)),
        name="in_proj",
    )(x2, attn_norm_w.reshape(1, D_MODEL), w_in_p)


NT_DIMS = (((1,), (1,)), ((), ()))
TN_DIMS = (((0,), (0,)), ((), ()))


def _mla_prep_kernel(cq_ref, ckv_ref, kr_ref, pos_ref, qnw_ref, kvnw_ref, wuqT_ref, wuk_ref, wuvT_ref, inv_ref,
                     qT_ref, k_ref, vT_ref):
    scale = (MLA_NOPE + MLA_ROPE) ** -0.5
    qn = _rms(cq_ref[...], qnw_ref[...]).astype(BF16)
    qT = lax.dot_general(wuqT_ref[...], qn, NT_DIMS, preferred_element_type=F32)
    kvn = _rms(ckv_ref[...], kvnw_ref[...]).astype(BF16)
    kn = jnp.dot(kvn, wuk_ref[...], preferred_element_type=F32)
    vT = lax.dot_general(wuvT_ref[...], kvn, NT_DIMS, preferred_element_type=F32)

    ang = pos_ref[...].astype(F32) * inv_ref[...]
    c = jnp.cos(ang)
    s = jnp.sin(ang)
    lane = lax.broadcasted_iota(jnp.int32, ang.shape, 1)
    coef = jnp.where(lane < 64, c, jnp.where(lane < 96, -s, s))
    coefT = coef.T

    t = kr_ref[...] * coef
    kpe = jnp.where(lane < 64, t + pltpu.roll(t, 64, 1), 0.0)
    for h in range(MLA_HEADS):
        base = h * MLA_QK_PAD
        tT = qT[base + 128:base + 256, :] * coefT
        ropeT = tT + pltpu.roll(tT, 64, 0)
        qT_ref[h] = (jnp.concatenate([qT[base:base + 128, :], ropeT], axis=0) * scale).astype(BF16)
        k_ref[h] = jnp.concatenate([kn[:, h * 128:(h + 1) * 128], kpe], axis=-1).astype(BF16)
        vT_ref[h] = vT[h * 128:(h + 1) * 128, :].astype(BF16)


def _mla_prep(proj, pos2, qnw, kvnw, wuqT, wuk, wuvT, inv128, tm=256):
    T = proj.shape[0]
    H = MLA_HEADS
    fixed = lambda i: (0, 0)
    return pl.pallas_call(
        _mla_prep_kernel,
        out_shape=(
            jax.ShapeDtypeStruct((H, MLA_QK_PAD, T), BF16),
            jax.ShapeDtypeStruct((H, T, MLA_QK_PAD), BF16),
            jax.ShapeDtypeStruct((H, MLA_V, T), BF16),
        ),
        grid=(T // tm,),
        in_specs=[
            pl.BlockSpec((tm, MLA_Q_RANK), lambda i: (i, SEC_CQ // MLA_Q_RANK)),
            pl.BlockSpec((tm, MLA_KV_RANK), lambda i: (i, SEC_CKV // MLA_KV_RANK)),
            pl.BlockSpec((tm, LANES), lambda i: (i, SEC_ROPE // LANES)),
            pl.BlockSpec((tm, 1), lambda i: (i, 0)),
            pl.BlockSpec((1, MLA_Q_RANK), fixed),
            pl.BlockSpec((1, MLA_KV_RANK), fixed),
            pl.BlockSpec((H * MLA_QK_PAD, MLA_Q_RANK), fixed),
            pl.BlockSpec((MLA_KV_RANK, H * MLA_NOPE), fixed),
            pl.BlockSpec((H * MLA_V, MLA_KV_RANK), fixed),
            pl.BlockSpec((1, LANES), fixed),
        ],
        out_specs=(
            pl.BlockSpec((H, MLA_QK_PAD, tm), lambda i: (0, 0, i)),
            pl.BlockSpec((H, tm, MLA_QK_PAD), lambda i: (0, i, 0)),
            pl.BlockSpec((H, MLA_V, tm), lambda i: (0, 0, i)),
        ),
        compiler_params=_cparams(("parallel",)),
        name="mla_prep",
    )(proj, proj, proj, pos2, qnw.reshape(1, -1), kvnw.reshape(1, -1), wuqT, wuk, wuvT, inv128)


def _mla_attn_kernel(qi_ref, ki_ref, qT_ref, k_ref, vT_ref, o_ref, m_scr, l_scr, acc_scr):
    p_id = pl.program_id(0)
    qi = qi_ref[p_id]
    ki = ki_ref[p_id]

    @pl.when(ki == 0)
    def _():
        m_scr[...] = jnp.full(m_scr.shape, NEG, F32)
        l_scr[...] = jnp.zeros(l_scr.shape, F32)
        acc_scr[...] = jnp.zeros(acc_scr.shape, F32)

    def step(diag):
        for h in range(MLA_HEADS):
            s = jnp.dot(k_ref[h], qT_ref[h], preferred_element_type=F32)
            if diag:
                key = lax.broadcasted_iota(jnp.int32, s.shape, 0)
                qry = lax.broadcasted_iota(jnp.int32, s.shape, 1)
                s = jnp.where(key <= qry, s, NEG)
            m_prev = m_scr[h]
            m_new = jnp.maximum(m_prev, jnp.max(s, axis=0, keepdims=True))
            alpha = jnp.exp(m_prev - m_new)
            p = jnp.exp(s - m_new)
            l_new = alpha * l_scr[h] + jnp.sum(p, axis=0, keepdims=True)
            acc = alpha * acc_scr[h] + jnp.dot(vT_ref[h], p.astype(BF16), preferred_element_type=F32)
            if diag:
                o_ref[h * MLA_V:(h + 1) * MLA_V, :] = acc * (1.0 / l_new)
            else:
                m_scr[h] = m_new
                l_scr[h] = l_new
                acc_scr[h] = acc

    @pl.when(ki < qi)
    def _():
        step(False)

    @pl.when(ki == qi)
    def _():
        step(True)


def _tri_pairs(nq, per_q):
    qi, ki = [], []
    for i in range(nq):
        for c in range(per_q(i) + 1):
            qi.append(i)
            ki.append(c)
    return jnp.asarray(qi, jnp.int32), jnp.asarray(ki, jnp.int32)


def _mla_attn(qT, k, vT):
    H, T, _ = k.shape
    tq = min(MLA_TQ, T)
    nq = T // tq
    qi, ki = _tri_pairs(nq, lambda i: i)
    grid_spec = pltpu.PrefetchScalarGridSpec(
        num_scalar_prefetch=2,
        grid=(int(qi.shape[0]),),
        in_specs=[
            pl.BlockSpec((H, MLA_QK_PAD, tq), lambda p, qi, ki: (0, 0, qi[p])),
            pl.BlockSpec((H, tq, MLA_QK_PAD), lambda p, qi, ki: (0, ki[p], 0)),
            pl.BlockSpec((H, MLA_V, tq), lambda p, qi, ki: (0, 0, ki[p])),
        ],
        out_specs=pl.BlockSpec((H * MLA_V, tq), lambda p, qi, ki: (0, qi[p])),
        scratch_shapes=[
            pltpu.VMEM((H, 1, tq), F32),
            pltpu.VMEM((H, 1, tq), F32),
            pltpu.VMEM((H, MLA_V, tq), F32),
        ],
    )
    return pl.pallas_call(
        _mla_attn_kernel,
        out_shape=jax.ShapeDtypeStruct((H * MLA_V, T), F32),
        grid_spec=grid_spec,
        compiler_params=_cparams(("arbitrary",)),
        name="mla_attn",
    )(qi, ki, qT, k, vT)


def _gelu_tanh(x):
    return 0.5 * x * (1.0 + jnp.tanh(np.sqrt(2.0 / np.pi).astype(np.float32) * (x + 0.044715 * (x * x * x))))


def _compress_kernel(r_ref, pos_ref, w1_ref, w2_ref, o_ref):
    r = r_ref[0]
    nr = r.shape[0]
    half = CMP_STRIDE * NSA_D
    hp = lax.Precision.HIGHEST
    a = jnp.dot(r + pos_ref[0, 0:1, :], w1_ref[0, :half, :], precision=hp, preferred_element_type=F32)
    b = jnp.dot(r + pos_ref[0, 1:2, :], w1_ref[0, half:, :], precision=hp, preferred_element_type=F32)
    hid = _gelu_tanh(a + pltpu.roll(b, nr - 1, 0))
    out = jnp.dot(hid, w2_ref[0], precision=hp, preferred_element_type=F32)
    row = lax.broadcasted_iota(jnp.int32, out.shape, 0)
    o_ref[0] = jnp.where(row < nr - 1, out, 0.0)


def _compress(r4, pos_kv, w1_kv, w2_kv):
    _, nr, width = r4.shape
    return pl.pallas_call(
        _compress_kernel,
        out_shape=jax.ShapeDtypeStruct((4, nr, NSA_D), F32),
        grid=(4,),
        in_specs=[
            pl.BlockSpec((1, nr, width), lambda a: (a, 0, 0)),
            pl.BlockSpec((1, 2, width), lambda a: (a % 2, 0, 0)),
            pl.BlockSpec((1, 2 * width, CMP_HIDDEN), lambda a: (a % 2, 0, 0)),
            pl.BlockSpec((1, CMP_HIDDEN, NSA_D), lambda a: (a % 2, 0, 0)),
        ],
        out_specs=pl.BlockSpec((1, nr, NSA_D), lambda a: (a, 0, 0)),
        compiler_params=_cparams(("parallel",)),
        name="nsa_compress",
    )(r4, pos_kv, w1_kv, w2_kv)


def _store_heads(o_ref, o):
    tq = o.shape[0] // NSA_HPG
    for h in range(NSA_HPG):
        o_ref[:, h * NSA_D:(h + 1) * NSA_D] = o[h * tq:(h + 1) * tq, :]


def _cmp_attn_kernel(slopes_ref, q_ref, kc_ref, vc_ref, o_ref, sel_ref, *, nsel_pad):
    g = pl.program_id(0)
    i = pl.program_id(1)
    tq = NSA_TQ
    nr = kc_ref.shape[1]
    qs = (q_ref[...].reshape(NSA_HPG * tq, NSA_D) * (NSA_D ** -0.5)).astype(BF16)
    s = lax.dot_general(qs, kc_ref[0].astype(BF16), (((1,), (1,)), ((), ())), preferred_element_type=F32)

    qpos = i * tq + lax.broadcasted_iota(jnp.int32, (tq, nr), 0)
    n_id = lax.broadcasted_iota(jnp.int32, (tq, nr), 1)
    dist_i = qpos - (n_id * CMP_STRIDE + (CMP_LEN - 1))
    valid = jnp.logical_and(dist_i >= 0, n_id < nr - 1)
    dist = dist_i.astype(F32)

    psum = jnp.zeros((tq, nr), F32)
    outs = []
    for h in range(NSA_HPG):
        sh = s[h * tq:(h + 1) * tq, :] - slopes_ref[g * NSA_HPG + h] * dist
        sh = jnp.where(valid, sh, NEG)
        m = jnp.max(sh, axis=-1, keepdims=True)
        e = jnp.where(valid, jnp.exp(sh - m), 0.0)
        l = jnp.sum(e, axis=-1, keepdims=True)
        p = e / jnp.where(l > 0.0, l, 1.0)
        psum = psum + p
        outs.append(p.astype(BF16))
    p_all = jnp.concatenate(outs, axis=0)
    o = jnp.dot(p_all, vc_ref[0].astype(BF16), preferred_element_type=F32)
    _store_heads(o_ref, o)

    n2 = lax.broadcasted_iota(jnp.int32, (nr, nsel_pad), 0)
    j2 = lax.broadcasted_iota(jnp.int32, (nr, nsel_pad), 1)
    ov = jnp.logical_and(n2 * CMP_STRIDE <= j2 * SEL_LEN + (SEL_LEN - 1),
                         n2 * CMP_STRIDE + (CMP_LEN - 1) >= j2 * SEL_LEN)
    ov = jnp.logical_and(ov, n2 < nr - 1)
    imp = jnp.dot(psum, jnp.where(ov, 1.0, 0.0).astype(F32), precision=lax.Precision.HIGHEST,
                  preferred_element_type=F32)

    blk = lax.broadcasted_iota(jnp.int32, (tq, nsel_pad), 1)
    cur = jnp.right_shift(i * tq + lax.broadcasted_iota(jnp.int32, (tq, nsel_pad), 0), SEL_SHIFT)
    forced = jnp.logical_or(blk == 0, jnp.logical_or(blk == cur, blk == cur - 1))
    score = jnp.where(blk <= cur, imp + jnp.where(forced, FORCE_BONUS, 0.0), NEG)

    blk_f = blk.astype(F32)

    def pick(_, carry):
        sc, sel = carry
        mx = jnp.max(sc, axis=-1, keepdims=True)
        first = jnp.min(jnp.where(sc == mx, blk_f, float(nsel_pad)), axis=-1, keepdims=True)
        hit = blk_f == first
        sel = jnp.where(jnp.logical_and(hit, mx > NEG * 0.5), 1.0, sel)
        return jnp.where(hit, REMOVED, sc), sel

    _, sel = lax.fori_loop(0, SEL_TOPK, pick, (score, jnp.zeros((tq, nsel_pad), F32)))
    sel_ref[0] = sel


def _cmp_attn(slopes, hm, kvc, nsel_pad):
    T = hm.shape[1]
    nr = kvc.shape[1]
    tq = NSA_TQ
    grid_spec = pltpu.PrefetchScalarGridSpec(
        num_scalar_prefetch=1,
        grid=(NSA_G, T // tq),
        in_specs=[
            pl.BlockSpec((NSA_HPG, tq, NSA_D), lambda g, i, sl: (g, i, 0)),
            pl.BlockSpec((1, nr, NSA_D), lambda g, i, sl: (2 * g, 0, 0)),
            pl.BlockSpec((1, nr, NSA_D), lambda g, i, sl: (2 * g + 1, 0, 0)),
        ],
        out_specs=(
            pl.BlockSpec((tq, NSA_HPG * NSA_D), lambda g, i, sl: (i, g)),
            pl.BlockSpec((1, tq, nsel_pad), lambda g, i, sl: (g, i, 0)),
        ),
    )
    return pl.pallas_call(
        functools.partial(_cmp_attn_kernel, nsel_pad=nsel_pad),
        out_shape=(
            jax.ShapeDtypeStruct((T, NSA_HEADS * NSA_D), F32),
            jax.ShapeDtypeStruct((NSA_G, T, nsel_pad), F32),
        ),
        grid_spec=grid_spec,
        compiler_params=_cparams(("parallel", "parallel")),
        name="nsa_cmp_attn",
    )(slopes, hm, kvc, kvc)


def _sel_attn_kernel(qi_ref, ki_ref, slopes_ref, q_ref, k_ref, v_ref, sel_ref, o_ref, m_scr, l_scr, acc_scr,
                     *, nsel_pad):
    g = pl.program_id(0)
    p_id = pl.program_id(1)
    i = qi_ref[p_id]
    c = ki_ref[p_id]
    tq = NSA_TQ
    tk = k_ref.shape[1]

    @pl.when(c == 0)
    def _():
        m_scr[...] = jnp.full(m_scr.shape, NEG, F32)
        l_scr[...] = jnp.zeros(l_scr.shape, F32)
        acc_scr[...] = jnp.zeros(acc_scr.shape, F32)

    qs = (q_ref[...].reshape(NSA_HPG * tq, NSA_D) * (NSA_D ** -0.5)).astype(BF16)
    s = lax.dot_general(qs, k_ref[0].astype(BF16), (((1,), (1,)), ((), ())), preferred_element_type=F32)

    jb = lax.broadcasted_iota(jnp.int32, (nsel_pad, tk), 0)
    kb = jnp.right_shift(c * tk + lax.broadcasted_iota(jnp.int32, (nsel_pad, tk), 1), SEL_SHIFT)
    expand = jnp.where(jb == kb, 1.0, 0.0).astype(BF16)
    selk = jnp.dot(sel_ref[0].astype(BF16), expand, preferred_element_type=F32)
    qpos = i * tq + lax.broadcasted_iota(jnp.int32, (tq, tk), 0)
    kpos = c * tk + lax.broadcasted_iota(jnp.int32, (tq, tk), 1)
    allowed = jnp.logical_and(selk > 0.5, kpos <= qpos)
    dist = (qpos - kpos).astype(F32)

    ps = []
    for h in range(NSA_HPG):
        rows = slice(h * tq, (h + 1) * tq)
        sh = jnp.where(allowed, s[rows, :] - slopes_ref[g * NSA_HPG + h] * dist, NEG)
        m_prev = m_scr[rows, :]
        m_new = jnp.maximum(m_prev, jnp.max(sh, axis=-1, keepdims=True))
        alpha = jnp.exp(m_prev - m_new)
        p = jnp.where(allowed, jnp.exp(sh - m_new), 0.0)
        l_scr[rows, :] = alpha * l_scr[rows, :] + jnp.sum(p, axis=-1, keepdims=True)
        acc_scr[rows, :] = alpha * acc_scr[rows, :]
        m_scr[rows, :] = m_new
        ps.append(p.astype(BF16))
    p_all = jnp.concatenate(ps, axis=0)
    acc_scr[...] += jnp.dot(p_all, v_ref[0].astype(BF16), preferred_element_type=F32)

    @pl.when(c == (i * tq + tq - 1) // tk)
    def _():
        _store_heads(o_ref, acc_scr[...] / l_scr[...])


def _sel_attn(slopes, hm, sel):
    T = hm.shape[1]
    tq = NSA_TQ
    tk = min(SEL_TK, T)
    nsel_pad = sel.shape[-1]
    qi, ki = _tri_pairs(T // tq, lambda i: (i * tq + tq - 1) // tk)
    grid_spec = pltpu.PrefetchScalarGridSpec(
        num_scalar_prefetch=3,
        grid=(NSA_G, int(qi.shape[0])),
        in_specs=[
            pl.BlockSpec((NSA_HPG, tq, NSA_D), lambda g, p, qi, ki, sl: (g, qi[p], 0)),
            pl.BlockSpec((1, tk, NSA_D), lambda g, p, qi, ki, sl: (20 + 2 * g, ki[p], 0)),
            pl.BlockSpec((1, tk, NSA_D), lambda g, p, qi, ki, sl: (21 + 2 * g, ki[p], 0)),
            pl.BlockSpec((1, tq, nsel_pad), lambda g, p, qi, ki, sl: (g, qi[p], 0)),
        ],
        out_specs=pl.BlockSpec((tq, NSA_HPG * NSA_D), lambda g, p, qi, ki, sl: (qi[p], g)),
        scratch_shapes=[
            pltpu.VMEM((NSA_HPG * tq, 1), F32),
            pltpu.VMEM((NSA_HPG * tq, 1), F32),
            pltpu.VMEM((NSA_HPG * tq, NSA_D), F32),
        ],
    )
    return pl.pallas_call(
        functools.partial(_sel_attn_kernel, nsel_pad=nsel_pad),
        out_shape=jax.ShapeDtypeStruct((T, NSA_HEADS * NSA_D), F32),
        grid_spec=grid_spec,
        compiler_params=_cparams(("parallel", "arbitrary")),
        name="nsa_sel_attn",
    )(qi, ki, slopes, hm, hm, hm, sel)


WIN_BLOCKS = WINDOW // NSA_TQ + 1


def _win_attn_kernel(slopes_ref, q_ref, *refs):
    k_refs = refs[:WIN_BLOCKS]
    v_refs = refs[WIN_BLOCKS:2 * WIN_BLOCKS]
    o_ref = refs[2 * WIN_BLOCKS]
    g = pl.program_id(0)
    i = pl.program_id(1)
    tq = NSA_TQ
    nk = WIN_BLOCKS * tq
    qs = (q_ref[...].reshape(NSA_HPG * tq, NSA_D) * (NSA_D ** -0.5)).astype(BF16)
    kcat = jnp.concatenate([r[0] for r in k_refs], axis=0).astype(BF16)
    vcat = jnp.concatenate([r[0] for r in v_refs], axis=0).astype(BF16)
    s = lax.dot_general(qs, kcat, (((1,), (1,)), ((), ())), preferred_element_type=F32)

    qpos = i * tq + lax.broadcasted_iota(jnp.int32, (tq, nk), 0)
    kpos = i * tq - WINDOW + lax.broadcasted_iota(jnp.int32, (tq, nk), 1)
    dist_i = qpos - kpos
    allowed = jnp.logical_and(jnp.logical_and(dist_i >= 0, dist_i < WINDOW), kpos >= 0)
    dist = dist_i.astype(F32)
    ps = []
    for h in range(NSA_HPG):
        sh = jnp.where(allowed, s[h * tq:(h + 1) * tq, :] - slopes_ref[g * NSA_HPG + h] * dist, NEG)
        m = jnp.max(sh, axis=-1, keepdims=True)
        e = jnp.where(allowed, jnp.exp(sh - m), 0.0)
        ps.append((e / jnp.sum(e, axis=-1, keepdims=True)).astype(BF16))
    o = jnp.dot(jnp.concatenate(ps, axis=0), vcat, preferred_element_type=F32)
    _store_heads(o_ref, o)


def _win_attn(slopes, hm):
    T = hm.shape[1]
    tq = NSA_TQ
    back = WIN_BLOCKS - 1

    def kv_spec(a0, j):
        return pl.BlockSpec((1, tq, NSA_D), lambda g, i, sl: (a0 + 2 * g, jnp.maximum(i + j - back, 0), 0))

    grid_spec = pltpu.PrefetchScalarGridSpec(
        num_scalar_prefetch=1,
        grid=(NSA_G, T // tq),
        in_specs=[pl.BlockSpec((NSA_HPG, tq, NSA_D), lambda g, i, sl: (g, i, 0))]
        + [kv_spec(24, j) for j in range(WIN_BLOCKS)]
        + [kv_spec(25, j) for j in range(WIN_BLOCKS)],
        out_specs=pl.BlockSpec((tq, NSA_HPG * NSA_D), lambda g, i, sl: (i, g)),
    )
    return pl.pallas_call(
        _win_attn_kernel,
        out_shape=jax.ShapeDtypeStruct((T, NSA_HEADS * NSA_D), F32),
        grid_spec=grid_spec,
        compiler_params=_cparams(("parallel", "parallel")),
        name="nsa_win_attn",
    )(slopes, hm, *([hm] * (2 * WIN_BLOCKS)))


def _out_proj_kernel(x_ref, ymT_ref, oc_ref, os_ref, ow_ref, gl_ref, ex_ref, mnw_ref, nnw_ref, wo_ref, o_ref):
    gates = 1.0 / (1.0 + jnp.exp(-gl_ref[...]))
    hp = lax.Precision.HIGHEST
    y_nsa = (jnp.dot(gates, ex_ref[0], precision=hp, preferred_element_type=F32) * oc_ref[...]
             + jnp.dot(gates, ex_ref[1], precision=hp, preferred_element_type=F32) * os_ref[...]
             + jnp.dot(gates, ex_ref[2], precision=hp, preferred_element_type=F32) * ow_ref[...])
    half = MLA_HEADS * MLA_V
    ymT = ymT_ref[...]
    ymT = (ymT * lax.rsqrt(jnp.mean(ymT * ymT, axis=0, keepdims=True) + EPS) * mnw_ref[...]).astype(BF16)
    yn = _rms(y_nsa, nnw_ref[...]).astype(BF16)
    o_ref[...] = (x_ref[...]
                  + lax.dot_general(ymT, wo_ref[:half, :], TN_DIMS, preferred_element_type=F32)
                  + jnp.dot(yn, wo_ref[half:, :], preferred_element_type=F32))


def _out_proj(x2, y_mlaT, o_c, o_s, o_w, proj, expand, mnw, nnw, wo, tm=256):
    T = x2.shape[0]
    half = MLA_HEADS * MLA_V
    row = lambda i: (i, 0)
    fixed2 = lambda i: (0, 0)
    return pl.pallas_call(
        _out_proj_kernel,
        out_shape=jax.ShapeDtypeStruct((T, D_MODEL), F32),
        grid=(T // tm,),
        in_specs=[
            pl.BlockSpec((tm, D_MODEL), row),
            pl.BlockSpec((half, tm), lambda i: (0, i)),
            pl.BlockSpec((tm, half), row),
            pl.BlockSpec((tm, half), row),
            pl.BlockSpec((tm, half), row),
            pl.BlockSpec((tm, LANES), lambda i: (i, SEC_GATE // LANES)),
            pl.BlockSpec((3, LANES, half), lambda i: (0, 0, 0)),
            pl.BlockSpec((half, 1), fixed2),
            pl.BlockSpec((1, half), fixed2),
            pl.BlockSpec((2 * half, D_MODEL), fixed2),
        ],
        out_specs=pl.BlockSpec((tm, D_MODEL), row),
        compiler_params=_cparams(("parallel",)),
        name="out_proj",
    )(x2, y_mlaT, o_c, o_s, o_w, proj, expand, mnw.reshape(-1, 1), nnw.reshape(1, -1), wo)


def _mlp_kernel(h_ref, nw_ref, wu_ref, wd_ref, fw_ref, o_ref, n_scr, acc_scr, *, final):
    c = pl.program_id(1)

    @pl.when(c == 0)
    def _():
        n_scr[...] = _rms(h_ref[...], nw_ref[...]).astype(BF16)
        acc_scr[...] = jnp.zeros(acc_scr.shape, F32)

    u = jnp.dot(n_scr[...], wu_ref[...], preferred_element_type=F32)
    a = jnp.square(jnp.maximum(u, 0.0)).astype(BF16)
    acc_scr[...] += jnp.dot(a, wd_ref[...], preferred_element_type=F32)

    @pl.when(c == pl.num_programs(1) - 1)
    def _():
        h2 = h_ref[...] + acc_scr[...]
        o_ref[...] = _rms(h2, fw_ref[...]) if final else h2


def _mlp(h1, nw, wu, wd, fw, final, tm=512, tc=1024):
    T = h1.shape[0]
    tm = min(tm, T)
    return pl.pallas_call(
        functools.partial(_mlp_kernel, final=final),
        out_shape=jax.ShapeDtypeStruct((T, D_MODEL), F32),
        grid=(T // tm, MLP_HIDDEN // tc),
        in_specs=[
            pl.BlockSpec((tm, D_MODEL), lambda i, c: (i, 0)),
            pl.BlockSpec((1, D_MODEL), lambda i, c: (0, 0)),
            pl.BlockSpec((D_MODEL, tc), lambda i, c: (0, c)),
            pl.BlockSpec((tc, D_MODEL), lambda i, c: (c, 0)),
            pl.BlockSpec((1, D_MODEL), lambda i, c: (0, 0)),
        ],
        out_specs=pl.BlockSpec((tm, D_MODEL), lambda i, c: (i, 0)),
        scratch_shapes=[pltpu.VMEM((tm, D_MODEL), BF16), pltpu.VMEM((tm, D_MODEL), F32)],
        compiler_params=_cparams(("parallel", "arbitrary")),
        name="mlp",
    )(h1, nw.reshape(1, -1), wu, wd, fw.reshape(1, -1))


def _regroup_w_in(w_in):
    cq = w_in[:, 0:512]
    ckv = w_in[:, 512:768]
    kr = w_in[:, 768:832]
    rest = w_in[:, 832:2624]
    gates = w_in[:, 2624:2672]
    x1, x2 = kr[:, :32], kr[:, 32:]
    zeros = jnp.zeros((w_in.shape[0], IN_PAD - SEC_GATE - 48), w_in.dtype)
    return jnp.concatenate([cq, ckv, rest, x1, x2, x2, x1, gates, zeros], axis=1).astype(BF16)


def _regroup_w_uq(w_uq):
    w = w_uq.reshape(MLA_Q_RANK, MLA_HEADS, MLA_NOPE + MLA_ROPE)
    nope = w[:, :, :MLA_NOPE]
    x1 = w[:, :, MLA_NOPE:MLA_NOPE + 32]
    x2 = w[:, :, MLA_NOPE + 32:]
    return jnp.concatenate([nope, x1, x2, x2, x1], axis=-1).reshape(MLA_Q_RANK, MLA_HEADS * MLA_QK_PAD).astype(BF16)


def _gate_expand():
    ex = np.zeros((3, LANES, NSA_HEADS * NSA_D), np.float32)
    for b in range(3):
        for h in range(NSA_HEADS):
            ex[b, 3 * h + b, h * NSA_D:(h + 1) * NSA_D] = 1.0
    return jnp.asarray(ex)


def kernel(x, positions, attn_norm_w, w_in, mla_q_norm_w, mla_w_uq, mla_kv_norm_w, mla_w_ukv, cmp_pos_k,
           cmp_pos_v, cmp_w1_k, cmp_w2_k, cmp_w1_v, cmp_w2_v, mla_out_norm_w, nsa_out_norm_w, w_o, mlp_norm_w,
           w_up, w_down, final_norm_w):
    B, T, _ = x.shape
    depth = w_in.shape[0]
    inv = 1.0 / (ROPE_THETA ** (jnp.arange(0, MLA_ROPE, 2, dtype=F32) / MLA_ROPE))
    inv128 = jnp.tile(inv, 4).reshape(1, LANES)
    start = 2.0 ** (-8.0 / NSA_HEADS)
    slopes = start ** jnp.arange(1, NSA_HEADS + 1, dtype=F32)
    expand = _gate_expand()
    nsel_pad = -(-(T // SEL_LEN) // LANES) * LANES
    n_slabs = (SEC_ROPE - SEC_HM) // NSA_D

    outs = []
    for b in range(B):
        h = x[b]
        pos2 = positions[b].reshape(T, 1)
        for l in range(depth):
            proj = _in_proj(h, attn_norm_w[l], _regroup_w_in(w_in[l]))
            wukv = mla_w_ukv[l].reshape(MLA_KV_RANK, MLA_HEADS, MLA_NOPE + MLA_V)
            wuk = wukv[:, :, :MLA_NOPE].reshape(MLA_KV_RANK, MLA_HEADS * MLA_NOPE).astype(BF16)
            wuvT = wukv[:, :, MLA_NOPE:].reshape(MLA_KV_RANK, MLA_HEADS * MLA_V).T.astype(BF16)
            qT, k, vT = _mla_prep(proj, pos2, mla_q_norm_w[l], mla_kv_norm_w[l], _regroup_w_uq(mla_w_uq[l]).T,
                                  wuk, wuvT, inv128)
            y_mla = _mla_attn(qT, k, vT)

            hm = proj[:, SEC_HM:SEC_ROPE].reshape(T, n_slabs, NSA_D).transpose(1, 0, 2)
            r4 = hm[16:20].reshape(4, T // CMP_STRIDE, CMP_STRIDE * NSA_D)
            half = CMP_STRIDE * NSA_D
            pos_kv = jnp.stack([cmp_pos_k[l].reshape(2, half), cmp_pos_v[l].reshape(2, half)])
            kvc = _compress(r4, pos_kv, jnp.stack([cmp_w1_k[l], cmp_w1_v[l]]),
                            jnp.stack([cmp_w2_k[l], cmp_w2_v[l]]))
            o_c, sel = _cmp_attn(slopes, hm, kvc, nsel_pad)
            o_s = _sel_attn(slopes, hm, sel)
            o_w = _win_attn(slopes, hm)

            h = _out_proj(h, y_mla, o_c, o_s, o_w, proj, expand, mla_out_norm_w[l], nsa_out_norm_w[l],
                          w_o[l].astype(BF16))
            h = _mlp(h, mlp_norm_w[l], w_up[l].astype(BF16), w_down[l].astype(BF16), final_norm_w,
                     final=(l == depth - 1))
        outs.append(h)
    return jnp.stack(outs)
```

```python
import functools

import numpy as np
import jax
import jax.numpy as jnp
from jax import lax
from jax.experimental import pallas as pl
from jax.experimental.pallas import tpu as pltpu

F32 = jnp.float32
BF16 = jnp.bfloat16

EPS = 1e-6
NEG = -1e30
REMOVED = -3e38

D_MODEL = 2048
MLA_HEADS = 8
MLA_Q_RANK = 512
MLA_KV_RANK = 256
MLA_NOPE = 128
MLA_ROPE = 64
MLA_V = 128
ROPE_THETA = 10000.0
MLA_QK_PAD = 256

NSA_HEADS = 16
NSA_G = 2
NSA_HPG = 8
NSA_D = 64
CMP_LEN = 32
CMP_STRIDE = 16
CMP_HIDDEN = 256
SEL_LEN = 64
SEL_SHIFT = 6
SEL_TOPK = 16
WINDOW = 512
FORCE_BONUS = 1e4
MLP_HIDDEN = 4 * D_MODEL

LANES = 128
NSA_TQ = 128
SEL_TK = 512
MLA_TQ = 512
MLA_TK = 512
VMEM_LIMIT = 56 * 1024 * 1024

SEC_CQ = 0
SEC_CKV = 512
SEC_HM = 768
SEC_ROPE = 2560
SEC_GATE = 2688
IN_PAD = 2816


def _cparams(sem):
    return pltpu.CompilerParams(dimension_semantics=sem, vmem_limit_bytes=VMEM_LIMIT)


def _rms(x, w):
    return x * lax.rsqrt(jnp.mean(x * x, axis=-1, keepdims=True) + EPS) * w


def _in_proj_kernel(x_ref, nw_ref, w_ref, o_ref):
    n = _rms(x_ref[...], nw_ref[...]).astype(BF16)
    o_ref[...] = jnp.dot(n, w_ref[...], preferred_element_type=F32)


def _in_proj(x2, attn_norm_w, w_in_p, tm=512):
    T = x2.shape[0]
    return pl.pallas_call(
        _in_proj_kernel,
        out_shape=jax.ShapeDtypeStruct((T, IN_PAD), F32),
        grid=(T // tm,),
        in_specs=[
            pl.BlockSpec((tm, D_MODEL), lambda i: (i, 0)),
            pl.BlockSpec((1, D_MODEL), lambda i: (0, 0)),
            pl.BlockSpec((D_MODEL, IN_PAD), lambda i: (0, 0)),
        ],
        out_specs=pl.BlockSpec((tm, IN_PAD), lambda i: (i, 0)),
        compiler_params=_cparams(("parallel",)),
        name="in_proj",
    )(x2, attn_norm_w.reshape(1, D_MODEL), w_in_p)


NT_DIMS = (((1,), (1,)), ((), ()))
TN_DIMS = (((0,), (0,)), ((), ()))


def _mla_prep_kernel(cq_ref, ckv_ref, kr_ref, pos_ref, qnw_ref, kvnw_ref, wuqT_ref, wuk_ref, wuvT_ref, inv_ref,
                     qT_ref, k_ref, vT_ref):
    scale = (MLA_NOPE + MLA_ROPE) ** -0.5
    qn = _rms(cq_ref[...], qnw_ref[...]).astype(BF16)
    qT = lax.dot_general(wuqT_ref[...], qn, NT_DIMS, preferred_element_type=F32)
    kvn = _rms(ckv_ref[...], kvnw_ref[...]).astype(BF16)
    kn = jnp.dot(kvn, wuk_ref[...], preferred_element_type=F32)
    vT = lax.dot_general(wuvT_ref[...], kvn, NT_DIMS, preferred_element_type=F32)

    ang = pos_ref[...].astype(F32) * inv_ref[...]
    c = jnp.cos(ang)
    s = jnp.sin(ang)
    lane = lax.broadcasted_iota(jnp.int32, ang.shape, 1)
    coef = jnp.where(lane < 64, c, jnp.where(lane < 96, -s, s))
    coefT = coef.T

    t = kr_ref[...] * coef
    kpe = jnp.where(lane < 64, t + pltpu.roll(t, 64, 1), 0.0)
    for h in range(MLA_HEADS):
        base = h * MLA_QK_PAD
        tT = qT[base + 128:base + 256, :] * coefT
        ropeT = tT + pltpu.roll(tT, 64, 0)
        qT_ref[h] = (jnp.concatenate([qT[base:base + 128, :], ropeT], axis=0) * scale).astype(BF16)
        k_ref[h] = jnp.concatenate([kn[:, h * 128:(h + 1) * 128], kpe], axis=-1).astype(BF16)
        vT_ref[h] = vT[h * 128:(h + 1) * 128, :].astype(BF16)


def _mla_prep(proj, pos2, qnw, kvnw, wuqT, wuk, wuvT, inv128, tm=256):
    T = proj.shape[0]
    H = MLA_HEADS
    fixed = lambda i: (0, 0)
    return pl.pallas_call(
        _mla_prep_kernel,
        out_shape=(
            jax.ShapeDtypeStruct((H, MLA_QK_PAD, T), BF16),
            jax.ShapeDtypeStruct((H, T, MLA_QK_PAD), BF16),
            jax.ShapeDtypeStruct((H, MLA_V, T), BF16),
        ),
        grid=(T // tm,),
        in_specs=[
            pl.BlockSpec((tm, MLA_Q_RANK), lambda i: (i, SEC_CQ // MLA_Q_RANK)),
            pl.BlockSpec((tm, MLA_KV_RANK), lambda i: (i, SEC_CKV // MLA_KV_RANK)),
            pl.BlockSpec((tm, LANES), lambda i: (i, SEC_ROPE // LANES)),
            pl.BlockSpec((tm, 1), lambda i: (i, 0)),
            pl.BlockSpec((1, MLA_Q_RANK), fixed),
            pl.BlockSpec((1, MLA_KV_RANK), fixed),
            pl.BlockSpec((H * MLA_QK_PAD, MLA_Q_RANK), fixed),
            pl.BlockSpec((MLA_KV_RANK, H * MLA_NOPE), fixed),
            pl.BlockSpec((H * MLA_V, MLA_KV_RANK), fixed),
            pl.BlockSpec((1, LANES), fixed),
        ],
        out_specs=(
            pl.BlockSpec((H, MLA_QK_PAD, tm), lambda i: (0, 0, i)),
            pl.BlockSpec((H, tm, MLA_QK_PAD), lambda i: (0, i, 0)),
            pl.BlockSpec((H, MLA_V, tm), lambda i: (0, 0, i)),
        ),
        compiler_params=_cparams(("parallel",)),
        name="mla_prep",
    )(proj, proj, proj, pos2, qnw.reshape(1, -1), kvnw.reshape(1, -1), wuqT, wuk, wuvT, inv128)


def _mla_attn_kernel(qi_ref, ki_ref, qT_ref, k_ref, vT_ref, o_ref, m_scr, l_scr, acc_scr):
    p_id = pl.program_id(0)
    qi = qi_ref[p_id]
    ki = ki_ref[p_id]

    @pl.when(ki == 0)
    def _():
        m_scr[...] = jnp.full(m_scr.shape, NEG, F32)
        l_scr[...] = jnp.zeros(l_scr.shape, F32)
        acc_scr[...] = jnp.zeros(acc_scr.shape, F32)

    def step(diag):
        for h in range(MLA_HEADS):
            s = jnp.dot(k_ref[h], qT_ref[h], preferred_element_type=F32)
            if diag:
                key = lax.broadcasted_iota(jnp.int32, s.shape, 0)
                qry = lax.broadcasted_iota(jnp.int32, s.shape, 1)
                s = jnp.where(key <= qry, s, NEG)
            m_prev = m_scr[h]
            m_new = jnp.maximum(m_prev, jnp.max(s, axis=0, keepdims=True))
            alpha = jnp.exp(m_prev - m_new)
            p = jnp.exp(s - m_new)
            l_new = alpha * l_scr[h] + jnp.sum(p, axis=0, keepdims=True)
            acc = alpha * acc_scr[h] + jnp.dot(vT_ref[h], p.astype(BF16), preferred_element_type=F32)
            if diag:
                o_ref[h * MLA_V:(h + 1) * MLA_V, :] = acc * (1.0 / l_new)
            else:
                m_scr[h] = m_new
                l_scr[h] = l_new
                acc_scr[h] = acc

    @pl.when(ki < qi)
    def _():
        step(False)

    @pl.when(ki == qi)
    def _():
        step(True)


def _tri_pairs(nq, per_q):
    qi, ki = [], []
    for i in range(nq):
        for c in range(per_q(i) + 1):
            qi.append(i)
            ki.append(c)
    return jnp.asarray(qi, jnp.int32), jnp.asarray(ki, jnp.int32)


def _mla_attn(qT, k, vT):
    H, T, _ = k.shape
    tq = min(MLA_TQ, T)
    nq = T // tq
    qi, ki = _tri_pairs(nq, lambda i: i)
    grid_spec = pltpu.PrefetchScalarGridSpec(
        num_scalar_prefetch=2,
        grid=(int(qi.shape[0]),),
        in_specs=[
            pl.BlockSpec((H, MLA_QK_PAD, tq), lambda p, qi, ki: (0, 0, qi[p])),
            pl.BlockSpec((H, tq, MLA_QK_PAD), lambda p, qi, ki: (0, ki[p], 0)),
            pl.BlockSpec((H, MLA_V, tq), lambda p, qi, ki: (0, 0, ki[p])),
        ],
        out_specs=pl.BlockSpec((H * MLA_V, tq), lambda p, qi, ki: (0, qi[p])),
        scratch_shapes=[
            pltpu.VMEM((H, 1, tq), F32),
            pltpu.VMEM((H, 1, tq), F32),
            pltpu.VMEM((H, MLA_V, tq), F32),
        ],
    )
    return pl.pallas_call(
        _mla_attn_kernel,
        out_shape=jax.ShapeDtypeStruct((H * MLA_V, T), F32),
        grid_spec=grid_spec,
        compiler_params=_cparams(("arbitrary",)),
        name="mla_attn",
    )(qi, ki, qT, k, vT)


def _gelu_tanh(x):
    return 0.5 * x * (1.0 + jnp.tanh(np.sqrt(2.0 / np.pi).astype(np.float32) * (x + 0.044715 * (x * x * x))))


def _compress_kernel(r_ref, pos_ref, w1_ref, w2_ref, o_ref):
    r = r_ref[0]
    nr = r.shape[0]
    half = CMP_STRIDE * NSA_D
    hp = lax.Precision.HIGHEST
    a = jnp.dot(r + pos_ref[0, 0:1, :], w1_ref[0, :half, :], precision=hp, preferred_element_type=F32)
    b = jnp.dot(r + pos_ref[0, 1:2, :], w1_ref[0, half:, :], precision=hp, preferred_element_type=F32)
    hid = _gelu_tanh(a + pltpu.roll(b, nr - 1, 0))
    out = jnp.dot(hid, w2_ref[0], precision=hp, preferred_element_type=F32)
    row = lax.broadcasted_iota(jnp.int32, out.shape, 0)
    o_ref[0] = jnp.where(row < nr - 1, out, 0.0)


def _compress(r4, pos_kv, w1_kv, w2_kv):
    _, nr, width = r4.shape
    return pl.pallas_call(
        _compress_kernel,
        out_shape=jax.ShapeDtypeStruct((4, nr, NSA_D), F32),
        grid=(4,),
        in_specs=[
            pl.BlockSpec((1, nr, width), lambda a: (a, 0, 0)),
            pl.BlockSpec((1, 2, width), lambda a: (a % 2, 0, 0)),
            pl.BlockSpec((1, 2 * width, CMP_HIDDEN), lambda a: (a % 2, 0, 0)),
            pl.BlockSpec((1, CMP_HIDDEN, NSA_D), lambda a: (a % 2, 0, 0)),
        ],
        out_specs=pl.BlockSpec((1, nr, NSA_D), lambda a: (a, 0, 0)),
        compiler_params=_cparams(("parallel",)),
        name="nsa_compress",
    )(r4, pos_kv, w1_kv, w2_kv)


def _store_heads(o_ref, o):
    tq = o.shape[0] // NSA_HPG
    for h in range(NSA_HPG):
        o_ref[:, h * NSA_D:(h + 1) * NSA_D] = o[h * tq:(h + 1) * tq, :]


def _cmp_attn_kernel(slopes_ref, q_ref, kc_ref, vc_ref, o_ref, sel_ref, *, nsel_pad):
    g = pl.program_id(0)
    i = pl.program_id(1)
    tq = NSA_TQ
    nr = kc_ref.shape[1]
    qs = (q_ref[...].reshape(NSA_HPG * tq, NSA_D) * (NSA_D ** -0.5)).astype(BF16)
    s = lax.dot_general(qs, kc_ref[0].astype(BF16), (((1,), (1,)), ((), ())), preferred_element_type=F32)

    qpos = i * tq + lax.broadcasted_iota(jnp.int32, (tq, nr), 0)
    n_id = lax.broadcasted_iota(jnp.int32, (tq, nr), 1)
    dist_i = qpos - (n_id * CMP_STRIDE + (CMP_LEN - 1))
    valid = jnp.logical_and(dist_i >= 0, n_id < nr - 1)
    dist = dist_i.astype(F32)

    psum = jnp.zeros((tq, nr), F32)
    outs = []
    for h in range(NSA_HPG):
        sh = s[h * tq:(h + 1) * tq, :] - slopes_ref[g * NSA_HPG + h] * dist
        sh = jnp.where(valid, sh, NEG)
        m = jnp.max(sh, axis=-1, keepdims=True)
        e = jnp.where(valid, jnp.exp(sh - m), 0.0)
        l = jnp.sum(e, axis=-1, keepdims=True)
        p = e / jnp.where(l > 0.0, l, 1.0)
        psum = psum + p
        outs.append(p.astype(BF16))
    p_all = jnp.concatenate(outs, axis=0)
    o = jnp.dot(p_all, vc_ref[0].astype(BF16), preferred_element_type=F32)
    _store_heads(o_ref, o)

    n2 = lax.broadcasted_iota(jnp.int32, (nr, nsel_pad), 0)
    j2 = lax.broadcasted_iota(jnp.int32, (nr, nsel_pad), 1)
    ov = jnp.logical_and(n2 * CMP_STRIDE <= j2 * SEL_LEN + (SEL_LEN - 1),
                         n2 * CMP_STRIDE + (CMP_LEN - 1) >= j2 * SEL_LEN)
    ov = jnp.logical_and(ov, n2 < nr - 1)
    imp = jnp.dot(psum, jnp.where(ov, 1.0, 0.0).astype(F32), precision=lax.Precision.HIGHEST,
                  preferred_element_type=F32)

    blk = lax.broadcasted_iota(jnp.int32, (tq, nsel_pad), 1)
    cur = jnp.right_shift(i * tq + lax.broadcasted_iota(jnp.int32, (tq, nsel_pad), 0), SEL_SHIFT)
    forced = jnp.logical_or(blk == 0, jnp.logical_or(blk == cur, blk == cur - 1))
    score = jnp.where(blk <= cur, imp + jnp.where(forced, FORCE_BONUS, 0.0), NEG)

    blk_f = blk.astype(F32)

    def pick(_, carry):
        sc, sel = carry
        mx = jnp.max(sc, axis=-1, keepdims=True)
        first = jnp.min(jnp.where(sc == mx, blk_f, float(nsel_pad)), axis=-1, keepdims=True)
        hit = blk_f == first
        sel = jnp.where(jnp.logical_and(hit, mx > NEG * 0.5), 1.0, sel)
        return jnp.where(hit, REMOVED, sc), sel

    _, sel = lax.fori_loop(0, SEL_TOPK, pick, (score, jnp.zeros((tq, nsel_pad), F32)))
    sel_ref[0] = sel


def _cmp_attn(slopes, hm, kvc, nsel_pad):
    T = hm.shape[1]
    nr = kvc.shape[1]
    tq = NSA_TQ
    grid_spec = pltpu.PrefetchScalarGridSpec(
        num_scalar_prefetch=1,
        grid=(NSA_G, T // tq),
        in_specs=[
            pl.BlockSpec((NSA_HPG, tq, NSA_D), lambda g, i, sl: (g, i, 0)),
            pl.BlockSpec((1, nr, NSA_D), lambda g, i, sl: (2 * g, 0, 0)),
            pl.BlockSpec((1, nr, NSA_D), lambda g, i, sl: (2 * g + 1, 0, 0)),
        ],
        out_specs=(
            pl.BlockSpec((tq, NSA_HPG * NSA_D), lambda g, i, sl: (i, g)),
            pl.BlockSpec((1, tq, nsel_pad), lambda g, i, sl: (g, i, 0)),
        ),
    )
    return pl.pallas_call(
        functools.partial(_cmp_attn_kernel, nsel_pad=nsel_pad),
        out_shape=(
            jax.ShapeDtypeStruct((T, NSA_HEADS * NSA_D), F32),
            jax.ShapeDtypeStruct((NSA_G, T, nsel_pad), F32),
        ),
        grid_spec=grid_spec,
        compiler_params=_cparams(("parallel", "parallel")),
        name="nsa_cmp_attn",
    )(slopes, hm, kvc, kvc)


ALIBI_PARTS = 4
ALIBI_HI = 16
NEG_BF16 = -1e30


def _sel_prep_kernel(k_ref, vT_ref, ka_ref, va_ref, *, tk, nsel_pad):
    c = pl.program_id(1)
    k = k_ref[0]
    lane = lax.broadcasted_iota(jnp.int32, (tk, NSA_D), 1)
    r = lax.broadcasted_iota(jnp.int32, (tk, NSA_D), 0)
    digits = jnp.where(lane < ALIBI_PARTS, jnp.right_shift(r, 4),
                       jnp.where(lane < 2 * ALIBI_PARTS, jnp.bitwise_and(r, ALIBI_HI - 1), 0)).astype(F32)
    blk = lax.broadcasted_iota(jnp.int32, (tk, nsel_pad), 1)
    kblk = jnp.right_shift(c * tk + lax.broadcasted_iota(jnp.int32, (tk, nsel_pad), 0), SEL_SHIFT)
    onehot = jnp.where(blk == kblk, 1.0, 0.0)
    ka_ref[0] = jnp.concatenate([k, digits, onehot], axis=-1).astype(BF16)
    row = lax.broadcasted_iota(jnp.int32, (SEL_VROWS - NSA_D, tk), 0)
    va_ref[0, 0] = jnp.concatenate([vT_ref[...], jnp.where(row == 0, 1.0, 0.0)], axis=0).astype(BF16)


SEL_VROWS = 80


def _sel_prep(hm, fm, tk, nsel_pad):
    T = hm.shape[1]
    kw = 2 * NSA_D + nsel_pad
    return pl.pallas_call(
        functools.partial(_sel_prep_kernel, tk=tk, nsel_pad=nsel_pad),
        out_shape=(
            jax.ShapeDtypeStruct((NSA_G, T, kw), BF16),
            jax.ShapeDtypeStruct((NSA_G, T // tk, SEL_VROWS, tk), BF16),
        ),
        grid=(NSA_G, T // tk),
        in_specs=[
            pl.BlockSpec((1, tk, NSA_D), lambda g, c: (20 + 2 * g, c, 0)),
            pl.BlockSpec((NSA_D, tk), lambda g, c: (21 + 2 * g, c)),
        ],
        out_specs=(
            pl.BlockSpec((1, tk, kw), lambda g, c: (g, c, 0)),
            pl.BlockSpec((1, 1, SEL_VROWS, tk), lambda g, c: (g, c, 0, 0)),
        ),
        compiler_params=_cparams(("parallel", "parallel")),
        name="nsa_sel_prep",
    )(hm, fm)


def _sel_attn_kernel(qT_ref, sel_ref, arow_ref, srow_ref, ka_ref, va_ref, o_ref,
                     qa_scr, s0_scr, s1_scr, p0_scr, p1_scr, al0_scr, al1_scr, m_scr, acc_scr, *, tk, n_chunks):
    i = pl.program_id(1)
    tq = NSA_TQ
    n = NSA_HPG * tq
    nsel_pad = sel_ref.shape[-1]
    q = qT_ref[...] * (NSA_D ** -0.5)
    qa_scr[0:NSA_D, :] = jnp.concatenate([q[h * NSA_D:(h + 1) * NSA_D, :] for h in range(NSA_HPG)],
                                         axis=1).astype(BF16)
    qa_scr[NSA_D:2 * NSA_D, :] = arow_ref[0]
    blk = lax.broadcasted_iota(jnp.int32, (nsel_pad, tq), 0)
    own = jnp.right_shift(blk, 1) == i
    chosen = jnp.logical_and(sel_ref[0].T > 0.5, jnp.logical_not(own))
    qa_scr[2 * NSA_D:, :] = jnp.concatenate([jnp.where(chosen, 0.0, NEG_BF16).astype(BF16)] * NSA_HPG, axis=1)

    tpos = i * tq + jnp.bitwise_and(lax.broadcasted_iota(jnp.int32, (1, n), 1), tq - 1)
    slope_row = srow_ref[0]

    def rterm(c):
        return slope_row * (c * tk - tpos).astype(F32)

    c_own = (i * tq) // tk
    own_start = pl.multiple_of(i * tq, tq)
    s = jnp.dot(ka_ref[0, pl.ds(own_start, tq), 0:2 * NSA_D], qa_scr[0:2 * NSA_D, :], preferred_element_type=F32)
    kpos = i * tq + lax.broadcasted_iota(jnp.int32, s.shape, 0)
    s = jnp.where(kpos <= tpos, s, NEG)
    r_own = rterm(c_own)
    m0 = jnp.max(s, axis=0, keepdims=True) + r_own
    p = jnp.exp(s - (m0 - r_own)).astype(BF16)
    v_own = va_ref[0, c_own, :, pl.ds(pl.multiple_of(i * tq - c_own * tk, tq), tq)]
    acc_scr[...] = jnp.dot(v_own, p, preferred_element_type=F32)
    m_scr[...] = m0

    s_slots = (s0_scr, s1_scr)
    p_slots = (p0_scr, p1_scr)
    al_slots = (al0_scr, al1_scr)

    def scores(j, slot):
        start = pl.multiple_of(j * tk, tk)
        s_slots[slot][...] = jnp.dot(ka_ref[0, pl.ds(start, tk), :], qa_scr[...], preferred_element_type=F32)

    def softmax(j, slot):
        sj = s_slots[slot][...]
        r = rterm(j)
        m_prev = m_scr[...]
        m_new = jnp.maximum(m_prev, jnp.max(sj, axis=0, keepdims=True) + r)
        p_slots[slot][...] = jnp.exp(sj - (m_new - r)).astype(BF16)
        al_slots[slot][...] = jnp.exp(m_prev - m_new)
        m_scr[...] = m_new

    def values(j, slot):
        acc_scr[...] = (al_slots[slot][...] * acc_scr[...]
                        + jnp.dot(va_ref[0, j], p_slots[slot][...], preferred_element_type=F32))

    n_pairs = jnp.minimum(((i * tq + tq - 1) // tk + 2) // 2, n_chunks // 2)
    scores(0, 0)
    scores(1, 1)
    softmax(0, 0)

    def body(jj, carry):
        j = 2 * jj
        scores(j, 0)
        softmax(j - 1, 1)
        values(j - 2, 0)
        scores(j + 1, 1)
        softmax(j, 0)
        values(j - 1, 1)
        return carry

    lax.fori_loop(1, n_pairs, body, 0)
    softmax(2 * n_pairs - 1, 1)
    values(2 * n_pairs - 2, 0)
    values(2 * n_pairs - 1, 1)
    acc = acc_scr[...]
    o = acc[0:NSA_D, :] * (1.0 / acc[NSA_D:NSA_D + 1, :])
    o_ref[...] = jnp.concatenate([o[:, h * tq:(h + 1) * tq] for h in range(NSA_HPG)], axis=0)


def _alibi_rows(slopes):
    parts, rest = [], slopes
    for _ in range(ALIBI_PARTS):
        piece = rest.astype(BF16)
        parts.append(piece)
        rest = rest - piece.astype(F32)
    pieces = jnp.stack(parts).astype(F32)
    rows = jnp.concatenate([pieces * ALIBI_HI, pieces, jnp.zeros((NSA_D - 2 * ALIBI_PARTS, NSA_HEADS), F32)])
    rows = jnp.repeat(rows.reshape(NSA_D, NSA_G, NSA_HPG), NSA_TQ, axis=2).transpose(1, 0, 2)
    srow = jnp.repeat(slopes.reshape(NSA_G, 1, NSA_HPG), NSA_TQ, axis=2)
    return rows.astype(BF16), srow


def _sel_attn(slopes, hm, fm, sel):
    T = hm.shape[1]
    tq = NSA_TQ
    tk = min(SEL_TK, T)
    nsel_pad = sel.shape[-1]
    n = NSA_HPG * tq
    kw = 2 * NSA_D + nsel_pad
    ka, va = _sel_prep(hm, fm, tk, nsel_pad)
    arows, srow = _alibi_rows(slopes)
    return pl.pallas_call(
        functools.partial(_sel_attn_kernel, tk=tk, n_chunks=T // tk),
        out_shape=jax.ShapeDtypeStruct((NSA_HEADS * NSA_D, T), F32),
        grid=(NSA_G, T // tq),
        in_specs=[
            pl.BlockSpec((NSA_HPG * NSA_D, tq), lambda g, i: (g, i)),
            pl.BlockSpec((1, tq, nsel_pad), lambda g, i: (g, i, 0)),
            pl.BlockSpec((1, NSA_D, n), lambda g, i: (g, 0, 0)),
            pl.BlockSpec((1, 1, n), lambda g, i: (g, 0, 0)),
            pl.BlockSpec((1, T, kw), lambda g, i: (g, 0, 0)),
            pl.BlockSpec((1, T // tk, SEL_VROWS, tk), lambda g, i: (g, 0, 0, 0)),
        ],
        out_specs=pl.BlockSpec((NSA_HPG * NSA_D, tq), lambda g, i: (g, i)),
        scratch_shapes=[
            pltpu.VMEM((kw, n), BF16),
            pltpu.VMEM((tk, n), F32),
            pltpu.VMEM((tk, n), F32),
            pltpu.VMEM((tk, n), BF16),
            pltpu.VMEM((tk, n), BF16),
            pltpu.VMEM((1, n), F32),
            pltpu.VMEM((1, n), F32),
            pltpu.VMEM((1, n), F32),
            pltpu.VMEM((SEL_VROWS, n), F32),
        ],
        compiler_params=_cparams(("parallel", "arbitrary")),
        name="nsa_sel_attn",
    )(fm, sel, arows, srow, ka, va)


WIN_BLOCKS = WINDOW // NSA_TQ + 1


def _win_attn_kernel(slopes_ref, q_ref, *refs):
    k_refs = refs[:WIN_BLOCKS]
    v_refs = refs[WIN_BLOCKS:2 * WIN_BLOCKS]
    o_ref = refs[2 * WIN_BLOCKS]
    g = pl.program_id(0)
    i = pl.program_id(1)
    tq = NSA_TQ
    nk = WIN_BLOCKS * tq
    qs = (q_ref[...].reshape(NSA_HPG * tq, NSA_D) * (NSA_D ** -0.5)).astype(BF16)
    kcat = jnp.concatenate([r[0] for r in k_refs], axis=0).astype(BF16)
    vcat = jnp.concatenate([r[0] for r in v_refs], axis=0).astype(BF16)
    s = lax.dot_general(qs, kcat, (((1,), (1,)), ((), ())), preferred_element_type=F32)

    qpos = i * tq + lax.broadcasted_iota(jnp.int32, (tq, nk), 0)
    kpos = i * tq - WINDOW + lax.broadcasted_iota(jnp.int32, (tq, nk), 1)
    dist_i = qpos - kpos
    allowed = jnp.logical_and(jnp.logical_and(dist_i >= 0, dist_i < WINDOW), kpos >= 0)
    dist = dist_i.astype(F32)
    ps = []
    for h in range(NSA_HPG):
        sh = jnp.where(allowed, s[h * tq:(h + 1) * tq, :] - slopes_ref[g * NSA_HPG + h] * dist, NEG)
        m = jnp.max(sh, axis=-1, keepdims=True)
        e = jnp.where(allowed, jnp.exp(sh - m), 0.0)
        ps.append((e / jnp.sum(e, axis=-1, keepdims=True)).astype(BF16))
    o = jnp.dot(jnp.concatenate(ps, axis=0), vcat, preferred_element_type=F32)
    _store_heads(o_ref, o)


def _win_attn(slopes, hm):
    T = hm.shape[1]
    tq = NSA_TQ
    back = WIN_BLOCKS - 1

    def kv_spec(a0, j):
        return pl.BlockSpec((1, tq, NSA_D), lambda g, i, sl: (a0 + 2 * g, jnp.maximum(i + j - back, 0), 0))

    grid_spec = pltpu.PrefetchScalarGridSpec(
        num_scalar_prefetch=1,
        grid=(NSA_G, T // tq),
        in_specs=[pl.BlockSpec((NSA_HPG, tq, NSA_D), lambda g, i, sl: (g, i, 0))]
        + [kv_spec(24, j) for j in range(WIN_BLOCKS)]
        + [kv_spec(25, j) for j in range(WIN_BLOCKS)],
        out_specs=pl.BlockSpec((tq, NSA_HPG * NSA_D), lambda g, i, sl: (i, g)),
    )
    return pl.pallas_call(
        _win_attn_kernel,
        out_shape=jax.ShapeDtypeStruct((T, NSA_HEADS * NSA_D), F32),
        grid_spec=grid_spec,
        compiler_params=_cparams(("parallel", "parallel")),
        name="nsa_win_attn",
    )(slopes, hm, *([hm] * (2 * WIN_BLOCKS)))


def _out_proj_kernel(x_ref, ymT_ref, oc_ref, os_ref, ow_ref, gl_ref, ex_ref, mnw_ref, nnw_ref, wo_ref, o_ref):
    gates = 1.0 / (1.0 + jnp.exp(-gl_ref[...]))
    hp = lax.Precision.HIGHEST
    y_nsa = (jnp.dot(gates, ex_ref[0], precision=hp, preferred_element_type=F32) * oc_ref[...]
             + jnp.dot(gates, ex_ref[1], precision=hp, preferred_element_type=F32) * os_ref[...]
             + jnp.dot(gates, ex_ref[2], precision=hp, preferred_element_type=F32) * ow_ref[...])
    half = MLA_HEADS * MLA_V
    ymT = ymT_ref[...]
    ymT = (ymT * lax.rsqrt(jnp.mean(ymT * ymT, axis=0, keepdims=True) + EPS) * mnw_ref[...]).astype(BF16)
    yn = _rms(y_nsa, nnw_ref[...]).astype(BF16)
    o_ref[...] = (x_ref[...]
                  + lax.dot_general(ymT, wo_ref[:half, :], TN_DIMS, preferred_element_type=F32)
                  + jnp.dot(yn, wo_ref[half:, :], preferred_element_type=F32))


def _out_proj(x2, y_mlaT, o_c, o_s, o_w, proj, expand, mnw, nnw, wo, tm=256):
    T = x2.shape[0]
    half = MLA_HEADS * MLA_V
    row = lambda i: (i, 0)
    fixed2 = lambda i: (0, 0)
    return pl.pallas_call(
        _out_proj_kernel,
        out_shape=jax.ShapeDtypeStruct((T, D_MODEL), F32),
        grid=(T // tm,),
        in_specs=[
            pl.BlockSpec((tm, D_MODEL), row),
            pl.BlockSpec((half, tm), lambda i: (0, i)),
            pl.BlockSpec((tm, half), row),
            pl.BlockSpec((tm, half), row),
            pl.BlockSpec((tm, half), row),
            pl.BlockSpec((tm, LANES), lambda i: (i, SEC_GATE // LANES)),
            pl.BlockSpec((3, LANES, half), lambda i: (0, 0, 0)),
            pl.BlockSpec((half, 1), fixed2),
            pl.BlockSpec((1, half), fixed2),
            pl.BlockSpec((2 * half, D_MODEL), fixed2),
        ],
        out_specs=pl.BlockSpec((tm, D_MODEL), row),
        compiler_params=_cparams(("parallel",)),
        name="out_proj",
    )(x2, y_mlaT, o_c, o_s, o_w, proj, expand, mnw.reshape(-1, 1), nnw.reshape(1, -1), wo)


def _mlp_kernel(h_ref, nw_ref, wu_ref, wd_ref, fw_ref, o_ref, n_scr, acc_scr, *, final):
    c = pl.program_id(1)

    @pl.when(c == 0)
    def _():
        n_scr[...] = _rms(h_ref[...], nw_ref[...]).astype(BF16)
        acc_scr[...] = jnp.zeros(acc_scr.shape, F32)

    u = jnp.dot(n_scr[...], wu_ref[...], preferred_element_type=F32)
    a = jnp.square(jnp.maximum(u, 0.0)).astype(BF16)
    acc_scr[...] += jnp.dot(a, wd_ref[...], preferred_element_type=F32)

    @pl.when(c == pl.num_programs(1) - 1)
    def _():
        h2 = h_ref[...] + acc_scr[...]
        o_ref[...] = _rms(h2, fw_ref[...]) if final else h2


def _mlp(h1, nw, wu, wd, fw, final, tm=512, tc=1024):
    T = h1.shape[0]
    tm = min(tm, T)
    return pl.pallas_call(
        functools.partial(_mlp_kernel, final=final),
        out_shape=jax.ShapeDtypeStruct((T, D_MODEL), F32),
        grid=(T // tm, MLP_HIDDEN // tc),
        in_specs=[
            pl.BlockSpec((tm, D_MODEL), lambda i, c: (i, 0)),
            pl.BlockSpec((1, D_MODEL), lambda i, c: (0, 0)),
            pl.BlockSpec((D_MODEL, tc), lambda i, c: (0, c)),
            pl.BlockSpec((tc, D_MODEL), lambda i, c: (c, 0)),
            pl.BlockSpec((1, D_MODEL), lambda i, c: (0, 0)),
        ],
        out_specs=pl.BlockSpec((tm, D_MODEL), lambda i, c: (i, 0)),
        scratch_shapes=[pltpu.VMEM((tm, D_MODEL), BF16), pltpu.VMEM((tm, D_MODEL), F32)],
        compiler_params=_cparams(("parallel", "arbitrary")),
        name="mlp",
    )(h1, nw.reshape(1, -1), wu, wd, fw.reshape(1, -1))


def _regroup_w_in(w_in):
    cq = w_in[:, 0:512]
    ckv = w_in[:, 512:768]
    kr = w_in[:, 768:832]
    rest = w_in[:, 832:2624]
    gates = w_in[:, 2624:2672]
    x1, x2 = kr[:, :32], kr[:, 32:]
    zeros = jnp.zeros((w_in.shape[0], IN_PAD - SEC_GATE - 48), w_in.dtype)
    return jnp.concatenate([cq, ckv, rest, x1, x2, x2, x1, gates, zeros], axis=1).astype(BF16)


def _regroup_w_uq(w_uq):
    w = w_uq.reshape(MLA_Q_RANK, MLA_HEADS, MLA_NOPE + MLA_ROPE)
    nope = w[:, :, :MLA_NOPE]
    x1 = w[:, :, MLA_NOPE:MLA_NOPE + 32]
    x2 = w[:, :, MLA_NOPE + 32:]
    return jnp.concatenate([nope, x1, x2, x2, x1], axis=-1).reshape(MLA_Q_RANK, MLA_HEADS * MLA_QK_PAD).astype(BF16)


def _gate_expand():
    ex = np.zeros((3, LANES, NSA_HEADS * NSA_D), np.float32)
    for b in range(3):
        for h in range(NSA_HEADS):
            ex[b, 3 * h + b, h * NSA_D:(h + 1) * NSA_D] = 1.0
    return jnp.asarray(ex)


def kernel(x, positions, attn_norm_w, w_in, mla_q_norm_w, mla_w_uq, mla_kv_norm_w, mla_w_ukv, cmp_pos_k,
           cmp_pos_v, cmp_w1_k, cmp_w2_k, cmp_w1_v, cmp_w2_v, mla_out_norm_w, nsa_out_norm_w, w_o, mlp_norm_w,
           w_up, w_down, final_norm_w):
    B, T, _ = x.shape
    depth = w_in.shape[0]
    inv = 1.0 / (ROPE_THETA ** (jnp.arange(0, MLA_ROPE, 2, dtype=F32) / MLA_ROPE))
    inv128 = jnp.tile(inv, 4).reshape(1, LANES)
    start = 2.0 ** (-8.0 / NSA_HEADS)
    slopes = start ** jnp.arange(1, NSA_HEADS + 1, dtype=F32)
    expand = _gate_expand()
    nsel_pad = -(-(T // SEL_LEN) // LANES) * LANES
    n_slabs = (SEC_ROPE - SEC_HM) // NSA_D

    outs = []
    for b in range(B):
        h = x[b]
        pos2 = positions[b].reshape(T, 1)
        for l in range(depth):
            proj = _in_proj(h, attn_norm_w[l], _regroup_w_in(w_in[l]))
            wukv = mla_w_ukv[l].reshape(MLA_KV_RANK, MLA_HEADS, MLA_NOPE + MLA_V)
            wuk = wukv[:, :, :MLA_NOPE].reshape(MLA_KV_RANK, MLA_HEADS * MLA_NOPE).astype(BF16)
            wuvT = wukv[:, :, MLA_NOPE:].reshape(MLA_KV_RANK, MLA_HEADS * MLA_V).T.astype(BF16)
            qT, k, vT = _mla_prep(proj, pos2, mla_q_norm_w[l], mla_kv_norm_w[l], _regroup_w_uq(mla_w_uq[l]).T,
                                  wuk, wuvT, inv128)
            y_mla = _mla_attn(qT, k, vT)

            hm = proj[:, SEC_HM:SEC_ROPE].reshape(T, n_slabs, NSA_D).transpose(1, 0, 2)
            r4 = hm[16:20].reshape(4, T // CMP_STRIDE, CMP_STRIDE * NSA_D)
            half = CMP_STRIDE * NSA_D
            pos_kv = jnp.stack([cmp_pos_k[l].reshape(2, half), cmp_pos_v[l].reshape(2, half)])
            kvc = _compress(r4, pos_kv, jnp.stack([cmp_w1_k[l], cmp_w1_v[l]]),
                            jnp.stack([cmp_w2_k[l], cmp_w2_v[l]]))
            o_c, sel = _cmp_attn(slopes, hm, kvc, nsel_pad)
            fm = proj[:, SEC_HM:SEC_ROPE].T
            o_s = _sel_attn(slopes, hm, fm, sel).T
            o_w = _win_attn(slopes, hm)

            h = _out_proj(h, y_mla, o_c, o_s, o_w, proj, expand, mla_out_norm_w[l], nsa_out_norm_w[l],
                          w_o[l].astype(BF16))
            h = _mlp(h, mlp_norm_w[l], w_up[l].astype(BF16), w_down[l].astype(BF16), final_norm_w,
                     final=(l == depth - 1))
        outs.append(h)
    return jnp.stack(outs)
```

```python
import functools

import numpy as np
import jax
import jax.numpy as jnp
from jax import lax
from jax.experimental import pallas as pl
from jax.experimental.pallas import tpu as pltpu

F32 = jnp.float32
BF16 = jnp.bfloat16

EPS = 1e-6
NEG = -1e30
REMOVED = -3e38

D_MODEL = 2048
MLA_HEADS = 8
MLA_Q_RANK = 512
MLA_KV_RANK = 256
MLA_NOPE = 128
MLA_ROPE = 64
MLA_V = 128
ROPE_THETA = 10000.0
MLA_QK_PAD = 256

NSA_HEADS = 16
NSA_G = 2
NSA_HPG = 8
NSA_D = 64
CMP_LEN = 32
CMP_STRIDE = 16
CMP_HIDDEN = 256
SEL_LEN = 64
SEL_SHIFT = 6
SEL_TOPK = 16
WINDOW = 512
FORCE_BONUS = 1e4
MLP_HIDDEN = 4 * D_MODEL

LANES = 128
NSA_TQ = 128
SEL_TK = 512
MLA_TQ = 512
MLA_TK = 512
VMEM_LIMIT = 56 * 1024 * 1024

SEC_CQ = 0
SEC_CKV = 512
SEC_HM = 768
SEC_ROPE = 2560
SEC_GATE = 2688
IN_PAD = 2816


def _cparams(sem):
    return pltpu.CompilerParams(dimension_semantics=sem, vmem_limit_bytes=VMEM_LIMIT)


def _rms(x, w):
    return x * lax.rsqrt(jnp.mean(x * x, axis=-1, keepdims=True) + EPS) * w


def _in_proj_kernel(x_ref, nw_ref, w_ref, o_ref):
    n = _rms(x_ref[...], nw_ref[...]).astype(BF16)
    o_ref[...] = jnp.dot(n, w_ref[...], preferred_element_type=F32)


def _in_proj(x2, attn_norm_w, w_in_p, tm=512):
    T = x2.shape[0]
    return pl.pallas_call(
        _in_proj_kernel,
        out_shape=jax.ShapeDtypeStruct((T, IN_PAD), F32),
        grid=(T // tm,),
        in_specs=[
            pl.BlockSpec((tm, D_MODEL), lambda i: (i, 0)),
            pl.BlockSpec((1, D_MODEL), lambda i: (0, 0)),
            pl.BlockSpec((D_MODEL, IN_PAD), lambda i: (0, 0)),
        ],
        out_specs=pl.BlockSpec((tm, IN_PAD), lambda i: (i, 0)),
        compiler_params=_cparams(("parallel",)),
        name="in_proj",
    )(x2, attn_norm_w.reshape(1, D_MODEL), w_in_p)


NT_DIMS = (((1,), (1,)), ((), ()))
TN_DIMS = (((0,), (0,)), ((), ()))


def _mla_prep_kernel(cq_ref, ckv_ref, kr_ref, pos_ref, qnw_ref, kvnw_ref, wuqT_ref, wuk_ref, wuvT_ref, inv_ref,
                     qT_ref, k_ref, vT_ref):
    scale = (MLA_NOPE + MLA_ROPE) ** -0.5
    qn = _rms(cq_ref[...], qnw_ref[...]).astype(BF16)
    qT = lax.dot_general(wuqT_ref[...], qn, NT_DIMS, preferred_element_type=F32)
    kvn = _rms(ckv_ref[...], kvnw_ref[...]).astype(BF16)
    kn = jnp.dot(kvn, wuk_ref[...], preferred_element_type=F32)
    vT = lax.dot_general(wuvT_ref[...], kvn, NT_DIMS, preferred_element_type=F32)

    ang = pos_ref[...].astype(F32) * inv_ref[...]
    c = jnp.cos(ang)
    s = jnp.sin(ang)
    lane = lax.broadcasted_iota(jnp.int32, ang.shape, 1)
    coef = jnp.where(lane < 64, c, jnp.where(lane < 96, -s, s))
    coefT = coef.T

    t = kr_ref[...] * coef
    kpe = jnp.where(lane < 64, t + pltpu.roll(t, 64, 1), 0.0)
    for h in range(MLA_HEADS):
        base = h * MLA_QK_PAD
        tT = qT[base + 128:base + 256, :] * coefT
        ropeT = tT + pltpu.roll(tT, 64, 0)
        qT_ref[h] = (jnp.concatenate([qT[base:base + 128, :], ropeT], axis=0) * scale).astype(BF16)
        k_ref[h] = jnp.concatenate([kn[:, h * 128:(h + 1) * 128], kpe], axis=-1).astype(BF16)
        vT_ref[h] = vT[h * 128:(h + 1) * 128, :].astype(BF16)


def _mla_prep(proj, pos2, qnw, kvnw, wuqT, wuk, wuvT, inv128, tm=256):
    T = proj.shape[0]
    H = MLA_HEADS
    fixed = lambda i: (0, 0)
    return pl.pallas_call(
        _mla_prep_kernel,
        out_shape=(
            jax.ShapeDtypeStruct((H, MLA_QK_PAD, T), BF16),
            jax.ShapeDtypeStruct((H, T, MLA_QK_PAD), BF16),
            jax.ShapeDtypeStruct((H, MLA_V, T), BF16),
        ),
        grid=(T // tm,),
        in_specs=[
            pl.BlockSpec((tm, MLA_Q_RANK), lambda i: (i, SEC_CQ // MLA_Q_RANK)),
            pl.BlockSpec((tm, MLA_KV_RANK), lambda i: (i, SEC_CKV // MLA_KV_RANK)),
            pl.BlockSpec((tm, LANES), lambda i: (i, SEC_ROPE // LANES)),
            pl.BlockSpec((tm, 1), lambda i: (i, 0)),
            pl.BlockSpec((1, MLA_Q_RANK), fixed),
            pl.BlockSpec((1, MLA_KV_RANK), fixed),
            pl.BlockSpec((H * MLA_QK_PAD, MLA_Q_RANK), fixed),
            pl.BlockSpec((MLA_KV_RANK, H * MLA_NOPE), fixed),
            pl.BlockSpec((H * MLA_V, MLA_KV_RANK), fixed),
            pl.BlockSpec((1, LANES), fixed),
        ],
        out_specs=(
            pl.BlockSpec((H, MLA_QK_PAD, tm), lambda i: (0, 0, i)),
            pl.BlockSpec((H, tm, MLA_QK_PAD), lambda i: (0, i, 0)),
            pl.BlockSpec((H, MLA_V, tm), lambda i: (0, 0, i)),
        ),
        compiler_params=_cparams(("parallel",)),
        name="mla_prep",
    )(proj, proj, proj, pos2, qnw.reshape(1, -1), kvnw.reshape(1, -1), wuqT, wuk, wuvT, inv128)


def _mla_attn_kernel(qi_ref, ki_ref, qT_ref, k_ref, vT_ref, o_ref, m_scr, l_scr, acc_scr):
    p_id = pl.program_id(0)
    qi = qi_ref[p_id]
    ki = ki_ref[p_id]

    @pl.when(ki == 0)
    def _():
        m_scr[...] = jnp.full(m_scr.shape, NEG, F32)
        l_scr[...] = jnp.zeros(l_scr.shape, F32)
        acc_scr[...] = jnp.zeros(acc_scr.shape, F32)

    def step(diag):
        for h in range(MLA_HEADS):
            s = jnp.dot(k_ref[h], qT_ref[h], preferred_element_type=F32)
            if diag:
                key = lax.broadcasted_iota(jnp.int32, s.shape, 0)
                qry = lax.broadcasted_iota(jnp.int32, s.shape, 1)
                s = jnp.where(key <= qry, s, NEG)
            m_prev = m_scr[h]
            m_new = jnp.maximum(m_prev, jnp.max(s, axis=0, keepdims=True))
            alpha = jnp.exp(m_prev - m_new)
            p = jnp.exp(s - m_new)
            l_new = alpha * l_scr[h] + jnp.sum(p, axis=0, keepdims=True)
            acc = alpha * acc_scr[h] + jnp.dot(vT_ref[h], p.astype(BF16), preferred_element_type=F32)
            if diag:
                o_ref[h * MLA_V:(h + 1) * MLA_V, :] = acc * (1.0 / l_new)
            else:
                m_scr[h] = m_new
                l_scr[h] = l_new
                acc_scr[h] = acc

    @pl.when(ki < qi)
    def _():
        step(False)

    @pl.when(ki == qi)
    def _():
        step(True)


def _tri_pairs(nq, per_q):
    qi, ki = [], []
    for i in range(nq):
        for c in range(per_q(i) + 1):
            qi.append(i)
            ki.append(c)
    return jnp.asarray(qi, jnp.int32), jnp.asarray(ki, jnp.int32)


def _mla_attn(qT, k, vT):
    H, T, _ = k.shape
    tq = min(MLA_TQ, T)
    nq = T // tq
    qi, ki = _tri_pairs(nq, lambda i: i)
    grid_spec = pltpu.PrefetchScalarGridSpec(
        num_scalar_prefetch=2,
        grid=(int(qi.shape[0]),),
        in_specs=[
            pl.BlockSpec((H, MLA_QK_PAD, tq), lambda p, qi, ki: (0, 0, qi[p])),
            pl.BlockSpec((H, tq, MLA_QK_PAD), lambda p, qi, ki: (0, ki[p], 0)),
            pl.BlockSpec((H, MLA_V, tq), lambda p, qi, ki: (0, 0, ki[p])),
        ],
        out_specs=pl.BlockSpec((H * MLA_V, tq), lambda p, qi, ki: (0, qi[p])),
        scratch_shapes=[
            pltpu.VMEM((H, 1, tq), F32),
            pltpu.VMEM((H, 1, tq), F32),
            pltpu.VMEM((H, MLA_V, tq), F32),
        ],
    )
    return pl.pallas_call(
        _mla_attn_kernel,
        out_shape=jax.ShapeDtypeStruct((H * MLA_V, T), F32),
        grid_spec=grid_spec,
        compiler_params=_cparams(("arbitrary",)),
        name="mla_attn",
    )(qi, ki, qT, k, vT)


ALIBI_PARTS = 4
ALIBI_HI = 16
NEG_BF16 = -1e30
SEL_VROWS = 80


def _gelu_tanh(x):
    return 0.5 * x * (1.0 + jnp.tanh(np.sqrt(2.0 / np.pi).astype(np.float32) * (x + 0.044715 * (x * x * x))))


def _digit_cols(offset, scale):
    lane = lax.broadcasted_iota(jnp.int32, offset.shape, 1)
    hi = jnp.right_shift(offset, 4) * scale
    lo = jnp.bitwise_and(offset, ALIBI_HI - 1) * scale
    return jnp.where(lane < ALIBI_PARTS, hi, jnp.where(lane < 2 * ALIBI_PARTS, lo, 0)).astype(F32)


def _compress_kernel(r_ref, pos_ref, w1_ref, w2_ref, kc_ref, vcT_ref):
    r = r_ref[0]
    nr = r.shape[0]
    half = CMP_STRIDE * NSA_D
    hp = lax.Precision.HIGHEST
    a = jnp.dot(r + pos_ref[0, 0:1, :], w1_ref[0, :half, :], precision=hp, preferred_element_type=F32)
    b = jnp.dot(r + pos_ref[0, 1:2, :], w1_ref[0, half:, :], precision=hp, preferred_element_type=F32)
    hid = _gelu_tanh(a + pltpu.roll(b, nr - 1, 0))
    out = jnp.dot(hid, w2_ref[0], precision=hp, preferred_element_type=F32)
    row = lax.broadcasted_iota(jnp.int32, out.shape, 0)
    out = jnp.where(row < nr - 1, out, 0.0)
    is_key = pl.program_id(0) % 2 == 0

    @pl.when(is_key)
    def _():
        kc_ref[0] = jnp.concatenate([out, _digit_cols(row, CMP_STRIDE)], axis=-1).astype(BF16)

    @pl.when(jnp.logical_not(is_key))
    def _():
        vcT_ref[0] = jnp.concatenate([out, jnp.zeros_like(out)], axis=-1).T[0:NSA_D, :].astype(BF16)


def _compress(r4, pos_kv, w1_kv, w2_kv):
    _, nr, width = r4.shape
    return pl.pallas_call(
        _compress_kernel,
        out_shape=(
            jax.ShapeDtypeStruct((NSA_G, nr, 2 * NSA_D), BF16),
            jax.ShapeDtypeStruct((NSA_G, NSA_D, nr), BF16),
        ),
        grid=(2 * NSA_G,),
        in_specs=[
            pl.BlockSpec((1, nr, width), lambda a: (a, 0, 0)),
            pl.BlockSpec((1, 2, width), lambda a: (a % 2, 0, 0)),
            pl.BlockSpec((1, 2 * width, CMP_HIDDEN), lambda a: (a % 2, 0, 0)),
            pl.BlockSpec((1, CMP_HIDDEN, NSA_D), lambda a: (a % 2, 0, 0)),
        ],
        out_specs=(
            pl.BlockSpec((1, nr, 2 * NSA_D), lambda a: (a // 2, 0, 0)),
            pl.BlockSpec((1, NSA_D, nr), lambda a: (a // 2, 0, 0)),
        ),
        compiler_params=_cparams(("arbitrary",)),
        name="nsa_compress",
    )(r4, pos_kv, w1_kv, w2_kv)


def _alibi_rows(slopes):
    parts, rest = [], slopes
    for _ in range(ALIBI_PARTS):
        piece = rest.astype(BF16)
        parts.append(piece)
        rest = rest - piece.astype(F32)
    pieces = jnp.stack(parts).astype(F32)
    rows = jnp.concatenate([pieces * ALIBI_HI, pieces, jnp.zeros((NSA_D - 2 * ALIBI_PARTS, NSA_HEADS), F32)])
    rows = jnp.repeat(rows.reshape(NSA_D, NSA_G, NSA_HPG), NSA_TQ, axis=2).transpose(1, 0, 2)
    srow = jnp.repeat(slopes.reshape(NSA_G, 1, NSA_HPG), NSA_TQ, axis=2)
    return rows.astype(BF16), srow


def _nsa_queries(qT_ref, arow_ref):
    q = qT_ref[...] * (NSA_D ** -0.5)
    top = jnp.concatenate([q[h * NSA_D:(h + 1) * NSA_D, :] for h in range(NSA_HPG)], axis=1).astype(BF16)
    return jnp.concatenate([top, arow_ref[0]], axis=0)


def _heads_to_rows(o):
    tq = o.shape[1] // NSA_HPG
    return jnp.concatenate([o[:, h * tq:(h + 1) * tq] for h in range(NSA_HPG)], axis=0)


def _token_pos(i, n):
    return i * NSA_TQ + jnp.bitwise_and(lax.broadcasted_iota(jnp.int32, (1, n), 1), NSA_TQ - 1)


def _cmp_attn_kernel(qT_ref, arow_ref, kc_ref, vcT_ref, o_ref, selT_ref, *, nsel_pad):
    i = pl.program_id(1)
    tq = NSA_TQ
    n = NSA_HPG * tq
    nr = kc_ref.shape[1]
    s = jnp.dot(kc_ref[0], _nsa_queries(qT_ref, arow_ref), preferred_element_type=F32)
    tpos = _token_pos(i, n)
    last_done = jnp.right_shift(tpos - (CMP_LEN - 1), 4)
    valid = lax.broadcasted_iota(jnp.int32, (nr, n), 0) <= last_done
    s = jnp.where(valid, s, NEG)
    e = jnp.where(valid, jnp.exp(s - jnp.max(s, axis=0, keepdims=True)), 0.0)
    l = jnp.sum(e, axis=0, keepdims=True)
    inv = 1.0 / jnp.where(l > 0.0, l, 1.0)
    acc = jnp.dot(vcT_ref[0], e.astype(BF16), preferred_element_type=F32)
    o_ref[...] = _heads_to_rows(acc * inv)

    pn = e * inv
    psum = pn[:, 0:tq]
    for h in range(1, NSA_HPG):
        psum = psum + pn[:, h * tq:(h + 1) * tq]
    j2 = lax.broadcasted_iota(jnp.int32, (nsel_pad, nr), 0)
    n2 = lax.broadcasted_iota(jnp.int32, (nsel_pad, nr), 1)
    ov = jnp.logical_and(n2 * CMP_STRIDE <= j2 * SEL_LEN + (SEL_LEN - 1),
                         n2 * CMP_STRIDE + (CMP_LEN - 1) >= j2 * SEL_LEN)
    ov = jnp.logical_and(ov, n2 < nr - 1)
    imp = jnp.dot(jnp.where(ov, 1.0, 0.0).astype(F32), psum, precision=lax.Precision.HIGHEST,
                  preferred_element_type=F32)

    blk = lax.broadcasted_iota(jnp.int32, (nsel_pad, tq), 0)
    cur = jnp.right_shift(i * tq + lax.broadcasted_iota(jnp.int32, (nsel_pad, tq), 1), SEL_SHIFT)
    forced = jnp.logical_or(blk == 0, jnp.logical_or(blk == cur, blk == cur - 1))
    score = jnp.where(blk <= cur, imp + jnp.where(forced, FORCE_BONUS, 0.0), NEG)
    blk_f = blk.astype(F32)

    def pick(_, carry):
        sc, sel = carry
        mx = jnp.max(sc, axis=0, keepdims=True)
        first = jnp.min(jnp.where(sc == mx, blk_f, float(nsel_pad)), axis=0, keepdims=True)
        hit = blk_f == first
        sel = jnp.where(jnp.logical_and(hit, mx > NEG * 0.5), 1.0, sel)
        return jnp.where(hit, REMOVED, sc), sel

    _, sel = lax.fori_loop(0, SEL_TOPK, pick, (score, jnp.zeros((nsel_pad, tq), F32)))
    selT_ref[0] = sel


def _cmp_attn(fm, arows, kc, vcT, nsel_pad):
    T = fm.shape[1]
    nr = kc.shape[1]
    tq = NSA_TQ
    n = NSA_HPG * tq
    return pl.pallas_call(
        functools.partial(_cmp_attn_kernel, nsel_pad=nsel_pad),
        out_shape=(
            jax.ShapeDtypeStruct((NSA_HEADS * NSA_D, T), F32),
            jax.ShapeDtypeStruct((NSA_G, nsel_pad, T), F32),
        ),
        grid=(NSA_G, T // tq),
        in_specs=[
            pl.BlockSpec((NSA_HPG * NSA_D, tq), lambda g, i: (g, i)),
            pl.BlockSpec((1, NSA_D, n), lambda g, i: (g, 0, 0)),
            pl.BlockSpec((1, nr, 2 * NSA_D), lambda g, i: (g, 0, 0)),
            pl.BlockSpec((1, NSA_D, nr), lambda g, i: (g, 0, 0)),
        ],
        out_specs=(
            pl.BlockSpec((NSA_HPG * NSA_D, tq), lambda g, i: (g, i)),
            pl.BlockSpec((1, nsel_pad, tq), lambda g, i: (g, 0, i)),
        ),
        compiler_params=_cparams(("parallel", "parallel")),
        name="nsa_cmp_attn",
    )(fm, arows, kc, vcT)


def _nsa_prep_kernel(ks_ref, vsT_ref, kw_ref, vwT_ref, kas_ref, vas_ref, kaw_ref, vaw_ref, *, tk, nsel_pad):
    c = pl.program_id(1)
    r = lax.broadcasted_iota(jnp.int32, (tk, NSA_D), 0)
    blk = lax.broadcasted_iota(jnp.int32, (tk, nsel_pad), 1)
    kblk = jnp.right_shift(c * tk + lax.broadcasted_iota(jnp.int32, (tk, nsel_pad), 0), SEL_SHIFT)
    onehot = jnp.where(blk == kblk, 1.0, 0.0)
    kas_ref[0] = jnp.concatenate([ks_ref[0], _digit_cols(r, 1), onehot], axis=-1).astype(BF16)
    kaw_ref[0] = jnp.concatenate([kw_ref[0], _digit_cols(jnp.bitwise_and(r, NSA_TQ - 1), 1)],
                                 axis=-1).astype(BF16)
    ones_row = jnp.where(lax.broadcasted_iota(jnp.int32, (SEL_VROWS - NSA_D, tk), 0) == 0, 1.0, 0.0)
    vas_ref[0, 0] = jnp.concatenate([vsT_ref[...], ones_row], axis=0).astype(BF16)
    vaw_ref[0] = jnp.concatenate([vwT_ref[...], ones_row], axis=0).astype(BF16)


def _nsa_prep(hm, fm, tk, nsel_pad):
    T = hm.shape[1]
    kw = 2 * NSA_D + nsel_pad
    return pl.pallas_call(
        functools.partial(_nsa_prep_kernel, tk=tk, nsel_pad=nsel_pad),
        out_shape=(
            jax.ShapeDtypeStruct((NSA_G, T, kw), BF16),
            jax.ShapeDtypeStruct((NSA_G, T // tk, SEL_VROWS, tk), BF16),
            jax.ShapeDtypeStruct((NSA_G, T, 2 * NSA_D), BF16),
            jax.ShapeDtypeStruct((NSA_G, SEL_VROWS, T), BF16),
        ),
        grid=(NSA_G, T // tk),
        in_specs=[
            pl.BlockSpec((1, tk, NSA_D), lambda g, c: (20 + 2 * g, c, 0)),
            pl.BlockSpec((NSA_D, tk), lambda g, c: (21 + 2 * g, c)),
            pl.BlockSpec((1, tk, NSA_D), lambda g, c: (24 + 2 * g, c, 0)),
            pl.BlockSpec((NSA_D, tk), lambda g, c: (25 + 2 * g, c)),
        ],
        out_specs=(
            pl.BlockSpec((1, tk, kw), lambda g, c: (g, c, 0)),
            pl.BlockSpec((1, 1, SEL_VROWS, tk), lambda g, c: (g, c, 0, 0)),
            pl.BlockSpec((1, tk, 2 * NSA_D), lambda g, c: (g, c, 0)),
            pl.BlockSpec((1, SEL_VROWS, tk), lambda g, c: (g, 0, c)),
        ),
        compiler_params=_cparams(("parallel", "parallel")),
        name="nsa_prep",
    )(hm, fm, hm, fm)


def _sel_attn_kernel(qT_ref, selT_ref, arow_ref, srow_ref, ka_ref, va_ref, o_ref,
                     qa_scr, s0_scr, s1_scr, p0_scr, p1_scr, al0_scr, al1_scr, m_scr, acc_scr, *, tk, n_chunks):
    i = pl.program_id(1)
    tq = NSA_TQ
    n = NSA_HPG * tq
    nsel_pad = selT_ref.shape[1]
    qa_scr[0:2 * NSA_D, :] = _nsa_queries(qT_ref, arow_ref)
    blk = lax.broadcasted_iota(jnp.int32, (nsel_pad, tq), 0)
    own = jnp.right_shift(blk, 1) == i
    chosen = jnp.logical_and(selT_ref[0] > 0.5, jnp.logical_not(own))
    qa_scr[2 * NSA_D:, :] = jnp.concatenate([jnp.where(chosen, 0.0, NEG_BF16).astype(BF16)] * NSA_HPG, axis=1)

    tpos = _token_pos(i, n)
    slope_row = srow_ref[0]

    def rterm(c):
        return slope_row * (c * tk - tpos).astype(F32)

    c_own = (i * tq) // tk
    own_start = pl.multiple_of(i * tq, tq)
    s = jnp.dot(ka_ref[0, pl.ds(own_start, tq), 0:2 * NSA_D], qa_scr[0:2 * NSA_D, :], preferred_element_type=F32)
    kpos = i * tq + lax.broadcasted_iota(jnp.int32, s.shape, 0)
    s = jnp.where(kpos <= tpos, s, NEG)
    r_own = rterm(c_own)
    m0 = jnp.max(s, axis=0, keepdims=True) + r_own
    p = jnp.exp(s - (m0 - r_own)).astype(BF16)
    v_own = va_ref[0, c_own, :, pl.ds(pl.multiple_of(i * tq - c_own * tk, tq), tq)]
    acc_scr[...] = jnp.dot(v_own, p, preferred_element_type=F32)
    m_scr[...] = m0

    s_slots = (s0_scr, s1_scr)
    p_slots = (p0_scr, p1_scr)
    al_slots = (al0_scr, al1_scr)

    def scores(j, slot):
        start = pl.multiple_of(j * tk, tk)
        s_slots[slot][...] = jnp.dot(ka_ref[0, pl.ds(start, tk), :], qa_scr[...], preferred_element_type=F32)

    def softmax(j, slot):
        sj = s_slots[slot][...]
        r = rterm(j)
        m_prev = m_scr[...]
        m_new = jnp.maximum(m_prev, jnp.max(sj, axis=0, keepdims=True) + r)
        p_slots[slot][...] = jnp.exp(sj - (m_new - r)).astype(BF16)
        al_slots[slot][...] = jnp.exp(m_prev - m_new)
        m_scr[...] = m_new

    def values(j, slot):
        acc_scr[...] = (al_slots[slot][...] * acc_scr[...]
                        + jnp.dot(va_ref[0, j], p_slots[slot][...], preferred_element_type=F32))

    n_pairs = jnp.minimum(((i * tq + tq - 1) // tk + 2) // 2, n_chunks // 2)
    scores(0, 0)
    scores(1, 1)
    softmax(0, 0)

    def body(jj, carry):
        j = 2 * jj
        scores(j, 0)
        softmax(j - 1, 1)
        values(j - 2, 0)
        scores(j + 1, 1)
        softmax(j, 0)
        values(j - 1, 1)
        return carry

    lax.fori_loop(1, n_pairs, body, 0)
    softmax(2 * n_pairs - 1, 1)
    values(2 * n_pairs - 2, 0)
    values(2 * n_pairs - 1, 1)
    acc = acc_scr[...]
    o_ref[...] = _heads_to_rows(acc[0:NSA_D, :] * (1.0 / acc[NSA_D:NSA_D + 1, :]))


def _sel_attn(fm, selT, arows, srow, ka, va):
    _, T, kw = ka.shape
    tk = va.shape[-1]
    tq = NSA_TQ
    nsel_pad = selT.shape[1]
    n = NSA_HPG * tq
    return pl.pallas_call(
        functools.partial(_sel_attn_kernel, tk=tk, n_chunks=T // tk),
        out_shape=jax.ShapeDtypeStruct((NSA_HEADS * NSA_D, T), F32),
        grid=(NSA_G, T // tq),
        in_specs=[
            pl.BlockSpec((NSA_HPG * NSA_D, tq), lambda g, i: (g, i)),
            pl.BlockSpec((1, nsel_pad, tq), lambda g, i: (g, 0, i)),
            pl.BlockSpec((1, NSA_D, n), lambda g, i: (g, 0, 0)),
            pl.BlockSpec((1, 1, n), lambda g, i: (g, 0, 0)),
            pl.BlockSpec((1, T, kw), lambda g, i: (g, 0, 0)),
            pl.BlockSpec((1, T // tk, SEL_VROWS, tk), lambda g, i: (g, 0, 0, 0)),
        ],
        out_specs=pl.BlockSpec((NSA_HPG * NSA_D, tq), lambda g, i: (g, i)),
        scratch_shapes=[
            pltpu.VMEM((kw, n), BF16),
            pltpu.VMEM((tk, n), F32),
            pltpu.VMEM((tk, n), F32),
            pltpu.VMEM((tk, n), BF16),
            pltpu.VMEM((tk, n), BF16),
            pltpu.VMEM((1, n), F32),
            pltpu.VMEM((1, n), F32),
            pltpu.VMEM((1, n), F32),
            pltpu.VMEM((SEL_VROWS, n), F32),
        ],
        compiler_params=_cparams(("parallel", "arbitrary")),
        name="nsa_sel_attn",
    )(fm, selT, arows, srow, ka, va)


WIN_KEYS = WINDOW + NSA_TQ


def _win_attn_kernel(qT_ref, arow_ref, srow_ref, ka_ref, va_ref, o_ref):
    i = pl.program_id(1)
    tq = NSA_TQ
    n = NSA_HPG * tq
    start = pl.multiple_of(jnp.maximum(i * tq - WINDOW, 0), tq)
    s = jnp.dot(ka_ref[0, pl.ds(start, WIN_KEYS), :], _nsa_queries(qT_ref, arow_ref), preferred_element_type=F32)
    tpos = _token_pos(i, n)
    kpos = start + lax.broadcasted_iota(jnp.int32, s.shape, 0)
    allowed = jnp.logical_and(kpos <= tpos, kpos > tpos - WINDOW)
    s = jnp.where(allowed, s, NEG)
    slope_row = srow_ref[0]
    tiles = [s[j * tq:(j + 1) * tq, :] for j in range(WIN_KEYS // tq)]
    rterms = [slope_row * (start + j * tq - tpos).astype(F32) for j in range(WIN_KEYS // tq)]
    m = jnp.max(tiles[0], axis=0, keepdims=True) + rterms[0]
    for sj, rj in zip(tiles[1:], rterms[1:]):
        m = jnp.maximum(m, jnp.max(sj, axis=0, keepdims=True) + rj)
    p = jnp.concatenate([jnp.exp(sj - (m - rj)) for sj, rj in zip(tiles, rterms)], axis=0).astype(BF16)
    acc = jnp.dot(va_ref[0, :, pl.ds(start, WIN_KEYS)], p, preferred_element_type=F32)
    o_ref[...] = _heads_to_rows(acc[0:NSA_D, :] * (1.0 / acc[NSA_D:NSA_D + 1, :]))


def _win_attn(fm, arows, srow, ka, va):
    _, T, kw = ka.shape
    assert T >= WIN_KEYS
    tq = NSA_TQ
    n = NSA_HPG * tq
    return pl.pallas_call(
        _win_attn_kernel,
        out_shape=jax.ShapeDtypeStruct((NSA_HEADS * NSA_D, T), F32),
        grid=(NSA_G, T // tq),
        in_specs=[
            pl.BlockSpec((NSA_HPG * NSA_D, tq), lambda g, i: (g, i)),
            pl.BlockSpec((1, NSA_D, n), lambda g, i: (g, 0, 0)),
            pl.BlockSpec((1, 1, n), lambda g, i: (g, 0, 0)),
            pl.BlockSpec((1, T, kw), lambda g, i: (g, 0, 0)),
            pl.BlockSpec((1, SEL_VROWS, T), lambda g, i: (g, 0, 0)),
        ],
        out_specs=pl.BlockSpec((NSA_HPG * NSA_D, tq), lambda g, i: (g, i)),
        compiler_params=_cparams(("parallel", "parallel")),
        name="nsa_win_attn",
    )(fm, arows, srow, ka, va)


def _out_proj_kernel(x_ref, ymT_ref, ocT_ref, osT_ref, owT_ref, glT_ref, mnw_ref, nnw_ref, wo_ref, o_ref):
    gates = 1.0 / (1.0 + jnp.exp(-glT_ref[...]))
    heads = []
    for h in range(NSA_HEADS):
        rows = slice(h * NSA_D, (h + 1) * NSA_D)
        heads.append(gates[3 * h:3 * h + 1, :] * ocT_ref[rows, :]
                     + gates[3 * h + 1:3 * h + 2, :] * osT_ref[rows, :]
                     + gates[3 * h + 2:3 * h + 3, :] * owT_ref[rows, :])
    ynT = jnp.concatenate(heads, axis=0)

    def rms_cols(y, w_col):
        return (y * lax.rsqrt(jnp.mean(y * y, axis=0, keepdims=True) + EPS) * w_col).astype(BF16)

    half = MLA_HEADS * MLA_V
    o_ref[...] = (x_ref[...]
                  + lax.dot_general(rms_cols(ymT_ref[...], mnw_ref[...]), wo_ref[:half, :], TN_DIMS,
                                    preferred_element_type=F32)
                  + lax.dot_general(rms_cols(ynT, nnw_ref[...]), wo_ref[half:, :], TN_DIMS,
                                    preferred_element_type=F32))


def _out_proj(x2, y_mlaT, o_cT, o_sT, o_wT, glT, mnw, nnw, wo, tm=256):
    T = x2.shape[0]
    half = MLA_HEADS * MLA_V
    row = lambda i: (i, 0)
    col = lambda i: (0, i)
    fixed2 = lambda i: (0, 0)
    return pl.pallas_call(
        _out_proj_kernel,
        out_shape=jax.ShapeDtypeStruct((T, D_MODEL), F32),
        grid=(T // tm,),
        in_specs=[
            pl.BlockSpec((tm, D_MODEL), row),
            pl.BlockSpec((half, tm), col),
            pl.BlockSpec((half, tm), col),
            pl.BlockSpec((half, tm), col),
            pl.BlockSpec((half, tm), col),
            pl.BlockSpec((LANES, tm), col),
            pl.BlockSpec((half, 1), fixed2),
            pl.BlockSpec((half, 1), fixed2),
            pl.BlockSpec((2 * half, D_MODEL), fixed2),
        ],
        out_specs=pl.BlockSpec((tm, D_MODEL), row),
        compiler_params=_cparams(("parallel",)),
        name="out_proj",
    )(x2, y_mlaT, o_cT, o_sT, o_wT, glT, mnw.reshape(-1, 1), nnw.reshape(-1, 1), wo)


def _mlp_kernel(h_ref, nw_ref, wu_ref, wd_ref, fw_ref, o_ref, n_scr, acc_scr, *, final):
    c = pl.program_id(1)

    @pl.when(c == 0)
    def _():
        n_scr[...] = _rms(h_ref[...], nw_ref[...]).astype(BF16)
        acc_scr[...] = jnp.zeros(acc_scr.shape, F32)

    u = jnp.dot(n_scr[...], wu_ref[...], preferred_element_type=F32)
    a = jnp.square(jnp.maximum(u, 0.0)).astype(BF16)
    acc_scr[...] += jnp.dot(a, wd_ref[...], preferred_element_type=F32)

    @pl.when(c == pl.num_programs(1) - 1)
    def _():
        h2 = h_ref[...] + acc_scr[...]
        o_ref[...] = _rms(h2, fw_ref[...]) if final else h2


def _mlp(h1, nw, wu, wd, fw, final, tm=512, tc=1024):
    T = h1.shape[0]
    tm = min(tm, T)
    return pl.pallas_call(
        functools.partial(_mlp_kernel, final=final),
        out_shape=jax.ShapeDtypeStruct((T, D_MODEL), F32),
        grid=(T // tm, MLP_HIDDEN // tc),
        in_specs=[
            pl.BlockSpec((tm, D_MODEL), lambda i, c: (i, 0)),
            pl.BlockSpec((1, D_MODEL), lambda i, c: (0, 0)),
            pl.BlockSpec((D_MODEL, tc), lambda i, c: (0, c)),
            pl.BlockSpec((tc, D_MODEL), lambda i, c: (c, 0)),
            pl.BlockSpec((1, D_MODEL), lambda i, c: (0, 0)),
        ],
        out_specs=pl.BlockSpec((tm, D_MODEL), lambda i, c: (i, 0)),
        scratch_shapes=[pltpu.VMEM((tm, D_MODEL), BF16), pltpu.VMEM((tm, D_MODEL), F32)],
        compiler_params=_cparams(("parallel", "arbitrary")),
        name="mlp",
    )(h1, nw.reshape(1, -1), wu, wd, fw.reshape(1, -1))


def _regroup_w_in(w_in):
    cq = w_in[:, 0:512]
    ckv = w_in[:, 512:768]
    kr = w_in[:, 768:832]
    rest = w_in[:, 832:2624]
    gates = w_in[:, 2624:2672]
    x1, x2 = kr[:, :32], kr[:, 32:]
    zeros = jnp.zeros((w_in.shape[0], IN_PAD - SEC_GATE - 48), w_in.dtype)
    return jnp.concatenate([cq, ckv, rest, x1, x2, x2, x1, gates, zeros], axis=1).astype(BF16)


def _regroup_w_uq(w_uq):
    w = w_uq.reshape(MLA_Q_RANK, MLA_HEADS, MLA_NOPE + MLA_ROPE)
    nope = w[:, :, :MLA_NOPE]
    x1 = w[:, :, MLA_NOPE:MLA_NOPE + 32]
    x2 = w[:, :, MLA_NOPE + 32:]
    return jnp.concatenate([nope, x1, x2, x2, x1], axis=-1).reshape(MLA_Q_RANK, MLA_HEADS * MLA_QK_PAD).astype(BF16)


def kernel(x, positions, attn_norm_w, w_in, mla_q_norm_w, mla_w_uq, mla_kv_norm_w, mla_w_ukv, cmp_pos_k,
           cmp_pos_v, cmp_w1_k, cmp_w2_k, cmp_w1_v, cmp_w2_v, mla_out_norm_w, nsa_out_norm_w, w_o, mlp_norm_w,
           w_up, w_down, final_norm_w):
    B, T, _ = x.shape
    depth = w_in.shape[0]
    inv = 1.0 / (ROPE_THETA ** (jnp.arange(0, MLA_ROPE, 2, dtype=F32) / MLA_ROPE))
    inv128 = jnp.tile(inv, 4).reshape(1, LANES)
    start = 2.0 ** (-8.0 / NSA_HEADS)
    slopes = start ** jnp.arange(1, NSA_HEADS + 1, dtype=F32)
    arows, srow = _alibi_rows(slopes)
    nsel_pad = -(-(T // SEL_LEN) // LANES) * LANES
    n_slabs = (SEC_ROPE - SEC_HM) // NSA_D

    outs = []
    for b in range(B):
        h = x[b]
        pos2 = positions[b].reshape(T, 1)
        for l in range(depth):
            proj = _in_proj(h, attn_norm_w[l], _regroup_w_in(w_in[l]))
            wukv = mla_w_ukv[l].reshape(MLA_KV_RANK, MLA_HEADS, MLA_NOPE + MLA_V)
            wuk = wukv[:, :, :MLA_NOPE].reshape(MLA_KV_RANK, MLA_HEADS * MLA_NOPE).astype(BF16)
            wuvT = wukv[:, :, MLA_NOPE:].reshape(MLA_KV_RANK, MLA_HEADS * MLA_V).T.astype(BF16)
            qT, k, vT = _mla_prep(proj, pos2, mla_q_norm_w[l], mla_kv_norm_w[l], _regroup_w_uq(mla_w_uq[l]).T,
                                  wuk, wuvT, inv128)
            y_mla = _mla_attn(qT, k, vT)

            hm = proj[:, SEC_HM:SEC_ROPE].reshape(T, n_slabs, NSA_D).transpose(1, 0, 2)
            fm = proj[:, SEC_HM:SEC_ROPE].T
            glT = proj[:, SEC_GATE:SEC_GATE + LANES].T
            r4 = hm[16:20].reshape(4, T // CMP_STRIDE, CMP_STRIDE * NSA_D)
            half = CMP_STRIDE * NSA_D
            pos_kv = jnp.stack([cmp_pos_k[l].reshape(2, half), cmp_pos_v[l].reshape(2, half)])
            kc, vcT = _compress(r4, pos_kv, jnp.stack([cmp_w1_k[l], cmp_w1_v[l]]),
                                jnp.stack([cmp_w2_k[l], cmp_w2_v[l]]))
            ka_s, va_s, ka_w, va_w = _nsa_prep(hm, fm, min(SEL_TK, T), nsel_pad)
            o_cT, selT = _cmp_attn(fm, arows, kc, vcT, nsel_pad)
            o_sT = _sel_attn(fm, selT, arows, srow, ka_s, va_s)
            o_wT = _win_attn(fm, arows, srow, ka_w, va_w)

            h = _out_proj(h, y_mla, o_cT, o_sT, o_wT, glT, mla_out_norm_w[l], nsa_out_norm_w[l],
                          w_o[l].astype(BF16))
            h = _mlp(h, mlp_norm_w[l], w_up[l].astype(BF16), w_down[l].astype(BF16), final_norm_w,
                     final=(l == depth - 1))
        outs.append(h)
    return jnp.stack(outs)
```

```python
import functools

import numpy as np
import jax
import jax.numpy as jnp
from jax import lax
from jax.experimental import pallas as pl
from jax.experimental.pallas import tpu as pltpu

F32 = jnp.float32
BF16 = jnp.bfloat16

EPS = 1e-6
NEG = -1e30
REMOVED = -3e38

D_MODEL = 2048
MLA_HEADS = 8
MLA_Q_RANK = 512
MLA_KV_RANK = 256
MLA_NOPE = 128
MLA_ROPE = 64
MLA_V = 128
ROPE_THETA = 10000.0
MLA_QK_PAD = 256
MLA_VROWS = 144

NSA_HEADS = 16
NSA_G = 2
NSA_HPG = 8
NSA_D = 64
CMP_LEN = 32
CMP_STRIDE = 16
CMP_HIDDEN = 256
SEL_LEN = 64
SEL_SHIFT = 6
SEL_TOPK = 16
WINDOW = 512
FORCE_BONUS = 1e4
MLP_HIDDEN = 4 * D_MODEL

LOG2E = 1.4426950408889634
LANES = 128
SUBLANES = 8
BF16_ROWS = 16
NSA_TQ = 128
SEL_TK = 512
MLA_TQ = 512
MLA_TK = 512
VMEM_LIMIT = 56 * 1024 * 1024

SEC_CQ = 0
SEC_CKV = 512
SEC_ROPE = 768
SEC_CMP = 896
SEC_KS = 1152
SEC_KW = 1280
TM_COLS = 1408
MLA_COLS = SEC_CMP
ROW_Q = 0
ROW_VS = 1024
ROW_VW = 1152
ROW_GATE = 1280
FM_ROWS = 1408


def _cparams(sem):
    return pltpu.CompilerParams(dimension_semantics=sem, vmem_limit_bytes=VMEM_LIMIT)


def _rms(x, w):
    return x * lax.rsqrt(jnp.mean(x * x, axis=-1, keepdims=True) + EPS) * w


ALIBI_PARTS = 4
ALIBI_HI = 16
NEG_BF16 = -1e30
SEL_VROWS = 80
NT_DIMS = (((1,), (1,)), ((), ()))
TN_DIMS = (((0,), (0,)), ((), ()))


def _digit_cols(offset, scale):
    lane = lax.broadcasted_iota(jnp.int32, offset.shape, 1)
    hi = jnp.right_shift(offset, 4) * scale
    lo = jnp.bitwise_and(offset, ALIBI_HI - 1) * scale
    return jnp.where(lane < ALIBI_PARTS, hi, jnp.where(lane < 2 * ALIBI_PARTS, lo, 0)).astype(F32)


def _in_proj_kernel(x_ref, nw_ref, w_ref, wT_ref, pm_ref, r_ref, kas_ref, kaw_ref, qT_ref, vas_ref, vaw_ref,
                    glT_ref, *, nsel_pad):
    c = pl.program_id(0)
    tm = x_ref.shape[0]
    n = _rms(x_ref[...], nw_ref[...]).astype(BF16)
    tok = jnp.dot(n, w_ref[...], preferred_element_type=F32)
    fea = lax.dot_general(wT_ref[...], n, NT_DIMS, preferred_element_type=F32)

    pm_ref[...] = tok[:, :MLA_COLS]
    for a in range(2 * NSA_G):
        r_ref[a] = tok[:, SEC_CMP + a * NSA_D:SEC_CMP + (a + 1) * NSA_D]
    r = lax.broadcasted_iota(jnp.int32, (tm, NSA_D), 0)
    blk = lax.broadcasted_iota(jnp.int32, (tm, nsel_pad), 1)
    kblk = jnp.right_shift(c * tm + lax.broadcasted_iota(jnp.int32, (tm, nsel_pad), 0), SEL_SHIFT)
    onehot = jnp.where(blk == kblk, 1.0, 0.0)
    sel_digits = _digit_cols(r, 1)
    win_digits = _digit_cols(jnp.bitwise_and(r, NSA_TQ - 1), 1)
    ones_row = jnp.where(lax.broadcasted_iota(jnp.int32, (SEL_VROWS - NSA_D, tm), 0) == 0, 1.0, 0.0)
    for g in range(NSA_G):
        ks = tok[:, SEC_KS + g * NSA_D:SEC_KS + (g + 1) * NSA_D]
        kw = tok[:, SEC_KW + g * NSA_D:SEC_KW + (g + 1) * NSA_D]
        kas_ref[g] = jnp.concatenate([ks, sel_digits, onehot], axis=-1).astype(BF16)
        kaw_ref[g] = jnp.concatenate([kw, win_digits], axis=-1).astype(BF16)
        vas_ref[g, 0] = jnp.concatenate([fea[ROW_VS + g * NSA_D:ROW_VS + (g + 1) * NSA_D, :], ones_row],
                                        axis=0).astype(BF16)
        vaw_ref[g] = jnp.concatenate([fea[ROW_VW + g * NSA_D:ROW_VW + (g + 1) * NSA_D, :], ones_row],
                                     axis=0).astype(BF16)
    qT_ref[...] = fea[ROW_Q:ROW_VS, :]
    glT_ref[...] = fea[ROW_GATE:, :]


def _in_proj(x2, attn_norm_w, w_tok, w_feaT, nsel_pad, tm=SEL_TK):
    T = x2.shape[0]
    tm = min(tm, T)
    kw = 2 * NSA_D + nsel_pad
    fixed = lambda i: (0, 0)
    return pl.pallas_call(
        functools.partial(_in_proj_kernel, nsel_pad=nsel_pad),
        out_shape=(
            jax.ShapeDtypeStruct((T, MLA_COLS), F32),
            jax.ShapeDtypeStruct((2 * NSA_G, T, NSA_D), F32),
            jax.ShapeDtypeStruct((NSA_G, T, kw), BF16),
            jax.ShapeDtypeStruct((NSA_G, T, 2 * NSA_D), BF16),
            jax.ShapeDtypeStruct((NSA_HEADS * NSA_D, T), F32),
            jax.ShapeDtypeStruct((NSA_G, T // tm, SEL_VROWS, tm), BF16),
            jax.ShapeDtypeStruct((NSA_G, SEL_VROWS, T), BF16),
            jax.ShapeDtypeStruct((LANES, T), F32),
        ),
        grid=(T // tm,),
        in_specs=[
            pl.BlockSpec((tm, D_MODEL), lambda i: (i, 0)),
            pl.BlockSpec((1, D_MODEL), fixed),
            pl.BlockSpec((D_MODEL, TM_COLS), fixed),
            pl.BlockSpec((FM_ROWS, D_MODEL), fixed),
        ],
        out_specs=(
            pl.BlockSpec((tm, MLA_COLS), lambda i: (i, 0)),
            pl.BlockSpec((2 * NSA_G, tm, NSA_D), lambda i: (0, i, 0)),
            pl.BlockSpec((NSA_G, tm, kw), lambda i: (0, i, 0)),
            pl.BlockSpec((NSA_G, tm, 2 * NSA_D), lambda i: (0, i, 0)),
            pl.BlockSpec((NSA_HEADS * NSA_D, tm), lambda i: (0, i)),
            pl.BlockSpec((NSA_G, 1, SEL_VROWS, tm), lambda i: (0, i, 0, 0)),
            pl.BlockSpec((NSA_G, SEL_VROWS, tm), lambda i: (0, 0, i)),
            pl.BlockSpec((LANES, tm), lambda i: (0, i)),
        ),
        compiler_params=_cparams(("parallel",)),
        name="in_proj",
    )(x2, attn_norm_w.reshape(1, D_MODEL), w_tok, w_feaT)


def _mla_prep_kernel(cq_ref, ckv_ref, kr_ref, pos_ref, qnw_ref, kvnw_ref, wuqT_ref, wuk_ref, wuvT_ref, inv_ref,
                     qT_ref, k_ref, vT_ref):
    scale = (MLA_NOPE + MLA_ROPE) ** -0.5 * LOG2E
    qn = _rms(cq_ref[...], qnw_ref[...]).astype(BF16)
    qT = lax.dot_general(wuqT_ref[...], qn, NT_DIMS, preferred_element_type=F32)
    kvn = _rms(ckv_ref[...], kvnw_ref[...]).astype(BF16)
    kn = jnp.dot(kvn, wuk_ref[...], preferred_element_type=F32)
    vT = lax.dot_general(wuvT_ref[...], kvn, NT_DIMS, preferred_element_type=F32)

    ang = pos_ref[...].astype(F32) * inv_ref[...]
    c = jnp.cos(ang)
    s = jnp.sin(ang)
    lane = lax.broadcasted_iota(jnp.int32, ang.shape, 1)
    coef = jnp.where(lane < 64, c, jnp.where(lane < 96, -s, s))
    coefT = coef.T

    ones_row = jnp.where(lax.broadcasted_iota(jnp.int32, (MLA_VROWS - MLA_V, coefT.shape[1]), 0) == 0, 1.0, 0.0)
    t = kr_ref[...] * coef
    kpe = jnp.where(lane < 64, t + pltpu.roll(t, 64, 1), 0.0)
    for h in range(MLA_HEADS):
        base = h * MLA_QK_PAD
        tT = qT[base + 128:base + 256, :] * coefT
        ropeT = tT + pltpu.roll(tT, 64, 0)
        qT_ref[h] = (jnp.concatenate([qT[base:base + 128, :], ropeT], axis=0) * scale).astype(BF16)
        k_ref[h] = jnp.concatenate([kn[:, h * 128:(h + 1) * 128], kpe], axis=-1).astype(BF16)
        vT_ref[h] = jnp.concatenate([vT[h * 128:(h + 1) * 128, :], ones_row], axis=0).astype(BF16)


def _mla_prep(pm, pos2, qnw, kvnw, wuqT, wuk, wuvT, inv128, tm=256):
    T = pm.shape[0]
    H = MLA_HEADS
    fixed = lambda i: (0, 0)
    return pl.pallas_call(
        _mla_prep_kernel,
        out_shape=(
            jax.ShapeDtypeStruct((H, MLA_QK_PAD, T), BF16),
            jax.ShapeDtypeStruct((H, T, MLA_QK_PAD), BF16),
            jax.ShapeDtypeStruct((H, MLA_VROWS, T), BF16),
        ),
        grid=(T // tm,),
        in_specs=[
            pl.BlockSpec((tm, MLA_Q_RANK), lambda i: (i, SEC_CQ // MLA_Q_RANK)),
            pl.BlockSpec((tm, MLA_KV_RANK), lambda i: (i, SEC_CKV // MLA_KV_RANK)),
            pl.BlockSpec((tm, LANES), lambda i: (i, SEC_ROPE // LANES)),
            pl.BlockSpec((tm, 1), lambda i: (i, 0)),
            pl.BlockSpec((1, MLA_Q_RANK), fixed),
            pl.BlockSpec((1, MLA_KV_RANK), fixed),
            pl.BlockSpec((H * MLA_QK_PAD, MLA_Q_RANK), fixed),
            pl.BlockSpec((MLA_KV_RANK, H * MLA_NOPE), fixed),
            pl.BlockSpec((H * MLA_V, MLA_KV_RANK), fixed),
            pl.BlockSpec((1, LANES), fixed),
        ],
        out_specs=(
            pl.BlockSpec((H, MLA_QK_PAD, tm), lambda i: (0, 0, i)),
            pl.BlockSpec((H, tm, MLA_QK_PAD), lambda i: (0, i, 0)),
            pl.BlockSpec((H, MLA_VROWS, tm), lambda i: (0, 0, i)),
        ),
        compiler_params=_cparams(("parallel",)),
        name="mla_prep",
    )(pm, pm, pm, pos2, qnw.reshape(1, -1), kvnw.reshape(1, -1), wuqT, wuk, wuvT, inv128)


def _mla_attn_kernel(qi_ref, ki_ref, qT_ref, k_ref, vT_ref, o_ref, m_scr, acc_scr):
    p_id = pl.program_id(0)
    qi = qi_ref[p_id]
    ki = ki_ref[p_id]

    @pl.when(ki == 0)
    def _():
        m_scr[...] = jnp.full(m_scr.shape, NEG, F32)
        acc_scr[...] = jnp.zeros(acc_scr.shape, F32)

    def step(diag):
        for h in range(MLA_HEADS):
            s = jnp.dot(k_ref[h], qT_ref[h], preferred_element_type=F32)
            if diag:
                key = lax.broadcasted_iota(jnp.int32, s.shape, 0)
                qry = lax.broadcasted_iota(jnp.int32, s.shape, 1)
                s = jnp.where(key <= qry, s, NEG)
            m_prev = m_scr[h]
            m_new = jnp.maximum(m_prev, jnp.max(s, axis=0, keepdims=True))
            p = jnp.exp2(s - m_new).astype(BF16)
            acc = jnp.exp2(m_prev - m_new) * acc_scr[h] + jnp.dot(vT_ref[h], p, preferred_element_type=F32)
            if diag:
                o_ref[h * MLA_V:(h + 1) * MLA_V, :] = acc[:MLA_V, :] * (1.0 / acc[MLA_V:MLA_V + 1, :])
            else:
                m_scr[h] = m_new
                acc_scr[h] = acc

    @pl.when(ki < qi)
    def _():
        step(False)

    @pl.when(ki == qi)
    def _():
        step(True)


def _tri_pairs(nq, per_q):
    qi, ki = [], []
    for i in range(nq):
        for c in range(per_q(i) + 1):
            qi.append(i)
            ki.append(c)
    return jnp.asarray(qi, jnp.int32), jnp.asarray(ki, jnp.int32)


def _mla_attn(qT, k, vT):
    H, T, _ = k.shape
    tq = min(MLA_TQ, T)
    nq = T // tq
    qi, ki = _tri_pairs(nq, lambda i: i)
    grid_spec = pltpu.PrefetchScalarGridSpec(
        num_scalar_prefetch=2,
        grid=(int(qi.shape[0]),),
        in_specs=[
            pl.BlockSpec((H, MLA_QK_PAD, tq), lambda p, qi, ki: (0, 0, qi[p])),
            pl.BlockSpec((H, tq, MLA_QK_PAD), lambda p, qi, ki: (0, ki[p], 0)),
            pl.BlockSpec((H, MLA_VROWS, tq), lambda p, qi, ki: (0, 0, ki[p])),
        ],
        out_specs=pl.BlockSpec((H * MLA_V, tq), lambda p, qi, ki: (0, qi[p])),
        scratch_shapes=[
            pltpu.VMEM((H, 1, tq), F32),
            pltpu.VMEM((H, MLA_VROWS, tq), F32),
        ],
    )
    return pl.pallas_call(
        _mla_attn_kernel,
        out_shape=jax.ShapeDtypeStruct((H * MLA_V, T), F32),
        grid_spec=grid_spec,
        compiler_params=_cparams(("arbitrary",)),
        name="mla_attn",
    )(qi, ki, qT, k, vT)


def _gelu_tanh(x):
    return 0.5 * x * (1.0 + jnp.tanh(np.sqrt(2.0 / np.pi).astype(np.float32) * (x + 0.044715 * (x * x * x))))


def _compress_kernel(r_ref, pos_ref, w1_ref, w2_ref, kc_ref, vcT_ref):
    r = r_ref[0]
    nr = r.shape[0]
    half = CMP_STRIDE * NSA_D
    hp = lax.Precision.HIGHEST
    a = jnp.dot(r + pos_ref[0, 0:1, :], w1_ref[0, :half, :], precision=hp, preferred_element_type=F32)
    b = jnp.dot(r + pos_ref[0, 1:2, :], w1_ref[0, half:, :], precision=hp, preferred_element_type=F32)
    hid = _gelu_tanh(a + pltpu.roll(b, nr - 1, 0))
    out = jnp.dot(hid, w2_ref[0], precision=hp, preferred_element_type=F32)
    row = lax.broadcasted_iota(jnp.int32, out.shape, 0)
    out = jnp.where(row < nr - 1, out, 0.0)
    is_key = pl.program_id(0) % 2 == 0

    @pl.when(is_key)
    def _():
        kc_ref[0] = jnp.concatenate([out, _digit_cols(row, CMP_STRIDE)], axis=-1).astype(BF16)

    @pl.when(jnp.logical_not(is_key))
    def _():
        vcT_ref[0] = jnp.concatenate([out, jnp.zeros_like(out)], axis=-1).T[0:NSA_D, :].astype(BF16)


def _compress(r4, pos_kv, w1_kv, w2_kv):
    _, nr, width = r4.shape
    return pl.pallas_call(
        _compress_kernel,
        out_shape=(
            jax.ShapeDtypeStruct((NSA_G, nr, 2 * NSA_D), BF16),
            jax.ShapeDtypeStruct((NSA_G, NSA_D, nr), BF16),
        ),
        grid=(2 * NSA_G,),
        in_specs=[
            pl.BlockSpec((1, nr, width), lambda a: (a, 0, 0)),
            pl.BlockSpec((1, 2, width), lambda a: (a % 2, 0, 0)),
            pl.BlockSpec((1, 2 * width, CMP_HIDDEN), lambda a: (a % 2, 0, 0)),
            pl.BlockSpec((1, CMP_HIDDEN, NSA_D), lambda a: (a % 2, 0, 0)),
        ],
        out_specs=(
            pl.BlockSpec((1, nr, 2 * NSA_D), lambda a: (a // 2, 0, 0)),
            pl.BlockSpec((1, NSA_D, nr), lambda a: (a // 2, 0, 0)),
        ),
        compiler_params=_cparams(("arbitrary",)),
        name="nsa_compress",
    )(r4, pos_kv, w1_kv, w2_kv)


def _alibi_rows(slopes):
    slopes = slopes * LOG2E
    parts, rest = [], slopes
    for _ in range(ALIBI_PARTS):
        piece = rest.astype(BF16)
        parts.append(piece)
        rest = rest - piece.astype(F32)
    pieces = jnp.stack(parts).astype(F32)
    rows = jnp.concatenate([pieces * ALIBI_HI, pieces, jnp.zeros((NSA_D - 2 * ALIBI_PARTS, NSA_HEADS), F32)])
    rows = jnp.repeat(rows.reshape(NSA_D, NSA_G, NSA_HPG), NSA_TQ, axis=2).transpose(1, 0, 2)
    srow = jnp.repeat(slopes.reshape(NSA_G, 1, NSA_HPG), NSA_TQ, axis=2)
    return rows.astype(BF16), srow


def _nsa_queries(qT_ref, arow_ref):
    q = qT_ref[...] * (NSA_D ** -0.5 * LOG2E)
    top = jnp.concatenate([q[h * NSA_D:(h + 1) * NSA_D, :] for h in range(NSA_HPG)], axis=1).astype(BF16)
    return jnp.concatenate([top, arow_ref[0]], axis=0)


def _heads_to_rows(o):
    tq = o.shape[1] // NSA_HPG
    return jnp.concatenate([o[:, h * tq:(h + 1) * tq] for h in range(NSA_HPG)], axis=0)


def _token_pos(i, n):
    return i * NSA_TQ + jnp.bitwise_and(lax.broadcasted_iota(jnp.int32, (1, n), 1), NSA_TQ - 1)


def _cmp_attn_kernel(qT_ref, arow_ref, kc_ref, vcT_ref, o_ref, selT_ref, *, nsel_pad):
    i = pl.program_id(1)
    tq = NSA_TQ
    n = NSA_HPG * tq
    nr = kc_ref.shape[1]
    s = jnp.dot(kc_ref[0], _nsa_queries(qT_ref, arow_ref), preferred_element_type=F32)
    tpos = _token_pos(i, n)
    last_done = jnp.right_shift(tpos - (CMP_LEN - 1), 4)
    valid = lax.broadcasted_iota(jnp.int32, (nr, n), 0) <= last_done
    s = jnp.where(valid, s, NEG)
    e = jnp.where(valid, jnp.exp2(s - jnp.max(s, axis=0, keepdims=True)), 0.0)
    l = jnp.sum(e, axis=0, keepdims=True)
    inv = 1.0 / jnp.where(l > 0.0, l, 1.0)
    acc = jnp.dot(vcT_ref[0], e.astype(BF16), preferred_element_type=F32)
    o_ref[...] = _heads_to_rows(acc * inv)

    pn = e * inv
    psum = pn[:, 0:tq]
    for h in range(1, NSA_HPG):
        psum = psum + pn[:, h * tq:(h + 1) * tq]
    j2 = lax.broadcasted_iota(jnp.int32, (nsel_pad, nr), 0)
    n2 = lax.broadcasted_iota(jnp.int32, (nsel_pad, nr), 1)
    ov = jnp.logical_and(n2 * CMP_STRIDE <= j2 * SEL_LEN + (SEL_LEN - 1),
                         n2 * CMP_STRIDE + (CMP_LEN - 1) >= j2 * SEL_LEN)
    ov = jnp.logical_and(ov, n2 < nr - 1)
    imp = jnp.dot(jnp.where(ov, 1.0, 0.0).astype(F32), psum, precision=lax.Precision.HIGHEST,
                  preferred_element_type=F32)

    blk = lax.broadcasted_iota(jnp.int32, (nsel_pad, tq), 0)
    cur = jnp.right_shift(i * tq + lax.broadcasted_iota(jnp.int32, (nsel_pad, tq), 1), SEL_SHIFT)
    forced = jnp.logical_or(blk == 0, jnp.logical_or(blk == cur, blk == cur - 1))
    score = jnp.where(blk <= cur, imp + jnp.where(forced, FORCE_BONUS, 0.0), NEG)
    blk_f = blk.astype(F32)

    def pick(_, carry):
        sc, sel = carry
        mx = jnp.max(sc, axis=0, keepdims=True)
        first = jnp.min(jnp.where(sc == mx, blk_f, float(nsel_pad)), axis=0, keepdims=True)
        hit = blk_f == first
        sel = jnp.where(jnp.logical_and(hit, mx > NEG * 0.5), 1.0, sel)
        return jnp.where(hit, REMOVED, sc), sel

    _, sel = lax.fori_loop(0, SEL_TOPK, pick, (score, jnp.zeros((nsel_pad, tq), F32)))
    selT_ref[0] = sel


def _cmp_attn(fm, arows, kc, vcT, nsel_pad):
    T = fm.shape[1]
    nr = kc.shape[1]
    tq = NSA_TQ
    n = NSA_HPG * tq
    return pl.pallas_call(
        functools.partial(_cmp_attn_kernel, nsel_pad=nsel_pad),
        out_shape=(
            jax.ShapeDtypeStruct((NSA_HEADS * NSA_D, T), F32),
            jax.ShapeDtypeStruct((NSA_G, nsel_pad, T), F32),
        ),
        grid=(NSA_G, T // tq),
        in_specs=[
            pl.BlockSpec((NSA_HPG * NSA_D, tq), lambda g, i: (g, i)),
            pl.BlockSpec((1, NSA_D, n), lambda g, i: (g, 0, 0)),
            pl.BlockSpec((1, nr, 2 * NSA_D), lambda g, i: (g, 0, 0)),
            pl.BlockSpec((1, NSA_D, nr), lambda g, i: (g, 0, 0)),
        ],
        out_specs=(
            pl.BlockSpec((NSA_HPG * NSA_D, tq), lambda g, i: (g, i)),
            pl.BlockSpec((1, nsel_pad, tq), lambda g, i: (g, 0, i)),
        ),
        compiler_params=_cparams(("parallel", "parallel")),
        name="nsa_cmp_attn",
    )(fm, arows, kc, vcT)


def _sel_attn_kernel(qT_ref, selT_ref, arow_ref, srow_ref, ka_ref, va_ref, o_ref,
                     qa_scr, s0_scr, s1_scr, top0_scr, top1_scr, p0_scr, p1_scr, al0_scr, al1_scr, m_scr, acc_scr,
                     *, tk, n_chunks):
    i = pl.program_id(1)
    tq = NSA_TQ
    n = NSA_HPG * tq
    nsel_pad = selT_ref.shape[1]
    qa_scr[0:2 * NSA_D, :] = _nsa_queries(qT_ref, arow_ref)
    blk = lax.broadcasted_iota(jnp.int32, (nsel_pad, tq), 0)
    own = jnp.right_shift(blk, 1) == i
    chosen = jnp.logical_and(selT_ref[0] > 0.5, jnp.logical_not(own))
    qa_scr[2 * NSA_D:, :] = jnp.concatenate([jnp.where(chosen, 0.0, NEG_BF16).astype(BF16)] * NSA_HPG, axis=1)

    tpos = _token_pos(i, n)
    slope_row = srow_ref[0]

    def rterm(c):
        return slope_row * (c * tk - tpos).astype(F32)

    c_own = (i * tq) // tk
    own_start = pl.multiple_of(i * tq, tq)
    s = jnp.dot(ka_ref[0, pl.ds(own_start, tq), 0:2 * NSA_D], qa_scr[0:2 * NSA_D, :], preferred_element_type=F32)
    kpos = i * tq + lax.broadcasted_iota(jnp.int32, s.shape, 0)
    s = jnp.where(kpos <= tpos, s, NEG)
    r_own = rterm(c_own)
    m0 = jnp.max(s, axis=0, keepdims=True) + r_own
    p = jnp.exp2(s - (m0 - r_own)).astype(BF16)
    v_own = va_ref[0, c_own, :, pl.ds(pl.multiple_of(i * tq - c_own * tk, tq), tq)]
    acc_scr[...] = jnp.dot(v_own, p, preferred_element_type=F32)
    m_scr[...] = m0

    s_slots = (s0_scr, s1_scr)
    p_slots = (p0_scr, p1_scr)
    al_slots = (al0_scr, al1_scr)

    top_slots = (top0_scr, top1_scr)

    def scores(j, slot):
        start = pl.multiple_of(j * tk, tk)
        s = jnp.dot(ka_ref[0, pl.ds(start, tk), :], qa_scr[...], preferred_element_type=F32)
        s_slots[slot][...] = s
        top_slots[slot][...] = jnp.max(s.reshape(tk // SUBLANES, SUBLANES, n), axis=0)

    def softmax(j, slot):
        r = rterm(j)
        m_prev = m_scr[...]
        s_ref, p_ref = s_slots[slot], p_slots[slot]
        m_new = jnp.maximum(m_prev, jnp.max(top_slots[slot][...], axis=0, keepdims=True) + r)
        shift = m_new - r
        for b in range(tk // BF16_ROWS):
            rows = slice(b * BF16_ROWS, (b + 1) * BF16_ROWS)
            p_ref[rows, :] = jnp.exp2(s_ref[rows, :] - shift).astype(BF16)
        al_slots[slot][...] = jnp.exp2(m_prev - m_new)
        m_scr[...] = m_new

    def values(j, slot):
        acc_scr[...] = (al_slots[slot][...] * acc_scr[...]
                        + jnp.dot(va_ref[0, j], p_slots[slot][...], preferred_element_type=F32))

    n_pairs = jnp.minimum(((i * tq + tq - 1) // tk + 2) // 2, n_chunks // 2)
    scores(0, 0)
    scores(1, 1)
    softmax(0, 0)

    def body(jj, carry):
        j = 2 * jj
        scores(j, 0)
        softmax(j - 1, 1)
        values(j - 2, 0)
        scores(j + 1, 1)
        softmax(j, 0)
        values(j - 1, 1)
        return carry

    lax.fori_loop(1, n_pairs, body, 0)
    softmax(2 * n_pairs - 1, 1)
    values(2 * n_pairs - 2, 0)
    values(2 * n_pairs - 1, 1)
    acc = acc_scr[...]
    o_ref[...] = _heads_to_rows(acc[0:NSA_D, :] * (1.0 / acc[NSA_D:NSA_D + 1, :]))


def _sel_attn(fm, selT, arows, srow, ka, va):
    _, T, kw = ka.shape
    tk = va.shape[-1]
    tq = NSA_TQ
    nsel_pad = selT.shape[1]
    n = NSA_HPG * tq
    return pl.pallas_call(
        functools.partial(_sel_attn_kernel, tk=tk, n_chunks=T // tk),
        out_shape=jax.ShapeDtypeStruct((NSA_HEADS * NSA_D, T), F32),
        grid=(NSA_G, T // tq),
        in_specs=[
            pl.BlockSpec((NSA_HPG * NSA_D, tq), lambda g, i: (g, i)),
            pl.BlockSpec((1, nsel_pad, tq), lambda g, i: (g, 0, i)),
            pl.BlockSpec((1, NSA_D, n), lambda g, i: (g, 0, 0)),
            pl.BlockSpec((1, 1, n), lambda g, i: (g, 0, 0)),
            pl.BlockSpec((1, T, kw), lambda g, i: (g, 0, 0)),
            pl.BlockSpec((1, T // tk, SEL_VROWS, tk), lambda g, i: (g, 0, 0, 0)),
        ],
        out_specs=pl.BlockSpec((NSA_HPG * NSA_D, tq), lambda g, i: (g, i)),
        scratch_shapes=[
            pltpu.VMEM((kw, n), BF16),
            pltpu.VMEM((tk, n), F32),
            pltpu.VMEM((tk, n), F32),
            pltpu.VMEM((SUBLANES, n), F32),
            pltpu.VMEM((SUBLANES, n), F32),
            pltpu.VMEM((tk, n), BF16),
            pltpu.VMEM((tk, n), BF16),
            pltpu.VMEM((1, n), F32),
            pltpu.VMEM((1, n), F32),
            pltpu.VMEM((1, n), F32),
            pltpu.VMEM((SEL_VROWS, n), F32),
        ],
        compiler_params=_cparams(("parallel", "arbitrary")),
        name="nsa_sel_attn",
    )(fm, selT, arows, srow, ka, va)


WIN_KEYS = WINDOW + NSA_TQ


def _win_attn_kernel(qT_ref, arow_ref, srow_ref, ka_ref, va_ref, o_ref):
    i = pl.program_id(1)
    tq = NSA_TQ
    n = NSA_HPG * tq
    start = pl.multiple_of(jnp.maximum(i * tq - WINDOW, 0), tq)
    s = jnp.dot(ka_ref[0, pl.ds(start, WIN_KEYS), :], _nsa_queries(qT_ref, arow_ref), preferred_element_type=F32)
    tpos = _token_pos(i, n)
    kpos = start + lax.broadcasted_iota(jnp.int32, s.shape, 0)
    allowed = jnp.logical_and(kpos <= tpos, kpos > tpos - WINDOW)
    s = jnp.where(allowed, s, NEG)
    slope_row = srow_ref[0]
    tiles = [s[j * tq:(j + 1) * tq, :] for j in range(WIN_KEYS // tq)]
    rterms = [slope_row * (start + j * tq - tpos).astype(F32) for j in range(WIN_KEYS // tq)]
    m = jnp.max(tiles[0], axis=0, keepdims=True) + rterms[0]
    for sj, rj in zip(tiles[1:], rterms[1:]):
        m = jnp.maximum(m, jnp.max(sj, axis=0, keepdims=True) + rj)
    p = jnp.concatenate([jnp.exp2(sj - (m - rj)) for sj, rj in zip(tiles, rterms)], axis=0).astype(BF16)
    acc = jnp.dot(va_ref[0, :, pl.ds(start, WIN_KEYS)], p, preferred_element_type=F32)
    o_ref[...] = _heads_to_rows(acc[0:NSA_D, :] * (1.0 / acc[NSA_D:NSA_D + 1, :]))


def _win_attn(fm, arows, srow, ka, va):
    _, T, kw = ka.shape
    assert T >= WIN_KEYS
    tq = NSA_TQ
    n = NSA_HPG * tq
    return pl.pallas_call(
        _win_attn_kernel,
        out_shape=jax.ShapeDtypeStruct((NSA_HEADS * NSA_D, T), F32),
        grid=(NSA_G, T // tq),
        in_specs=[
            pl.BlockSpec((NSA_HPG * NSA_D, tq), lambda g, i: (g, i)),
            pl.BlockSpec((1, NSA_D, n), lambda g, i: (g, 0, 0)),
            pl.BlockSpec((1, 1, n), lambda g, i: (g, 0, 0)),
            pl.BlockSpec((1, T, kw), lambda g, i: (g, 0, 0)),
            pl.BlockSpec((1, SEL_VROWS, T), lambda g, i: (g, 0, 0)),
        ],
        out_specs=pl.BlockSpec((NSA_HPG * NSA_D, tq), lambda g, i: (g, i)),
        compiler_params=_cparams(("parallel", "parallel")),
        name="nsa_win_attn",
    )(fm, arows, srow, ka, va)


def _out_proj_kernel(x_ref, ymT_ref, ocT_ref, osT_ref, owT_ref, glT_ref, mnw_ref, nnw_ref, wo_ref, o_ref):
    gates = 1.0 / (1.0 + jnp.exp(-glT_ref[...]))
    heads = []
    for h in range(NSA_HEADS):
        rows = slice(h * NSA_D, (h + 1) * NSA_D)
        heads.append(gates[3 * h:3 * h + 1, :] * ocT_ref[rows, :]
                     + gates[3 * h + 1:3 * h + 2, :] * osT_ref[rows, :]
                     + gates[3 * h + 2:3 * h + 3, :] * owT_ref[rows, :])
    ynT = jnp.concatenate(heads, axis=0)

    def rms_cols(y, w_col):
        return (y * lax.rsqrt(jnp.mean(y * y, axis=0, keepdims=True) + EPS) * w_col).astype(BF16)

    half = MLA_HEADS * MLA_V
    o_ref[...] = (x_ref[...]
                  + lax.dot_general(rms_cols(ymT_ref[...], mnw_ref[...]), wo_ref[:half, :], TN_DIMS,
                                    preferred_element_type=F32)
                  + lax.dot_general(rms_cols(ynT, nnw_ref[...]), wo_ref[half:, :], TN_DIMS,
                                    preferred_element_type=F32))


def _out_proj(x2, y_mlaT, o_cT, o_sT, o_wT, glT, mnw, nnw, wo, tm=256):
    T = x2.shape[0]
    half = MLA_HEADS * MLA_V
    row = lambda i: (i, 0)
    col = lambda i: (0, i)
    fixed2 = lambda i: (0, 0)
    return pl.pallas_call(
        _out_proj_kernel,
        out_shape=jax.ShapeDtypeStruct((T, D_MODEL), F32),
        grid=(T // tm,),
        in_specs=[
            pl.BlockSpec((tm, D_MODEL), row),
            pl.BlockSpec((half, tm), col),
            pl.BlockSpec((half, tm), col),
            pl.BlockSpec((half, tm), col),
            pl.BlockSpec((half, tm), col),
            pl.BlockSpec((LANES, tm), col),
            pl.BlockSpec((half, 1), fixed2),
            pl.BlockSpec((half, 1), fixed2),
            pl.BlockSpec((2 * half, D_MODEL), fixed2),
        ],
        out_specs=pl.BlockSpec((tm, D_MODEL), row),
        compiler_params=_cparams(("parallel",)),
        name="out_proj",
    )(x2, y_mlaT, o_cT, o_sT, o_wT, glT, mnw.reshape(-1, 1), nnw.reshape(-1, 1), wo)


def _mlp_kernel(h_ref, nw_ref, wu_ref, wd_ref, fw_ref, o_ref, n_scr, acc_scr, *, final):
    c = pl.program_id(1)

    @pl.when(c == 0)
    def _():
        n_scr[...] = _rms(h_ref[...], nw_ref[...]).astype(BF16)
        acc_scr[...] = jnp.zeros(acc_scr.shape, F32)

    u = jnp.dot(n_scr[...], wu_ref[...], preferred_element_type=F32)
    a = jnp.square(jnp.maximum(u, 0.0)).astype(BF16)
    acc_scr[...] += jnp.dot(a, wd_ref[...], preferred_element_type=F32)

    @pl.when(c == pl.num_programs(1) - 1)
    def _():
        h2 = h_ref[...] + acc_scr[...]
        o_ref[...] = _rms(h2, fw_ref[...]) if final else h2


def _mlp(h1, nw, wu, wd, fw, final, tm=512, tc=1024):
    T = h1.shape[0]
    tm = min(tm, T)
    return pl.pallas_call(
        functools.partial(_mlp_kernel, final=final),
        out_shape=jax.ShapeDtypeStruct((T, D_MODEL), F32),
        grid=(T // tm, MLP_HIDDEN // tc),
        in_specs=[
            pl.BlockSpec((tm, D_MODEL), lambda i, c: (i, 0)),
            pl.BlockSpec((1, D_MODEL), lambda i, c: (0, 0)),
            pl.BlockSpec((D_MODEL, tc), lambda i, c: (0, c)),
            pl.BlockSpec((tc, D_MODEL), lambda i, c: (c, 0)),
            pl.BlockSpec((1, D_MODEL), lambda i, c: (0, 0)),
        ],
        out_specs=pl.BlockSpec((tm, D_MODEL), lambda i, c: (i, 0)),
        scratch_shapes=[pltpu.VMEM((tm, D_MODEL), BF16), pltpu.VMEM((tm, D_MODEL), F32)],
        compiler_params=_cparams(("parallel", "arbitrary")),
        name="mlp",
    )(h1, nw.reshape(1, -1), wu, wd, fw.reshape(1, -1))


def _regroup_w_in(w_in):
    cq = w_in[:, 0:512]
    ckv = w_in[:, 512:768]
    kr = w_in[:, 768:832]
    q = w_in[:, 832:1856]
    kv_cmp = w_in[:, 1856:2112]
    kv_slc = w_in[:, 2112:2368].reshape(-1, NSA_G, 2, NSA_D)
    kv_win = w_in[:, 2368:2624].reshape(-1, NSA_G, 2, NSA_D)
    gates = w_in[:, 2624:2672]
    x1, x2 = kr[:, :32], kr[:, 32:]
    flat = lambda t: t.reshape(t.shape[0], NSA_G * NSA_D)
    w_tok = jnp.concatenate([cq, ckv, x1, x2, x2, x1, kv_cmp, flat(kv_slc[:, :, 0]), flat(kv_win[:, :, 0])], axis=1)
    zeros = jnp.zeros((w_in.shape[0], FM_ROWS - ROW_GATE - gates.shape[1]), w_in.dtype)
    w_fea = jnp.concatenate([q, flat(kv_slc[:, :, 1]), flat(kv_win[:, :, 1]), gates, zeros], axis=1)
    return w_tok.astype(BF16), w_fea.T.astype(BF16)


def _regroup_w_uq(w_uq):
    w = w_uq.reshape(MLA_Q_RANK, MLA_HEADS, MLA_NOPE + MLA_ROPE)
    nope = w[:, :, :MLA_NOPE]
    x1 = w[:, :, MLA_NOPE:MLA_NOPE + 32]
    x2 = w[:, :, MLA_NOPE + 32:]
    return jnp.concatenate([nope, x1, x2, x2, x1], axis=-1).reshape(MLA_Q_RANK, MLA_HEADS * MLA_QK_PAD).astype(BF16)


def kernel(x, positions, attn_norm_w, w_in, mla_q_norm_w, mla_w_uq, mla_kv_norm_w, mla_w_ukv, cmp_pos_k,
           cmp_pos_v, cmp_w1_k, cmp_w2_k, cmp_w1_v, cmp_w2_v, mla_out_norm_w, nsa_out_norm_w, w_o, mlp_norm_w,
           w_up, w_down, final_norm_w):
    B, T, _ = x.shape
    depth = w_in.shape[0]
    inv = 1.0 / (ROPE_THETA ** (jnp.arange(0, MLA_ROPE, 2, dtype=F32) / MLA_ROPE))
    inv128 = jnp.tile(inv, 4).reshape(1, LANES)
    start = 2.0 ** (-8.0 / NSA_HEADS)
    slopes = start ** jnp.arange(1, NSA_HEADS + 1, dtype=F32)
    arows, srow = _alibi_rows(slopes)
    nsel_pad = -(-(T // SEL_LEN) // LANES) * LANES

    outs = []
    for b in range(B):
        h = x[b]
        pos2 = positions[b].reshape(T, 1)
        for l in range(depth):
            w_tok, w_feaT = _regroup_w_in(w_in[l])
            pm, r4, ka_s, ka_w, qT_nsa, va_s, va_w, glT = _in_proj(h, attn_norm_w[l], w_tok, w_feaT, nsel_pad)
            wukv = mla_w_ukv[l].reshape(MLA_KV_RANK, MLA_HEADS, MLA_NOPE + MLA_V)
            wuk = wukv[:, :, :MLA_NOPE].reshape(MLA_KV_RANK, MLA_HEADS * MLA_NOPE).astype(BF16)
            wuvT = wukv[:, :, MLA_NOPE:].reshape(MLA_KV_RANK, MLA_HEADS * MLA_V).T.astype(BF16)
            qT, k, vT = _mla_prep(pm, pos2, mla_q_norm_w[l], mla_kv_norm_w[l], _regroup_w_uq(mla_w_uq[l]).T,
                                  wuk, wuvT, inv128)
            y_mla = _mla_attn(qT, k, vT)

            half = CMP_STRIDE * NSA_D
            pos_kv = jnp.stack([cmp_pos_k[l].reshape(2, half), cmp_pos_v[l].reshape(2, half)])
            kc, vcT = _compress(r4.reshape(2 * NSA_G, T // CMP_STRIDE, half), pos_kv,
                                jnp.stack([cmp_w1_k[l], cmp_w1_v[l]]), jnp.stack([cmp_w2_k[l], cmp_w2_v[l]]))
            o_cT, selT = _cmp_attn(qT_nsa, arows, kc, vcT, nsel_pad)
            o_sT = _sel_attn(qT_nsa, selT, arows, srow, ka_s, va_s)
            o_wT = _win_attn(qT_nsa, arows, srow, ka_w, va_w)

            h = _out_proj(h, y_mla, o_cT, o_sT, o_wT, glT, mla_out_norm_w[l], nsa_out_norm_w[l],
                          w_o[l].astype(BF16))
            h = _mlp(h, mlp_norm_w[l], w_up[l].astype(BF16), w_down[l].astype(BF16), final_norm_w,
                     final=(l == depth - 1))
        outs.append(h)
    return jnp.stack(outs)
```

```python
import functools

import numpy as np
import jax
import jax.numpy as jnp
from jax import lax
from jax.experimental import pallas as pl
from jax.experimental.pallas import tpu as pltpu

F32 = jnp.float32
BF16 = jnp.bfloat16

EPS = 1e-6
NEG = -1e30
REMOVED = -3e38

D_MODEL = 2048
MLA_HEADS = 8
MLA_Q_RANK = 512
MLA_KV_RANK = 256
MLA_NOPE = 128
MLA_ROPE = 64
MLA_V = 128
ROPE_THETA = 10000.0
MLA_QK_PAD = 256
MLA_VROWS = 144

NSA_HEADS = 16
NSA_G = 2
NSA_HPG = 8
NSA_D = 64
CMP_LEN = 32
CMP_STRIDE = 16
CMP_HIDDEN = 256
SEL_LEN = 64
SEL_SHIFT = 6
SEL_TOPK = 16
WINDOW = 512
FORCE_BONUS = 1e4
MLP_HIDDEN = 4 * D_MODEL

LOG2E = 1.4426950408889634
LANES = 128
SUBLANES = 8
BF16_ROWS = 16
NSA_TQ = 128
SEL_TK = 512
MLA_TQ = 512
MLA_TK = 512
VMEM_LIMIT = 56 * 1024 * 1024

SEC_CQ = 0
SEC_CKV = 512
SEC_ROPE = 768
SEC_CMP = 896
SEC_KS = 1152
SEC_KW = 1280
TM_COLS = 1408
MLA_COLS = SEC_CMP
ROW_Q = 0
ROW_VS = 1024
ROW_VW = 1152
ROW_GATE = 1280
FM_ROWS = 1408


def _cparams(sem):
    return pltpu.CompilerParams(dimension_semantics=sem, vmem_limit_bytes=VMEM_LIMIT)


def _rms(x, w):
    return x * lax.rsqrt(jnp.mean(x * x, axis=-1, keepdims=True) + EPS) * w


ALIBI_PARTS = 4
ALIBI_HI = 16
NEG_BF16 = -1e30
SEL_VROWS = 80
NT_DIMS = (((1,), (1,)), ((), ()))
TN_DIMS = (((0,), (0,)), ((), ()))


def _digit_cols(offset, scale):
    lane = lax.broadcasted_iota(jnp.int32, offset.shape, 1)
    hi = jnp.right_shift(offset, 4) * scale
    lo = jnp.bitwise_and(offset, ALIBI_HI - 1) * scale
    return jnp.where(lane < ALIBI_PARTS, hi, jnp.where(lane < 2 * ALIBI_PARTS, lo, 0)).astype(F32)


def _in_proj_kernel(x_ref, nw_ref, w_ref, wT_ref, pm_ref, r_ref, kas_ref, kaw_ref, qT_ref, vas_ref, vaw_ref,
                    glT_ref, *, nsel_pad):
    c = pl.program_id(0)
    tm = x_ref.shape[0]
    n = _rms(x_ref[...], nw_ref[...]).astype(BF16)
    tok = jnp.dot(n, w_ref[...], preferred_element_type=F32)
    fea = lax.dot_general(wT_ref[...], n, NT_DIMS, preferred_element_type=F32)

    pm_ref[...] = tok[:, :MLA_COLS]
    for a in range(2 * NSA_G):
        r_ref[a] = tok[:, SEC_CMP + a * NSA_D:SEC_CMP + (a + 1) * NSA_D]
    r = lax.broadcasted_iota(jnp.int32, (tm, NSA_D), 0)
    blk = lax.broadcasted_iota(jnp.int32, (tm, nsel_pad), 1)
    kblk = jnp.right_shift(c * tm + lax.broadcasted_iota(jnp.int32, (tm, nsel_pad), 0), SEL_SHIFT)
    onehot = jnp.where(blk == kblk, 1.0, 0.0)
    sel_digits = _digit_cols(r, 1)
    win_digits = _digit_cols(jnp.bitwise_and(r, NSA_TQ - 1), 1)
    ones_row = jnp.where(lax.broadcasted_iota(jnp.int32, (SEL_VROWS - NSA_D, tm), 0) == 0, 1.0, 0.0)
    for g in range(NSA_G):
        ks = tok[:, SEC_KS + g * NSA_D:SEC_KS + (g + 1) * NSA_D]
        kw = tok[:, SEC_KW + g * NSA_D:SEC_KW + (g + 1) * NSA_D]
        kas_ref[g] = jnp.concatenate([ks, sel_digits, onehot], axis=-1).astype(BF16)
        kaw_ref[g] = jnp.concatenate([kw, win_digits], axis=-1).astype(BF16)
        vas_ref[g, 0] = jnp.concatenate([fea[ROW_VS + g * NSA_D:ROW_VS + (g + 1) * NSA_D, :], ones_row],
                                        axis=0).astype(BF16)
        vaw_ref[g] = jnp.concatenate([fea[ROW_VW + g * NSA_D:ROW_VW + (g + 1) * NSA_D, :], ones_row],
                                     axis=0).astype(BF16)
    qT_ref[...] = fea[ROW_Q:ROW_VS, :]
    glT_ref[...] = fea[ROW_GATE:, :]


def _in_proj(x2, attn_norm_w, w_tok, w_feaT, nsel_pad, tm=SEL_TK):
    T = x2.shape[0]
    tm = min(tm, T)
    kw = 2 * NSA_D + nsel_pad
    fixed = lambda i: (0, 0)
    return pl.pallas_call(
        functools.partial(_in_proj_kernel, nsel_pad=nsel_pad),
        out_shape=(
            jax.ShapeDtypeStruct((T, MLA_COLS), F32),
            jax.ShapeDtypeStruct((2 * NSA_G, T, NSA_D), F32),
            jax.ShapeDtypeStruct((NSA_G, T, kw), BF16),
            jax.ShapeDtypeStruct((NSA_G, T, 2 * NSA_D), BF16),
            jax.ShapeDtypeStruct((NSA_HEADS * NSA_D, T), F32),
            jax.ShapeDtypeStruct((NSA_G, T // tm, SEL_VROWS, tm), BF16),
            jax.ShapeDtypeStruct((NSA_G, SEL_VROWS, T), BF16),
            jax.ShapeDtypeStruct((LANES, T), F32),
        ),
        grid=(T // tm,),
        in_specs=[
            pl.BlockSpec((tm, D_MODEL), lambda i: (i, 0)),
            pl.BlockSpec((1, D_MODEL), fixed),
            pl.BlockSpec((D_MODEL, TM_COLS), fixed),
            pl.BlockSpec((FM_ROWS, D_MODEL), fixed),
        ],
        out_specs=(
            pl.BlockSpec((tm, MLA_COLS), lambda i: (i, 0)),
            pl.BlockSpec((2 * NSA_G, tm, NSA_D), lambda i: (0, i, 0)),
            pl.BlockSpec((NSA_G, tm, kw), lambda i: (0, i, 0)),
            pl.BlockSpec((NSA_G, tm, 2 * NSA_D), lambda i: (0, i, 0)),
            pl.BlockSpec((NSA_HEADS * NSA_D, tm), lambda i: (0, i)),
            pl.BlockSpec((NSA_G, 1, SEL_VROWS, tm), lambda i: (0, i, 0, 0)),
            pl.BlockSpec((NSA_G, SEL_VROWS, tm), lambda i: (0, 0, i)),
            pl.BlockSpec((LANES, tm), lambda i: (0, i)),
        ),
        compiler_params=_cparams(("parallel",)),
        name="in_proj",
    )(x2, attn_norm_w.reshape(1, D_MODEL), w_tok, w_feaT)


def _mla_prep_kernel(cq_ref, ckv_ref, kr_ref, pos_ref, qnw_ref, kvnw_ref, wuqT_ref, wuk_ref, wuvT_ref, inv_ref,
                     qT_ref, k_ref, vT_ref):
    scale = (MLA_NOPE + MLA_ROPE) ** -0.5 * LOG2E
    qn = _rms(cq_ref[...], qnw_ref[...]).astype(BF16)
    qT = lax.dot_general(wuqT_ref[...], qn, NT_DIMS, preferred_element_type=F32)
    kvn = _rms(ckv_ref[...], kvnw_ref[...]).astype(BF16)
    kn = jnp.dot(kvn, wuk_ref[...], preferred_element_type=F32)
    vT = lax.dot_general(wuvT_ref[...], kvn, NT_DIMS, preferred_element_type=F32)

    ang = pos_ref[...].astype(F32) * inv_ref[...]
    c = jnp.cos(ang)
    s = jnp.sin(ang)
    lane = lax.broadcasted_iota(jnp.int32, ang.shape, 1)
    coef = jnp.where(lane < 64, c, jnp.where(lane < 96, -s, s))
    coefT = coef.T

    ones_row = jnp.where(lax.broadcasted_iota(jnp.int32, (MLA_VROWS - MLA_V, coefT.shape[1]), 0) == 0, 1.0, 0.0)
    t = kr_ref[...] * coef
    kpe = jnp.where(lane < 64, t + pltpu.roll(t, 64, 1), 0.0)
    for h in range(MLA_HEADS):
        base = h * MLA_QK_PAD
        tT = qT[base + 128:base + 256, :] * coefT
        ropeT = tT + pltpu.roll(tT, 64, 0)
        qT_ref[h] = (jnp.concatenate([qT[base:base + 128, :], ropeT], axis=0) * scale).astype(BF16)
        k_ref[h] = jnp.concatenate([kn[:, h * 128:(h + 1) * 128], kpe], axis=-1).astype(BF16)
        vT_ref[h] = jnp.concatenate([vT[h * 128:(h + 1) * 128, :], ones_row], axis=0).astype(BF16)


def _mla_prep(pm, pos2, qnw, kvnw, wuqT, wuk, wuvT, inv128, tm=256):
    T = pm.shape[0]
    H = MLA_HEADS
    fixed = lambda i: (0, 0)
    return pl.pallas_call(
        _mla_prep_kernel,
        out_shape=(
            jax.ShapeDtypeStruct((H, MLA_QK_PAD, T), BF16),
            jax.ShapeDtypeStruct((H, T, MLA_QK_PAD), BF16),
            jax.ShapeDtypeStruct((H, MLA_VROWS, T), BF16),
        ),
        grid=(T // tm,),
        in_specs=[
            pl.BlockSpec((tm, MLA_Q_RANK), lambda i: (i, SEC_CQ // MLA_Q_RANK)),
            pl.BlockSpec((tm, MLA_KV_RANK), lambda i: (i, SEC_CKV // MLA_KV_RANK)),
            pl.BlockSpec((tm, LANES), lambda i: (i, SEC_ROPE // LANES)),
            pl.BlockSpec((tm, 1), lambda i: (i, 0)),
            pl.BlockSpec((1, MLA_Q_RANK), fixed),
            pl.BlockSpec((1, MLA_KV_RANK), fixed),
            pl.BlockSpec((H * MLA_QK_PAD, MLA_Q_RANK), fixed),
            pl.BlockSpec((MLA_KV_RANK, H * MLA_NOPE), fixed),
            pl.BlockSpec((H * MLA_V, MLA_KV_RANK), fixed),
            pl.BlockSpec((1, LANES), fixed),
        ],
        out_specs=(
            pl.BlockSpec((H, MLA_QK_PAD, tm), lambda i: (0, 0, i)),
            pl.BlockSpec((H, tm, MLA_QK_PAD), lambda i: (0, i, 0)),
            pl.BlockSpec((H, MLA_VROWS, tm), lambda i: (0, 0, i)),
        ),
        compiler_params=_cparams(("parallel",)),
        name="mla_prep",
    )(pm, pm, pm, pos2, qnw.reshape(1, -1), kvnw.reshape(1, -1), wuqT, wuk, wuvT, inv128)


def _mla_attn_kernel(qi_ref, ki_ref, qT_ref, k_ref, vT_ref, o_ref, m_scr, acc_scr, s0_scr, s1_scr, p0_scr, p1_scr):
    p_id = pl.program_id(0)
    qi = qi_ref[p_id]
    ki = ki_ref[p_id]
    tk, tq = s0_scr.shape

    @pl.when(ki == 0)
    def _():
        m_scr[...] = jnp.full(m_scr.shape, NEG, F32)
        acc_scr[...] = jnp.zeros(acc_scr.shape, F32)

    def step(diag):
        for h in range(MLA_HEADS):
            s_ref, p_ref = ((s0_scr, p0_scr), (s1_scr, p1_scr))[h % 2]
            s = jnp.dot(k_ref[h], qT_ref[h], preferred_element_type=F32)
            if diag:
                key = lax.broadcasted_iota(jnp.int32, s.shape, 0)
                qry = lax.broadcasted_iota(jnp.int32, s.shape, 1)
                s = jnp.where(key <= qry, s, NEG)
            s_ref[...] = s
            top = jnp.max(s.reshape(tk // SUBLANES, SUBLANES, tq), axis=0)
            m_prev = m_scr[h]
            m_new = jnp.maximum(m_prev, jnp.max(top, axis=0, keepdims=True))
            for b in range(tk // BF16_ROWS):
                rows = slice(b * BF16_ROWS, (b + 1) * BF16_ROWS)
                p_ref[rows, :] = jnp.exp2(s_ref[rows, :] - m_new).astype(BF16)
            acc = (jnp.exp2(m_prev - m_new) * acc_scr[h]
                   + jnp.dot(vT_ref[h], p_ref[...], preferred_element_type=F32))
            if diag:
                o_ref[h * MLA_V:(h + 1) * MLA_V, :] = acc[:MLA_V, :] * (1.0 / acc[MLA_V:MLA_V + 1, :])
            else:
                m_scr[h] = m_new
                acc_scr[h] = acc

    @pl.when(ki < qi)
    def _():
        step(False)

    @pl.when(ki == qi)
    def _():
        step(True)


def _tri_pairs(nq, per_q):
    qi, ki = [], []
    for i in range(nq):
        for c in range(per_q(i) + 1):
            qi.append(i)
            ki.append(c)
    return jnp.asarray(qi, jnp.int32), jnp.asarray(ki, jnp.int32)


def _mla_attn(qT, k, vT):
    H, T, _ = k.shape
    tq = min(MLA_TQ, T)
    nq = T // tq
    qi, ki = _tri_pairs(nq, lambda i: i)
    grid_spec = pltpu.PrefetchScalarGridSpec(
        num_scalar_prefetch=2,
        grid=(int(qi.shape[0]),),
        in_specs=[
            pl.BlockSpec((H, MLA_QK_PAD, tq), lambda p, qi, ki: (0, 0, qi[p])),
            pl.BlockSpec((H, tq, MLA_QK_PAD), lambda p, qi, ki: (0, ki[p], 0)),
            pl.BlockSpec((H, MLA_VROWS, tq), lambda p, qi, ki: (0, 0, ki[p])),
        ],
        out_specs=pl.BlockSpec((H * MLA_V, tq), lambda p, qi, ki: (0, qi[p])),
        scratch_shapes=[
            pltpu.VMEM((H, 1, tq), F32),
            pltpu.VMEM((H, MLA_VROWS, tq), F32),
            pltpu.VMEM((tq, tq), F32),
            pltpu.VMEM((tq, tq), F32),
            pltpu.VMEM((tq, tq), BF16),
            pltpu.VMEM((tq, tq), BF16),
        ],
    )
    return pl.pallas_call(
        _mla_attn_kernel,
        out_shape=jax.ShapeDtypeStruct((H * MLA_V, T), F32),
        grid_spec=grid_spec,
        compiler_params=_cparams(("arbitrary",)),
        name="mla_attn",
    )(qi, ki, qT, k, vT)


def _gelu_tanh(x):
    return 0.5 * x * (1.0 + jnp.tanh(np.sqrt(2.0 / np.pi).astype(np.float32) * (x + 0.044715 * (x * x * x))))


def _compress_kernel(r_ref, pos_ref, w1_ref, w2_ref, kc_ref, vcT_ref):
    r = r_ref[0]
    nr = r.shape[0]
    half = CMP_STRIDE * NSA_D
    hp = lax.Precision.HIGHEST
    a = jnp.dot(r + pos_ref[0, 0:1, :], w1_ref[0, :half, :], precision=hp, preferred_element_type=F32)
    b = jnp.dot(r + pos_ref[0, 1:2, :], w1_ref[0, half:, :], precision=hp, preferred_element_type=F32)
    hid = _gelu_tanh(a + pltpu.roll(b, nr - 1, 0))
    out = jnp.dot(hid, w2_ref[0], precision=hp, preferred_element_type=F32)
    row = lax.broadcasted_iota(jnp.int32, out.shape, 0)
    out = jnp.where(row < nr - 1, out, 0.0)
    is_key = pl.program_id(0) % 2 == 0

    @pl.when(is_key)
    def _():
        kc_ref[0] = jnp.concatenate([out, _digit_cols(row, CMP_STRIDE)], axis=-1).astype(BF16)

    @pl.when(jnp.logical_not(is_key))
    def _():
        vcT_ref[0] = jnp.concatenate([out, jnp.zeros_like(out)], axis=-1).T[0:NSA_D, :].astype(BF16)


def _compress(r4, pos_kv, w1_kv, w2_kv):
    _, nr, width = r4.shape
    return pl.pallas_call(
        _compress_kernel,
        out_shape=(
            jax.ShapeDtypeStruct((NSA_G, nr, 2 * NSA_D), BF16),
            jax.ShapeDtypeStruct((NSA_G, NSA_D, nr), BF16),
        ),
        grid=(2 * NSA_G,),
        in_specs=[
            pl.BlockSpec((1, nr, width), lambda a: (a, 0, 0)),
            pl.BlockSpec((1, 2, width), lambda a: (a % 2, 0, 0)),
            pl.BlockSpec((1, 2 * width, CMP_HIDDEN), lambda a: (a % 2, 0, 0)),
            pl.BlockSpec((1, CMP_HIDDEN, NSA_D), lambda a: (a % 2, 0, 0)),
        ],
        out_specs=(
            pl.BlockSpec((1, nr, 2 * NSA_D), lambda a: (a // 2, 0, 0)),
            pl.BlockSpec((1, NSA_D, nr), lambda a: (a // 2, 0, 0)),
        ),
        compiler_params=_cparams(("arbitrary",)),
        name="nsa_compress",
    )(r4, pos_kv, w1_kv, w2_kv)


def _alibi_rows():
    start = np.float32(2.0 ** (-8.0 / NSA_HEADS))
    slopes = (start ** np.arange(1, NSA_HEADS + 1, dtype=np.float32)).astype(np.float32)
    slopes = (slopes.astype(np.float64) * LOG2E).astype(np.float32)
    parts, rest = [], slopes
    for _ in range(ALIBI_PARTS):
        piece = (rest.view(np.uint32) & np.uint32(0xFFFF0000)).view(np.float32)
        parts.append(piece)
        rest = rest - piece
    assert not rest.any()
    pieces = np.stack(parts)
    rows = np.concatenate([pieces * ALIBI_HI, pieces, np.zeros((NSA_D - 2 * ALIBI_PARTS, NSA_HEADS), np.float32)])
    rows = np.repeat(rows.reshape(NSA_D, NSA_G, NSA_HPG), NSA_TQ, axis=2).transpose(1, 0, 2)
    srow = np.repeat(slopes.reshape(NSA_G, 1, NSA_HPG), NSA_TQ, axis=2)
    return jnp.asarray(rows, BF16), jnp.asarray(srow, F32)


def _nsa_queries(qT_ref, arow_ref):
    q = qT_ref[...] * (NSA_D ** -0.5 * LOG2E)
    top = jnp.concatenate([q[h * NSA_D:(h + 1) * NSA_D, :] for h in range(NSA_HPG)], axis=1).astype(BF16)
    return jnp.concatenate([top, arow_ref[0]], axis=0)


def _heads_to_rows(o):
    tq = o.shape[1] // NSA_HPG
    return jnp.concatenate([o[:, h * tq:(h + 1) * tq] for h in range(NSA_HPG)], axis=0)


def _token_pos(i, n):
    return i * NSA_TQ + jnp.bitwise_and(lax.broadcasted_iota(jnp.int32, (1, n), 1), NSA_TQ - 1)


def _cmp_attn_kernel(qT_ref, arow_ref, kc_ref, vcT_ref, o_ref, selT_ref, *, nsel_pad):
    i = pl.program_id(1)
    tq = NSA_TQ
    n = NSA_HPG * tq
    nr = kc_ref.shape[1]
    s = jnp.dot(kc_ref[0], _nsa_queries(qT_ref, arow_ref), preferred_element_type=F32)
    tpos = _token_pos(i, n)
    last_done = jnp.right_shift(tpos - (CMP_LEN - 1), 4)
    valid = lax.broadcasted_iota(jnp.int32, (nr, n), 0) <= last_done
    s = jnp.where(valid, s, NEG)
    e = jnp.where(valid, jnp.exp2(s - jnp.max(s, axis=0, keepdims=True)), 0.0)
    l = jnp.sum(e, axis=0, keepdims=True)
    inv = 1.0 / jnp.where(l > 0.0, l, 1.0)
    acc = jnp.dot(vcT_ref[0], e.astype(BF16), preferred_element_type=F32)
    o_ref[...] = _heads_to_rows(acc * inv)

    pn = e * inv
    psum = pn[:, 0:tq]
    for h in range(1, NSA_HPG):
        psum = psum + pn[:, h * tq:(h + 1) * tq]
    j2 = lax.broadcasted_iota(jnp.int32, (nsel_pad, nr), 0)
    n2 = lax.broadcasted_iota(jnp.int32, (nsel_pad, nr), 1)
    ov = jnp.logical_and(n2 * CMP_STRIDE <= j2 * SEL_LEN + (SEL_LEN - 1),
                         n2 * CMP_STRIDE + (CMP_LEN - 1) >= j2 * SEL_LEN)
    ov = jnp.logical_and(ov, n2 < nr - 1)
    imp = jnp.dot(jnp.where(ov, 1.0, 0.0).astype(F32), psum, precision=lax.Precision.HIGHEST,
                  preferred_element_type=F32)

    blk = lax.broadcasted_iota(jnp.int32, (nsel_pad, tq), 0)
    cur = jnp.right_shift(i * tq + lax.broadcasted_iota(jnp.int32, (nsel_pad, tq), 1), SEL_SHIFT)
    forced = jnp.logical_or(blk == 0, jnp.logical_or(blk == cur, blk == cur - 1))
    score = jnp.where(blk <= cur, imp + jnp.where(forced, FORCE_BONUS, 0.0), NEG)
    blk_f = blk.astype(F32)

    def pick(_, carry):
        sc, sel = carry
        mx = jnp.max(sc, axis=0, keepdims=True)
        first = jnp.min(jnp.where(sc == mx, blk_f, float(nsel_pad)), axis=0, keepdims=True)
        hit = blk_f == first
        sel = jnp.where(jnp.logical_and(hit, mx > NEG * 0.5), 1.0, sel)
        return jnp.where(hit, REMOVED, sc), sel

    _, sel = lax.fori_loop(0, SEL_TOPK, pick, (score, jnp.zeros((nsel_pad, tq), F32)))
    selT_ref[0] = sel


def _cmp_attn(fm, arows, kc, vcT, nsel_pad):
    T = fm.shape[1]
    nr = kc.shape[1]
    tq = NSA_TQ
    n = NSA_HPG * tq
    return pl.pallas_call(
        functools.partial(_cmp_attn_kernel, nsel_pad=nsel_pad),
        out_shape=(
            jax.ShapeDtypeStruct((NSA_HEADS * NSA_D, T), F32),
            jax.ShapeDtypeStruct((NSA_G, nsel_pad, T), F32),
        ),
        grid=(NSA_G, T // tq),
        in_specs=[
            pl.BlockSpec((NSA_HPG * NSA_D, tq), lambda g, i: (g, i)),
            pl.BlockSpec((1, NSA_D, n), lambda g, i: (g, 0, 0)),
            pl.BlockSpec((1, nr, 2 * NSA_D), lambda g, i: (g, 0, 0)),
            pl.BlockSpec((1, NSA_D, nr), lambda g, i: (g, 0, 0)),
        ],
        out_specs=(
            pl.BlockSpec((NSA_HPG * NSA_D, tq), lambda g, i: (g, i)),
            pl.BlockSpec((1, nsel_pad, tq), lambda g, i: (g, 0, i)),
        ),
        compiler_params=_cparams(("parallel", "parallel")),
        name="nsa_cmp_attn",
    )(fm, arows, kc, vcT)


def _sel_attn_kernel(qT_ref, selT_ref, arow_ref, srow_ref, ka_ref, va_ref, o_ref,
                     qa_scr, s0_scr, s1_scr, top0_scr, top1_scr, p0_scr, p1_scr, al0_scr, al1_scr, m_scr, acc_scr,
                     *, tk, n_chunks):
    i = pl.program_id(1)
    tq = NSA_TQ
    n = NSA_HPG * tq
    nsel_pad = selT_ref.shape[1]
    qa_scr[0:2 * NSA_D, :] = _nsa_queries(qT_ref, arow_ref)
    blk = lax.broadcasted_iota(jnp.int32, (nsel_pad, tq), 0)
    own = jnp.right_shift(blk, 1) == i
    chosen = jnp.logical_and(selT_ref[0] > 0.5, jnp.logical_not(own))
    qa_scr[2 * NSA_D:, :] = jnp.concatenate([jnp.where(chosen, 0.0, NEG_BF16).astype(BF16)] * NSA_HPG, axis=1)

    tpos = _token_pos(i, n)
    slope_row = srow_ref[0]

    def rterm(c):
        return slope_row * (c * tk - tpos).astype(F32)

    c_own = (i * tq) // tk
    own_start = pl.multiple_of(i * tq, tq)
    s = jnp.dot(ka_ref[0, pl.ds(own_start, tq), 0:2 * NSA_D], qa_scr[0:2 * NSA_D, :], preferred_element_type=F32)
    kpos = i * tq + lax.broadcasted_iota(jnp.int32, s.shape, 0)
    s = jnp.where(kpos <= tpos, s, NEG)
    r_own = rterm(c_own)
    m0 = jnp.max(s, axis=0, keepdims=True) + r_own
    p = jnp.exp2(s - (m0 - r_own)).astype(BF16)
    v_own = va_ref[0, c_own, :, pl.ds(pl.multiple_of(i * tq - c_own * tk, tq), tq)]
    acc_scr[...] = jnp.dot(v_own, p, preferred_element_type=F32)
    m_scr[...] = m0

    s_slots = (s0_scr, s1_scr)
    p_slots = (p0_scr, p1_scr)
    al_slots = (al0_scr, al1_scr)

    top_slots = (top0_scr, top1_scr)

    def scores(j, slot):
        start = pl.multiple_of(j * tk, tk)
        s = jnp.dot(ka_ref[0, pl.ds(start, tk), :], qa_scr[...], preferred_element_type=F32)
        s_slots[slot][...] = s
        top_slots[slot][...] = jnp.max(s.reshape(tk // SUBLANES, SUBLANES, n), axis=0)

    def softmax(j, slot):
        r = rterm(j)
        m_prev = m_scr[...]
        s_ref, p_ref = s_slots[slot], p_slots[slot]
        m_new = jnp.maximum(m_prev, jnp.max(top_slots[slot][...], axis=0, keepdims=True) + r)
        shift = m_new - r
        for b in range(tk // BF16_ROWS):
            rows = slice(b * BF16_ROWS, (b + 1) * BF16_ROWS)
            p_ref[rows, :] = jnp.exp2(s_ref[rows, :] - shift).astype(BF16)
        al_slots[slot][...] = jnp.exp2(m_prev - m_new)
        m_scr[...] = m_new

    def values(j, slot):
        acc_scr[...] = (al_slots[slot][...] * acc_scr[...]
                        + jnp.dot(va_ref[0, j], p_slots[slot][...], preferred_element_type=F32))

    n_pairs = jnp.minimum(((i * tq + tq - 1) // tk + 2) // 2, n_chunks // 2)
    scores(0, 0)
    scores(1, 1)
    softmax(0, 0)

    def body(jj, carry):
        j = 2 * jj
        scores(j, 0)
        softmax(j - 1, 1)
        values(j - 2, 0)
        scores(j + 1, 1)
        softmax(j, 0)
        values(j - 1, 1)
        return carry

    lax.fori_loop(1, n_pairs, body, 0)
    softmax(2 * n_pairs - 1, 1)
    values(2 * n_pairs - 2, 0)
    values(2 * n_pairs - 1, 1)
    acc = acc_scr[...]
    o_ref[...] = _heads_to_rows(acc[0:NSA_D, :] * (1.0 / acc[NSA_D:NSA_D + 1, :]))


def _sel_attn(fm, selT, arows, srow, ka, va):
    _, T, kw = ka.shape
    tk = va.shape[-1]
    tq = NSA_TQ
    nsel_pad = selT.shape[1]
    n = NSA_HPG * tq
    return pl.pallas_call(
        functools.partial(_sel_attn_kernel, tk=tk, n_chunks=T // tk),
        out_shape=jax.ShapeDtypeStruct((NSA_HEADS * NSA_D, T), F32),
        grid=(NSA_G, T // tq),
        in_specs=[
            pl.BlockSpec((NSA_HPG * NSA_D, tq), lambda g, i: (g, i)),
            pl.BlockSpec((1, nsel_pad, tq), lambda g, i: (g, 0, i)),
            pl.BlockSpec((1, NSA_D, n), lambda g, i: (g, 0, 0)),
            pl.BlockSpec((1, 1, n), lambda g, i: (g, 0, 0)),
            pl.BlockSpec((1, T, kw), lambda g, i: (g, 0, 0)),
            pl.BlockSpec((1, T // tk, SEL_VROWS, tk), lambda g, i: (g, 0, 0, 0)),
        ],
        out_specs=pl.BlockSpec((NSA_HPG * NSA_D, tq), lambda g, i: (g, i)),
        scratch_shapes=[
            pltpu.VMEM((kw, n), BF16),
            pltpu.VMEM((tk, n), F32),
            pltpu.VMEM((tk, n), F32),
            pltpu.VMEM((SUBLANES, n), F32),
            pltpu.VMEM((SUBLANES, n), F32),
            pltpu.VMEM((tk, n), BF16),
            pltpu.VMEM((tk, n), BF16),
            pltpu.VMEM((1, n), F32),
            pltpu.VMEM((1, n), F32),
            pltpu.VMEM((1, n), F32),
            pltpu.VMEM((SEL_VROWS, n), F32),
        ],
        compiler_params=_cparams(("parallel", "arbitrary")),
        name="nsa_sel_attn",
    )(fm, selT, arows, srow, ka, va)


WIN_KEYS = WINDOW + NSA_TQ


def _win_attn_kernel(qT_ref, arow_ref, srow_ref, ka_ref, va_ref, o_ref):
    i = pl.program_id(1)
    tq = NSA_TQ
    n = NSA_HPG * tq
    start = pl.multiple_of(jnp.maximum(i * tq - WINDOW, 0), tq)
    s = jnp.dot(ka_ref[0, pl.ds(start, WIN_KEYS), :], _nsa_queries(qT_ref, arow_ref), preferred_element_type=F32)
    tpos = _token_pos(i, n)
    kpos = start + lax.broadcasted_iota(jnp.int32, s.shape, 0)
    allowed = jnp.logical_and(kpos <= tpos, kpos > tpos - WINDOW)
    s = jnp.where(allowed, s, NEG)
    slope_row = srow_ref[0]
    tiles = [s[j * tq:(j + 1) * tq, :] for j in range(WIN_KEYS // tq)]
    rterms = [slope_row * (start + j * tq - tpos).astype(F32) for j in range(WIN_KEYS // tq)]
    m = jnp.max(tiles[0], axis=0, keepdims=True) + rterms[0]
    for sj, rj in zip(tiles[1:], rterms[1:]):
        m = jnp.maximum(m, jnp.max(sj, axis=0, keepdims=True) + rj)
    p = jnp.concatenate([jnp.exp2(sj - (m - rj)) for sj, rj in zip(tiles, rterms)], axis=0).astype(BF16)
    acc = jnp.dot(va_ref[0, :, pl.ds(start, WIN_KEYS)], p, preferred_element_type=F32)
    o_ref[...] = _heads_to_rows(acc[0:NSA_D, :] * (1.0 / acc[NSA_D:NSA_D + 1, :]))


def _win_attn(fm, arows, srow, ka, va):
    _, T, kw = ka.shape
    assert T >= WIN_KEYS
    tq = NSA_TQ
    n = NSA_HPG * tq
    return pl.pallas_call(
        _win_attn_kernel,
        out_shape=jax.ShapeDtypeStruct((NSA_HEADS * NSA_D, T), F32),
        grid=(NSA_G, T // tq),
        in_specs=[
            pl.BlockSpec((NSA_HPG * NSA_D, tq), lambda g, i: (g, i)),
            pl.BlockSpec((1, NSA_D, n), lambda g, i: (g, 0, 0)),
            pl.BlockSpec((1, 1, n), lambda g, i: (g, 0, 0)),
            pl.BlockSpec((1, T, kw), lambda g, i: (g, 0, 0)),
            pl.BlockSpec((1, SEL_VROWS, T), lambda g, i: (g, 0, 0)),
        ],
        out_specs=pl.BlockSpec((NSA_HPG * NSA_D, tq), lambda g, i: (g, i)),
        compiler_params=_cparams(("parallel", "parallel")),
        name="nsa_win_attn",
    )(fm, arows, srow, ka, va)


def _out_proj_kernel(x_ref, ymT_ref, ocT_ref, osT_ref, owT_ref, glT_ref, mnw_ref, nnw_ref, wo_ref, o_ref):
    gates = 1.0 / (1.0 + jnp.exp(-glT_ref[...]))
    heads = []
    for h in range(NSA_HEADS):
        rows = slice(h * NSA_D, (h + 1) * NSA_D)
        heads.append(gates[3 * h:3 * h + 1, :] * ocT_ref[rows, :]
                     + gates[3 * h + 1:3 * h + 2, :] * osT_ref[rows, :]
                     + gates[3 * h + 2:3 * h + 3, :] * owT_ref[rows, :])
    ynT = jnp.concatenate(heads, axis=0)

    def rms_cols(y, w_col):
        return (y * lax.rsqrt(jnp.mean(y * y, axis=0, keepdims=True) + EPS) * w_col).astype(BF16)

    half = MLA_HEADS * MLA_V
    o_ref[...] = (x_ref[...]
                  + lax.dot_general(rms_cols(ymT_ref[...], mnw_ref[...]), wo_ref[:half, :], TN_DIMS,
                                    preferred_element_type=F32)
                  + lax.dot_general(rms_cols(ynT, nnw_ref[...]), wo_ref[half:, :], TN_DIMS,
                                    preferred_element_type=F32))


def _out_proj(x2, y_mlaT, o_cT, o_sT, o_wT, glT, mnw, nnw, wo, tm=512):
    T = x2.shape[0]
    half = MLA_HEADS * MLA_V
    row = lambda i: (i, 0)
    col = lambda i: (0, i)
    fixed2 = lambda i: (0, 0)
    return pl.pallas_call(
        _out_proj_kernel,
        out_shape=jax.ShapeDtypeStruct((T, D_MODEL), F32),
        grid=(T // tm,),
        in_specs=[
            pl.BlockSpec((tm, D_MODEL), row),
            pl.BlockSpec((half, tm), col),
            pl.BlockSpec((half, tm), col),
            pl.BlockSpec((half, tm), col),
            pl.BlockSpec((half, tm), col),
            pl.BlockSpec((LANES, tm), col),
            pl.BlockSpec((half, 1), fixed2),
            pl.BlockSpec((half, 1), fixed2),
            pl.BlockSpec((2 * half, D_MODEL), fixed2),
        ],
        out_specs=pl.BlockSpec((tm, D_MODEL), row),
        compiler_params=_cparams(("parallel",)),
        name="out_proj",
    )(x2, y_mlaT, o_cT, o_sT, o_wT, glT, mnw.reshape(-1, 1), nnw.reshape(-1, 1), wo)


def _mlp_kernel(h_ref, nw_ref, wu_ref, wd_ref, fw_ref, o_ref, n_scr, acc_scr, *, final):
    c = pl.program_id(1)

    @pl.when(c == 0)
    def _():
        n_scr[...] = _rms(h_ref[...], nw_ref[...]).astype(BF16)
        acc_scr[...] = jnp.zeros(acc_scr.shape, F32)

    u = jnp.dot(n_scr[...], wu_ref[...], preferred_element_type=F32)
    a = jnp.square(jnp.maximum(u, 0.0)).astype(BF16)
    acc_scr[...] += jnp.dot(a, wd_ref[...], preferred_element_type=F32)

    @pl.when(c == pl.num_programs(1) - 1)
    def _():
        h2 = h_ref[...] + acc_scr[...]
        o_ref[...] = _rms(h2, fw_ref[...]) if final else h2


def _mlp(h1, nw, wu, wd, fw, final, tm=512, tc=1024):
    T = h1.shape[0]
    tm = min(tm, T)
    return pl.pallas_call(
        functools.partial(_mlp_kernel, final=final),
        out_shape=jax.ShapeDtypeStruct((T, D_MODEL), F32),
        grid=(T // tm, MLP_HIDDEN // tc),
        in_specs=[
            pl.BlockSpec((tm, D_MODEL), lambda i, c: (i, 0)),
            pl.BlockSpec((1, D_MODEL), lambda i, c: (0, 0)),
            pl.BlockSpec((D_MODEL, tc), lambda i, c: (0, c)),
            pl.BlockSpec((tc, D_MODEL), lambda i, c: (c, 0)),
            pl.BlockSpec((1, D_MODEL), lambda i, c: (0, 0)),
        ],
        out_specs=pl.BlockSpec((tm, D_MODEL), lambda i, c: (i, 0)),
        scratch_shapes=[pltpu.VMEM((tm, D_MODEL), BF16), pltpu.VMEM((tm, D_MODEL), F32)],
        compiler_params=_cparams(("parallel", "arbitrary")),
        name="mlp",
    )(h1, nw.reshape(1, -1), wu, wd, fw.reshape(1, -1))


def _regroup_w_in(w_in):
    cq = w_in[:, 0:512]
    ckv = w_in[:, 512:768]
    kr = w_in[:, 768:832]
    q = w_in[:, 832:1856]
    kv_cmp = w_in[:, 1856:2112]
    kv_slc = w_in[:, 2112:2368].reshape(-1, NSA_G, 2, NSA_D)
    kv_win = w_in[:, 2368:2624].reshape(-1, NSA_G, 2, NSA_D)
    gates = w_in[:, 2624:2672]
    x1, x2 = kr[:, :32], kr[:, 32:]
    flat = lambda t: t.reshape(t.shape[0], NSA_G * NSA_D)
    w_tok = jnp.concatenate([cq, ckv, x1, x2, x2, x1, kv_cmp, flat(kv_slc[:, :, 0]), flat(kv_win[:, :, 0])], axis=1)
    zeros = jnp.zeros((w_in.shape[0], FM_ROWS - ROW_GATE - gates.shape[1]), w_in.dtype)
    w_fea = jnp.concatenate([q, flat(kv_slc[:, :, 1]), flat(kv_win[:, :, 1]), gates, zeros], axis=1)
    return w_tok.astype(BF16), w_fea.T.astype(BF16)


def _regroup_w_uq(w_uq):
    w = w_uq.reshape(MLA_Q_RANK, MLA_HEADS, MLA_NOPE + MLA_ROPE)
    nope = w[:, :, :MLA_NOPE]
    x1 = w[:, :, MLA_NOPE:MLA_NOPE + 32]
    x2 = w[:, :, MLA_NOPE + 32:]
    return jnp.concatenate([nope, x1, x2, x2, x1], axis=-1).reshape(MLA_Q_RANK, MLA_HEADS * MLA_QK_PAD).astype(BF16)


def kernel(x, positions, attn_norm_w, w_in, mla_q_norm_w, mla_w_uq, mla_kv_norm_w, mla_w_ukv, cmp_pos_k,
           cmp_pos_v, cmp_w1_k, cmp_w2_k, cmp_w1_v, cmp_w2_v, mla_out_norm_w, nsa_out_norm_w, w_o, mlp_norm_w,
           w_up, w_down, final_norm_w):
    B, T, _ = x.shape
    depth = w_in.shape[0]
    inv = 1.0 / (ROPE_THETA ** (jnp.arange(0, MLA_ROPE, 2, dtype=F32) / MLA_ROPE))
    inv128 = jnp.tile(inv, 4).reshape(1, LANES)
    arows, srow = _alibi_rows()
    nsel_pad = -(-(T // SEL_LEN) // LANES) * LANES

    outs = []
    for b in range(B):
        h = x[b]
        pos2 = positions[b].reshape(T, 1)
        for l in range(depth):
            w_tok, w_feaT = _regroup_w_in(w_in[l])
            pm, r4, ka_s, ka_w, qT_nsa, va_s, va_w, glT = _in_proj(h, attn_norm_w[l], w_tok, w_feaT, nsel_pad)
            wukv = mla_w_ukv[l].reshape(MLA_KV_RANK, MLA_HEADS, MLA_NOPE + MLA_V)
            wuk = wukv[:, :, :MLA_NOPE].reshape(MLA_KV_RANK, MLA_HEADS * MLA_NOPE).astype(BF16)
            wuvT = wukv[:, :, MLA_NOPE:].reshape(MLA_KV_RANK, MLA_HEADS * MLA_V).T.astype(BF16)
            qT, k, vT = _mla_prep(pm, pos2, mla_q_norm_w[l], mla_kv_norm_w[l], _regroup_w_uq(mla_w_uq[l]).T,
                                  wuk, wuvT, inv128)
            y_mla = _mla_attn(qT, k, vT)

            half = CMP_STRIDE * NSA_D
            pos_kv = jnp.stack([cmp_pos_k[l].reshape(2, half), cmp_pos_v[l].reshape(2, half)])
            kc, vcT = _compress(r4.reshape(2 * NSA_G, T // CMP_STRIDE, half), pos_kv,
                                jnp.stack([cmp_w1_k[l], cmp_w1_v[l]]), jnp.stack([cmp_w2_k[l], cmp_w2_v[l]]))
            o_cT, selT = _cmp_attn(qT_nsa, arows, kc, vcT, nsel_pad)
            o_sT = _sel_attn(qT_nsa, selT, arows, srow, ka_s, va_s)
            o_wT = _win_attn(qT_nsa, arows, srow, ka_w, va_w)

            h = _out_proj(h, y_mla, o_cT, o_sT, o_wT, glT, mla_out_norm_w[l], nsa_out_norm_w[l],
                          w_o[l].astype(BF16))
            h = _mlp(h, mlp_norm_w[l], w_up[l].astype(BF16), w_down[l].astype(BF16), final_norm_w,
                     final=(l == depth - 1))
        outs.append(h)
    return jnp.stack(outs)
```

```python
import functools

import numpy as np
import jax
import jax.numpy as jnp
from jax import lax
from jax.experimental import pallas as pl
from jax.experimental.pallas import tpu as pltpu

F32 = jnp.float32
BF16 = jnp.bfloat16

EPS = 1e-6
NEG = -1e30
REMOVED = -3e38

D_MODEL = 2048
MLA_HEADS = 8
MLA_Q_RANK = 512
MLA_KV_RANK = 256
MLA_NOPE = 128
MLA_ROPE = 64
MLA_V = 128
ROPE_THETA = 10000.0
MLA_QK_PAD = 256
MLA_VROWS = 144

NSA_HEADS = 16
NSA_G = 2
NSA_HPG = 8
NSA_D = 64
CMP_LEN = 32
CMP_STRIDE = 16
CMP_HIDDEN = 256
SEL_LEN = 64
SEL_SHIFT = 6
SEL_TOPK = 16
WINDOW = 512
FORCE_BONUS = 1e4
MLP_HIDDEN = 4 * D_MODEL

LOG2E = 1.4426950408889634
LANES = 128
SUBLANES = 8
BF16_ROWS = 16
NSA_TQ = 128
SEL_TQ = 256
SEL_TK = 512
MLA_TQ = 512
MLA_TK = 512
VMEM_LIMIT = 56 * 1024 * 1024

SEC_CQ = 0
SEC_CKV = 512
SEC_ROPE = 768
SEC_CMP = 896
SEC_KS = 1152
SEC_KW = 1280
TM_COLS = 1408
MLA_COLS = SEC_CMP
ROW_Q = 0
ROW_VS = 1024
ROW_VW = 1152
ROW_GATE = 1280
FM_ROWS = 1408


def _cparams(sem):
    return pltpu.CompilerParams(dimension_semantics=sem, vmem_limit_bytes=VMEM_LIMIT)


def _rms(x, w):
    return x * lax.rsqrt(jnp.mean(x * x, axis=-1, keepdims=True) + EPS) * w


ALIBI_PARTS = 4
ALIBI_HI = 16
NEG_BF16 = -1e30
SEL_VROWS = 80
NT_DIMS = (((1,), (1,)), ((), ()))
TN_DIMS = (((0,), (0,)), ((), ()))


def _digit_cols(offset, scale):
    lane = lax.broadcasted_iota(jnp.int32, offset.shape, 1)
    hi = jnp.right_shift(offset, 4) * scale
    lo = jnp.bitwise_and(offset, ALIBI_HI - 1) * scale
    return jnp.where(lane < ALIBI_PARTS, hi, jnp.where(lane < 2 * ALIBI_PARTS, lo, 0)).astype(F32)


def _in_proj_kernel(x_ref, nw_ref, w_ref, wT_ref, pm_ref, r_ref, kas_ref, kaw_ref, qT_ref, vas_ref, vaw_ref,
                    glT_ref, *, nsel_pad):
    c = pl.program_id(0)
    tm = x_ref.shape[0]
    n = _rms(x_ref[...], nw_ref[...]).astype(BF16)
    tok = jnp.dot(n, w_ref[...], preferred_element_type=F32)
    fea = lax.dot_general(wT_ref[...], n, NT_DIMS, preferred_element_type=F32)

    pm_ref[...] = tok[:, :MLA_COLS]
    for a in range(2 * NSA_G):
        r_ref[a] = tok[:, SEC_CMP + a * NSA_D:SEC_CMP + (a + 1) * NSA_D]
    r = lax.broadcasted_iota(jnp.int32, (tm, NSA_D), 0)
    blk = lax.broadcasted_iota(jnp.int32, (tm, nsel_pad), 1)
    kblk = jnp.right_shift(c * tm + lax.broadcasted_iota(jnp.int32, (tm, nsel_pad), 0), SEL_SHIFT)
    onehot = jnp.where(blk == kblk, 1.0, 0.0)
    sel_digits = _digit_cols(r, 1)
    win_digits = _digit_cols(jnp.bitwise_and(r, NSA_TQ - 1), 1)
    ones_row = jnp.where(lax.broadcasted_iota(jnp.int32, (SEL_VROWS - NSA_D, tm), 0) == 0, 1.0, 0.0)
    for g in range(NSA_G):
        ks = tok[:, SEC_KS + g * NSA_D:SEC_KS + (g + 1) * NSA_D]
        kw = tok[:, SEC_KW + g * NSA_D:SEC_KW + (g + 1) * NSA_D]
        kas_ref[g] = jnp.concatenate([ks, sel_digits, onehot], axis=-1).astype(BF16)
        kaw_ref[g] = jnp.concatenate([kw, win_digits], axis=-1).astype(BF16)
        vas_ref[g, 0] = jnp.concatenate([fea[ROW_VS + g * NSA_D:ROW_VS + (g + 1) * NSA_D, :], ones_row],
                                        axis=0).astype(BF16)
        vaw_ref[g] = jnp.concatenate([fea[ROW_VW + g * NSA_D:ROW_VW + (g + 1) * NSA_D, :], ones_row],
                                     axis=0).astype(BF16)
    qT_ref[...] = fea[ROW_Q:ROW_VS, :]
    glT_ref[...] = fea[ROW_GATE:, :]


def _in_proj(x2, attn_norm_w, w_tok, w_feaT, nsel_pad, tm=SEL_TK):
    T = x2.shape[0]
    tm = min(tm, T)
    kw = 2 * NSA_D + nsel_pad
    fixed = lambda i: (0, 0)
    return pl.pallas_call(
        functools.partial(_in_proj_kernel, nsel_pad=nsel_pad),
        out_shape=(
            jax.ShapeDtypeStruct((T, MLA_COLS), F32),
            jax.ShapeDtypeStruct((2 * NSA_G, T, NSA_D), F32),
            jax.ShapeDtypeStruct((NSA_G, T, kw), BF16),
            jax.ShapeDtypeStruct((NSA_G, T, 2 * NSA_D), BF16),
            jax.ShapeDtypeStruct((NSA_HEADS * NSA_D, T), F32),
            jax.ShapeDtypeStruct((NSA_G, T // tm, SEL_VROWS, tm), BF16),
            jax.ShapeDtypeStruct((NSA_G, SEL_VROWS, T), BF16),
            jax.ShapeDtypeStruct((LANES, T), F32),
        ),
        grid=(T // tm,),
        in_specs=[
            pl.BlockSpec((tm, D_MODEL), lambda i: (i, 0)),
            pl.BlockSpec((1, D_MODEL), fixed),
            pl.BlockSpec((D_MODEL, TM_COLS), fixed),
            pl.BlockSpec((FM_ROWS, D_MODEL), fixed),
        ],
        out_specs=(
            pl.BlockSpec((tm, MLA_COLS), lambda i: (i, 0)),
            pl.BlockSpec((2 * NSA_G, tm, NSA_D), lambda i: (0, i, 0)),
            pl.BlockSpec((NSA_G, tm, kw), lambda i: (0, i, 0)),
            pl.BlockSpec((NSA_G, tm, 2 * NSA_D), lambda i: (0, i, 0)),
            pl.BlockSpec((NSA_HEADS * NSA_D, tm), lambda i: (0, i)),
            pl.BlockSpec((NSA_G, 1, SEL_VROWS, tm), lambda i: (0, i, 0, 0)),
            pl.BlockSpec((NSA_G, SEL_VROWS, tm), lambda i: (0, 0, i)),
            pl.BlockSpec((LANES, tm), lambda i: (0, i)),
        ),
        compiler_params=_cparams(("parallel",)),
        name="in_proj",
    )(x2, attn_norm_w.reshape(1, D_MODEL), w_tok, w_feaT)


def _mla_prep_kernel(cq_ref, ckv_ref, kr_ref, pos_ref, qnw_ref, kvnw_ref, wuqT_ref, wuk_ref, wuvT_ref, inv_ref,
                     qT_ref, k_ref, vT_ref):
    scale = (MLA_NOPE + MLA_ROPE) ** -0.5 * LOG2E
    qn = _rms(cq_ref[...], qnw_ref[...]).astype(BF16)
    qT = lax.dot_general(wuqT_ref[...], qn, NT_DIMS, preferred_element_type=F32)
    kvn = _rms(ckv_ref[...], kvnw_ref[...]).astype(BF16)
    kn = jnp.dot(kvn, wuk_ref[...], preferred_element_type=F32)
    vT = lax.dot_general(wuvT_ref[...], kvn, NT_DIMS, preferred_element_type=F32)

    ang = pos_ref[...].astype(F32) * inv_ref[...]
    c = jnp.cos(ang)
    s = jnp.sin(ang)
    lane = lax.broadcasted_iota(jnp.int32, ang.shape, 1)
    coef = jnp.where(lane < 64, c, jnp.where(lane < 96, -s, s))
    coefT = coef.T

    ones_row = jnp.where(lax.broadcasted_iota(jnp.int32, (MLA_VROWS - MLA_V, coefT.shape[1]), 0) == 0, 1.0, 0.0)
    t = kr_ref[...] * coef
    kpe = jnp.where(lane < 64, t + pltpu.roll(t, 64, 1), 0.0)
    for h in range(MLA_HEADS):
        base = h * MLA_QK_PAD
        tT = qT[base + 128:base + 256, :] * coefT
        ropeT = tT + pltpu.roll(tT, 64, 0)
        qT_ref[h] = (jnp.concatenate([qT[base:base + 128, :], ropeT], axis=0) * scale).astype(BF16)
        k_ref[h] = jnp.concatenate([kn[:, h * 128:(h + 1) * 128], kpe], axis=-1).astype(BF16)
        vT_ref[h] = jnp.concatenate([vT[h * 128:(h + 1) * 128, :], ones_row], axis=0).astype(BF16)


def _mla_prep(pm, pos2, qnw, kvnw, wuqT, wuk, wuvT, inv128, tm=256):
    T = pm.shape[0]
    H = MLA_HEADS
    fixed = lambda i: (0, 0)
    return pl.pallas_call(
        _mla_prep_kernel,
        out_shape=(
            jax.ShapeDtypeStruct((H, MLA_QK_PAD, T), BF16),
            jax.ShapeDtypeStruct((H, T, MLA_QK_PAD), BF16),
            jax.ShapeDtypeStruct((H, MLA_VROWS, T), BF16),
        ),
        grid=(T // tm,),
        in_specs=[
            pl.BlockSpec((tm, MLA_Q_RANK), lambda i: (i, SEC_CQ // MLA_Q_RANK)),
            pl.BlockSpec((tm, MLA_KV_RANK), lambda i: (i, SEC_CKV // MLA_KV_RANK)),
            pl.BlockSpec((tm, LANES), lambda i: (i, SEC_ROPE // LANES)),
            pl.BlockSpec((tm, 1), lambda i: (i, 0)),
            pl.BlockSpec((1, MLA_Q_RANK), fixed),
            pl.BlockSpec((1, MLA_KV_RANK), fixed),
            pl.BlockSpec((H * MLA_QK_PAD, MLA_Q_RANK), fixed),
            pl.BlockSpec((MLA_KV_RANK, H * MLA_NOPE), fixed),
            pl.BlockSpec((H * MLA_V, MLA_KV_RANK), fixed),
            pl.BlockSpec((1, LANES), fixed),
        ],
        out_specs=(
            pl.BlockSpec((H, MLA_QK_PAD, tm), lambda i: (0, 0, i)),
            pl.BlockSpec((H, tm, MLA_QK_PAD), lambda i: (0, i, 0)),
            pl.BlockSpec((H, MLA_VROWS, tm), lambda i: (0, 0, i)),
        ),
        compiler_params=_cparams(("parallel",)),
        name="mla_prep",
    )(pm, pm, pm, pos2, qnw.reshape(1, -1), kvnw.reshape(1, -1), wuqT, wuk, wuvT, inv128)


def _mla_attn_kernel(qi_ref, ki_ref, qT_ref, k_ref, vT_ref, o_ref, m_scr, acc_scr):
    p_id = pl.program_id(0)
    qi = qi_ref[p_id]
    ki = ki_ref[p_id]

    @pl.when(ki == 0)
    def _():
        m_scr[...] = jnp.full(m_scr.shape, NEG, F32)
        acc_scr[...] = jnp.zeros(acc_scr.shape, F32)

    def step(diag):
        for h in range(MLA_HEADS):
            s = jnp.dot(k_ref[h], qT_ref[h], preferred_element_type=F32)
            if diag:
                key = lax.broadcasted_iota(jnp.int32, s.shape, 0)
                qry = lax.broadcasted_iota(jnp.int32, s.shape, 1)
                s = jnp.where(key <= qry, s, NEG)
            m_prev = m_scr[h]
            m_new = jnp.maximum(m_prev, jnp.max(s, axis=0, keepdims=True))
            p = jnp.exp2(s - m_new).astype(BF16)
            acc = jnp.exp2(m_prev - m_new) * acc_scr[h] + jnp.dot(vT_ref[h], p, preferred_element_type=F32)
            if diag:
                o_ref[h * MLA_V:(h + 1) * MLA_V, :] = acc[:MLA_V, :] * (1.0 / acc[MLA_V:MLA_V + 1, :])
            else:
                m_scr[h] = m_new
                acc_scr[h] = acc

    @pl.when(ki < qi)
    def _():
        step(False)

    @pl.when(ki == qi)
    def _():
        step(True)


def _tri_pairs(nq, per_q):
    qi, ki = [], []
    for i in range(nq):
        for c in range(per_q(i) + 1):
            qi.append(i)
            ki.append(c)
    return jnp.asarray(qi, jnp.int32), jnp.asarray(ki, jnp.int32)


def _mla_attn(qT, k, vT):
    H, T, _ = k.shape
    tq = min(MLA_TQ, T)
    nq = T // tq
    qi, ki = _tri_pairs(nq, lambda i: i)
    grid_spec = pltpu.PrefetchScalarGridSpec(
        num_scalar_prefetch=2,
        grid=(int(qi.shape[0]),),
        in_specs=[
            pl.BlockSpec((H, MLA_QK_PAD, tq), lambda p, qi, ki: (0, 0, qi[p])),
            pl.BlockSpec((H, tq, MLA_QK_PAD), lambda p, qi, ki: (0, ki[p], 0)),
            pl.BlockSpec((H, MLA_VROWS, tq), lambda p, qi, ki: (0, 0, ki[p])),
        ],
        out_specs=pl.BlockSpec((H * MLA_V, tq), lambda p, qi, ki: (0, qi[p])),
        scratch_shapes=[
            pltpu.VMEM((H, 1, tq), F32),
            pltpu.VMEM((H, MLA_VROWS, tq), F32),
        ],
    )
    return pl.pallas_call(
        _mla_attn_kernel,
        out_shape=jax.ShapeDtypeStruct((H * MLA_V, T), F32),
        grid_spec=grid_spec,
        compiler_params=_cparams(("arbitrary",)),
        name="mla_attn",
    )(qi, ki, qT, k, vT)


def _gelu_tanh(x):
    return 0.5 * x * (1.0 + jnp.tanh(np.sqrt(2.0 / np.pi).astype(np.float32) * (x + 0.044715 * (x * x * x))))


def _compress_kernel(r_ref, pos_ref, w1_ref, w2_ref, kc_ref, vcT_ref):
    r = r_ref[0]
    nr = r.shape[0]
    half = CMP_STRIDE * NSA_D
    hp = lax.Precision.HIGHEST
    a = jnp.dot(r + pos_ref[0, 0:1, :], w1_ref[0, :half, :], precision=hp, preferred_element_type=F32)
    b = jnp.dot(r + pos_ref[0, 1:2, :], w1_ref[0, half:, :], precision=hp, preferred_element_type=F32)
    hid = _gelu_tanh(a + pltpu.roll(b, nr - 1, 0))
    out = jnp.dot(hid, w2_ref[0], precision=hp, preferred_element_type=F32)
    row = lax.broadcasted_iota(jnp.int32, out.shape, 0)
    out = jnp.where(row < nr - 1, out, 0.0)
    is_key = pl.program_id(0) % 2 == 0

    @pl.when(is_key)
    def _():
        kc_ref[0] = jnp.concatenate([out, _digit_cols(row, CMP_STRIDE)], axis=-1).astype(BF16)

    @pl.when(jnp.logical_not(is_key))
    def _():
        vcT_ref[0] = jnp.concatenate([out, jnp.zeros_like(out)], axis=-1).T[0:NSA_D, :].astype(BF16)


def _compress(r4, pos_kv, w1_kv, w2_kv):
    _, nr, width = r4.shape
    return pl.pallas_call(
        _compress_kernel,
        out_shape=(
            jax.ShapeDtypeStruct((NSA_G, nr, 2 * NSA_D), BF16),
            jax.ShapeDtypeStruct((NSA_G, NSA_D, nr), BF16),
        ),
        grid=(2 * NSA_G,),
        in_specs=[
            pl.BlockSpec((1, nr, width), lambda a: (a, 0, 0)),
            pl.BlockSpec((1, 2, width), lambda a: (a % 2, 0, 0)),
            pl.BlockSpec((1, 2 * width, CMP_HIDDEN), lambda a: (a % 2, 0, 0)),
            pl.BlockSpec((1, CMP_HIDDEN, NSA_D), lambda a: (a % 2, 0, 0)),
        ],
        out_specs=(
            pl.BlockSpec((1, nr, 2 * NSA_D), lambda a: (a // 2, 0, 0)),
            pl.BlockSpec((1, NSA_D, nr), lambda a: (a // 2, 0, 0)),
        ),
        compiler_params=_cparams(("arbitrary",)),
        name="nsa_compress",
    )(r4, pos_kv, w1_kv, w2_kv)


def _alibi_rows(tq):
    start = np.float32(2.0 ** (-8.0 / NSA_HEADS))
    slopes = (start ** np.arange(1, NSA_HEADS + 1, dtype=np.float32)).astype(np.float32)
    slopes = (slopes.astype(np.float64) * LOG2E).astype(np.float32)
    parts, rest = [], slopes
    for _ in range(ALIBI_PARTS):
        piece = (rest.view(np.uint32) & np.uint32(0xFFFF0000)).view(np.float32)
        parts.append(piece)
        rest = rest - piece
    assert not rest.any()
    pieces = np.stack(parts)
    rows = np.concatenate([pieces * ALIBI_HI, pieces, np.zeros((NSA_D - 2 * ALIBI_PARTS, NSA_HEADS), np.float32)])
    rows = np.repeat(rows.reshape(NSA_D, NSA_G, NSA_HPG), tq, axis=2).transpose(1, 0, 2)
    srow = np.repeat(slopes.reshape(NSA_G, 1, NSA_HPG), tq, axis=2)
    return jnp.asarray(rows, BF16), jnp.asarray(srow, F32)


def _nsa_queries(qT_ref, arow_ref):
    q = qT_ref[...] * (NSA_D ** -0.5 * LOG2E)
    top = jnp.concatenate([q[h * NSA_D:(h + 1) * NSA_D, :] for h in range(NSA_HPG)], axis=1).astype(BF16)
    return jnp.concatenate([top, arow_ref[0]], axis=0)


def _heads_to_rows(o):
    tq = o.shape[1] // NSA_HPG
    return jnp.concatenate([o[:, h * tq:(h + 1) * tq] for h in range(NSA_HPG)], axis=0)


def _token_pos(i, tq):
    return i * tq + jnp.bitwise_and(lax.broadcasted_iota(jnp.int32, (1, NSA_HPG * tq), 1), tq - 1)


def _cmp_attn_kernel(qT_ref, arow_ref, kc_ref, vcT_ref, o_ref, selT_ref, *, nsel_pad):
    i = pl.program_id(1)
    tq = NSA_TQ
    n = NSA_HPG * tq
    nr = kc_ref.shape[1]
    s = jnp.dot(kc_ref[0], _nsa_queries(qT_ref, arow_ref), preferred_element_type=F32)
    tpos = _token_pos(i, tq)
    last_done = jnp.right_shift(tpos - (CMP_LEN - 1), 4)
    valid = lax.broadcasted_iota(jnp.int32, (nr, n), 0) <= last_done
    s = jnp.where(valid, s, NEG)
    e = jnp.where(valid, jnp.exp2(s - jnp.max(s, axis=0, keepdims=True)), 0.0)
    l = jnp.sum(e, axis=0, keepdims=True)
    inv = 1.0 / jnp.where(l > 0.0, l, 1.0)
    acc = jnp.dot(vcT_ref[0], e.astype(BF16), preferred_element_type=F32)
    o_ref[...] = _heads_to_rows(acc * inv)

    pn = e * inv
    psum = pn[:, 0:tq]
    for h in range(1, NSA_HPG):
        psum = psum + pn[:, h * tq:(h + 1) * tq]
    j2 = lax.broadcasted_iota(jnp.int32, (nsel_pad, nr), 0)
    n2 = lax.broadcasted_iota(jnp.int32, (nsel_pad, nr), 1)
    ov = jnp.logical_and(n2 * CMP_STRIDE <= j2 * SEL_LEN + (SEL_LEN - 1),
                         n2 * CMP_STRIDE + (CMP_LEN - 1) >= j2 * SEL_LEN)
    ov = jnp.logical_and(ov, n2 < nr - 1)
    imp = jnp.dot(jnp.where(ov, 1.0, 0.0).astype(F32), psum, precision=lax.Precision.HIGHEST,
                  preferred_element_type=F32)

    blk = lax.broadcasted_iota(jnp.int32, (nsel_pad, tq), 0)
    cur = jnp.right_shift(i * tq + lax.broadcasted_iota(jnp.int32, (nsel_pad, tq), 1), SEL_SHIFT)
    forced = jnp.logical_or(blk == 0, jnp.logical_or(blk == cur, blk == cur - 1))
    score = jnp.where(blk <= cur, imp + jnp.where(forced, FORCE_BONUS, 0.0), NEG)
    blk_f = blk.astype(F32)

    def pick(_, carry):
        sc, sel = carry
        mx = jnp.max(sc, axis=0, keepdims=True)
        first = jnp.min(jnp.where(sc == mx, blk_f, float(nsel_pad)), axis=0, keepdims=True)
        hit = blk_f == first
        sel = jnp.where(jnp.logical_and(hit, mx > NEG * 0.5), 1.0, sel)
        return jnp.where(hit, REMOVED, sc), sel

    _, sel = lax.fori_loop(0, SEL_TOPK, pick, (score, jnp.zeros((nsel_pad, tq), F32)))
    selT_ref[0] = sel


def _cmp_attn(fm, arows, kc, vcT, nsel_pad):
    T = fm.shape[1]
    nr = kc.shape[1]
    tq = NSA_TQ
    n = NSA_HPG * tq
    return pl.pallas_call(
        functools.partial(_cmp_attn_kernel, nsel_pad=nsel_pad),
        out_shape=(
            jax.ShapeDtypeStruct((NSA_HEADS * NSA_D, T), F32),
            jax.ShapeDtypeStruct((NSA_G, nsel_pad, T), F32),
        ),
        grid=(NSA_G, T // tq),
        in_specs=[
            pl.BlockSpec((NSA_HPG * NSA_D, tq), lambda g, i: (g, i)),
            pl.BlockSpec((1, NSA_D, n), lambda g, i: (g, 0, 0)),
            pl.BlockSpec((1, nr, 2 * NSA_D), lambda g, i: (g, 0, 0)),
            pl.BlockSpec((1, NSA_D, nr), lambda g, i: (g, 0, 0)),
        ],
        out_specs=(
            pl.BlockSpec((NSA_HPG * NSA_D, tq), lambda g, i: (g, i)),
            pl.BlockSpec((1, nsel_pad, tq), lambda g, i: (g, 0, i)),
        ),
        compiler_params=_cparams(("parallel", "parallel")),
        name="nsa_cmp_attn",
    )(fm, arows, kc, vcT)


def _sel_attn_kernel(qT_ref, selT_ref, arow_ref, srow_ref, ka_ref, va_ref, o_ref,
                     qa_scr, s0_scr, s1_scr, top0_scr, top1_scr, p0_scr, p1_scr, al0_scr, al1_scr, m_scr, acc_scr,
                     *, tk, n_chunks):
    i = pl.program_id(1)
    tq = SEL_TQ
    n = NSA_HPG * tq
    nsel_pad = selT_ref.shape[1]
    qa_scr[0:2 * NSA_D, :] = _nsa_queries(qT_ref, arow_ref)
    blk = lax.broadcasted_iota(jnp.int32, (nsel_pad, tq), 0)
    own = blk // (tq // SEL_LEN) == i
    chosen = jnp.logical_and(selT_ref[0] > 0.5, jnp.logical_not(own))
    qa_scr[2 * NSA_D:, :] = jnp.concatenate([jnp.where(chosen, 0.0, NEG_BF16).astype(BF16)] * NSA_HPG, axis=1)

    tpos = _token_pos(i, tq)
    slope_row = srow_ref[0]

    def rterm(c):
        return slope_row * (c * tk - tpos).astype(F32)

    c_own = (i * tq) // tk
    own_start = pl.multiple_of(i * tq, tq)
    s = jnp.dot(ka_ref[0, pl.ds(own_start, tq), 0:2 * NSA_D], qa_scr[0:2 * NSA_D, :], preferred_element_type=F32)
    kpos = i * tq + lax.broadcasted_iota(jnp.int32, s.shape, 0)
    s = jnp.where(kpos <= tpos, s, NEG)
    r_own = rterm(c_own)
    m0 = jnp.max(s, axis=0, keepdims=True) + r_own
    p = jnp.exp2(s - (m0 - r_own)).astype(BF16)
    v_own = va_ref[0, c_own, :, pl.ds(pl.multiple_of(i * tq - c_own * tk, tq), tq)]
    acc_scr[...] = jnp.dot(v_own, p, preferred_element_type=F32)
    m_scr[...] = m0

    s_slots = (s0_scr, s1_scr)
    p_slots = (p0_scr, p1_scr)
    al_slots = (al0_scr, al1_scr)

    top_slots = (top0_scr, top1_scr)

    def scores(j, slot):
        start = pl.multiple_of(j * tk, tk)
        s = jnp.dot(ka_ref[0, pl.ds(start, tk), :], qa_scr[...], preferred_element_type=F32)
        s_slots[slot][...] = s
        top_slots[slot][...] = jnp.max(s.reshape(tk // SUBLANES, SUBLANES, n), axis=0)

    def softmax(j, slot):
        r = rterm(j)
        m_prev = m_scr[...]
        s_ref, p_ref = s_slots[slot], p_slots[slot]
        m_new = jnp.maximum(m_prev, jnp.max(top_slots[slot][...], axis=0, keepdims=True) + r)
        shift = m_new - r
        for b in range(tk // BF16_ROWS):
            rows = slice(b * BF16_ROWS, (b + 1) * BF16_ROWS)
            p_ref[rows, :] = jnp.exp2(s_ref[rows, :] - shift).astype(BF16)
        al_slots[slot][...] = jnp.exp2(m_prev - m_new)
        m_scr[...] = m_new

    def values(j, slot):
        acc_scr[...] = (al_slots[slot][...] * acc_scr[...]
                        + jnp.dot(va_ref[0, j], p_slots[slot][...], preferred_element_type=F32))

    n_pairs = jnp.minimum(((i * tq + tq - 1) // tk + 2) // 2, n_chunks // 2)
    scores(0, 0)
    scores(1, 1)
    softmax(0, 0)

    def body(jj, carry):
        j = 2 * jj
        scores(j, 0)
        softmax(j - 1, 1)
        values(j - 2, 0)
        scores(j + 1, 1)
        softmax(j, 0)
        values(j - 1, 1)
        return carry

    lax.fori_loop(1, n_pairs, body, 0)
    softmax(2 * n_pairs - 1, 1)
    values(2 * n_pairs - 2, 0)
    values(2 * n_pairs - 1, 1)
    acc = acc_scr[...]
    o_ref[...] = _heads_to_rows(acc[0:NSA_D, :] * (1.0 / acc[NSA_D:NSA_D + 1, :]))


def _sel_attn(fm, selT, arows, srow, ka, va):
    _, T, kw = ka.shape
    tk = va.shape[-1]
    tq = min(SEL_TQ, T)
    nsel_pad = selT.shape[1]
    n = NSA_HPG * tq
    return pl.pallas_call(
        functools.partial(_sel_attn_kernel, tk=tk, n_chunks=T // tk),
        out_shape=jax.ShapeDtypeStruct((NSA_HEADS * NSA_D, T), F32),
        grid=(NSA_G, T // tq),
        in_specs=[
            pl.BlockSpec((NSA_HPG * NSA_D, tq), lambda g, i: (g, i)),
            pl.BlockSpec((1, nsel_pad, tq), lambda g, i: (g, 0, i)),
            pl.BlockSpec((1, NSA_D, n), lambda g, i: (g, 0, 0)),
            pl.BlockSpec((1, 1, n), lambda g, i: (g, 0, 0)),
            pl.BlockSpec((1, T, kw), lambda g, i: (g, 0, 0)),
            pl.BlockSpec((1, T // tk, SEL_VROWS, tk), lambda g, i: (g, 0, 0, 0)),
        ],
        out_specs=pl.BlockSpec((NSA_HPG * NSA_D, tq), lambda g, i: (g, i)),
        scratch_shapes=[
            pltpu.VMEM((kw, n), BF16),
            pltpu.VMEM((tk, n), F32),
            pltpu.VMEM((tk, n), F32),
            pltpu.VMEM((SUBLANES, n), F32),
            pltpu.VMEM((SUBLANES, n), F32),
            pltpu.VMEM((tk, n), BF16),
            pltpu.VMEM((tk, n), BF16),
            pltpu.VMEM((1, n), F32),
            pltpu.VMEM((1, n), F32),
            pltpu.VMEM((1, n), F32),
            pltpu.VMEM((SEL_VROWS, n), F32),
        ],
        compiler_params=_cparams(("parallel", "arbitrary")),
        name="nsa_sel_attn",
    )(fm, selT, arows, srow, ka, va)


WIN_KEYS = WINDOW + NSA_TQ


def _win_attn_kernel(qT_ref, arow_ref, srow_ref, ka_ref, va_ref, o_ref):
    i = pl.program_id(1)
    tq = NSA_TQ
    n = NSA_HPG * tq
    start = pl.multiple_of(jnp.maximum(i * tq - WINDOW, 0), tq)
    s = jnp.dot(ka_ref[0, pl.ds(start, WIN_KEYS), :], _nsa_queries(qT_ref, arow_ref), preferred_element_type=F32)
    tpos = _token_pos(i, tq)
    kpos = start + lax.broadcasted_iota(jnp.int32, s.shape, 0)
    allowed = jnp.logical_and(kpos <= tpos, kpos > tpos - WINDOW)
    s = jnp.where(allowed, s, NEG)
    slope_row = srow_ref[0]
    tiles = [s[j * tq:(j + 1) * tq, :] for j in range(WIN_KEYS // tq)]
    rterms = [slope_row * (start + j * tq - tpos).astype(F32) for j in range(WIN_KEYS // tq)]
    m = jnp.max(tiles[0], axis=0, keepdims=True) + rterms[0]
    for sj, rj in zip(tiles[1:], rterms[1:]):
        m = jnp.maximum(m, jnp.max(sj, axis=0, keepdims=True) + rj)
    p = jnp.concatenate([jnp.exp2(sj - (m - rj)) for sj, rj in zip(tiles, rterms)], axis=0).astype(BF16)
    acc = jnp.dot(va_ref[0, :, pl.ds(start, WIN_KEYS)], p, preferred_element_type=F32)
    o_ref[...] = _heads_to_rows(acc[0:NSA_D, :] * (1.0 / acc[NSA_D:NSA_D + 1, :]))


def _win_attn(fm, arows, srow, ka, va):
    _, T, kw = ka.shape
    assert T >= WIN_KEYS
    tq = NSA_TQ
    n = NSA_HPG * tq
    return pl.pallas_call(
        _win_attn_kernel,
        out_shape=jax.ShapeDtypeStruct((NSA_HEADS * NSA_D, T), F32),
        grid=(NSA_G, T // tq),
        in_specs=[
            pl.BlockSpec((NSA_HPG * NSA_D, tq), lambda g, i: (g, i)),
            pl.BlockSpec((1, NSA_D, n), lambda g, i: (g, 0, 0)),
            pl.BlockSpec((1, 1, n), lambda g, i: (g, 0, 0)),
            pl.BlockSpec((1, T, kw), lambda g, i: (g, 0, 0)),
            pl.BlockSpec((1, SEL_VROWS, T), lambda g, i: (g, 0, 0)),
        ],
        out_specs=pl.BlockSpec((NSA_HPG * NSA_D, tq), lambda g, i: (g, i)),
        compiler_params=_cparams(("parallel", "parallel")),
        name="nsa_win_attn",
    )(fm, arows, srow, ka, va)


def _out_proj_kernel(x_ref, ymT_ref, ocT_ref, osT_ref, owT_ref, glT_ref, mnw_ref, nnw_ref, wo_ref, o_ref):
    gates = 1.0 / (1.0 + jnp.exp(-glT_ref[...]))
    heads = []
    for h in range(NSA_HEADS):
        rows = slice(h * NSA_D, (h + 1) * NSA_D)
        heads.append(gates[3 * h:3 * h + 1, :] * ocT_ref[rows, :]
                     + gates[3 * h + 1:3 * h + 2, :] * osT_ref[rows, :]
                     + gates[3 * h + 2:3 * h + 3, :] * owT_ref[rows, :])
    ynT = jnp.concatenate(heads, axis=0)

    def rms_cols(y, w_col):
        return (y * lax.rsqrt(jnp.mean(y * y, axis=0, keepdims=True) + EPS) * w_col).astype(BF16)

    half = MLA_HEADS * MLA_V
    o_ref[...] = (x_ref[...]
                  + lax.dot_general(rms_cols(ymT_ref[...], mnw_ref[...]), wo_ref[:half, :], TN_DIMS,
                                    preferred_element_type=F32)
                  + lax.dot_general(rms_cols(ynT, nnw_ref[...]), wo_ref[half:, :], TN_DIMS,
                                    preferred_element_type=F32))


def _out_proj(x2, y_mlaT, o_cT, o_sT, o_wT, glT, mnw, nnw, wo, tm=512):
    T = x2.shape[0]
    half = MLA_HEADS * MLA_V
    row = lambda i: (i, 0)
    col = lambda i: (0, i)
    fixed2 = lambda i: (0, 0)
    return pl.pallas_call(
        _out_proj_kernel,
        out_shape=jax.ShapeDtypeStruct((T, D_MODEL), F32),
        grid=(T // tm,),
        in_specs=[
            pl.BlockSpec((tm, D_MODEL), row),
            pl.BlockSpec((half, tm), col),
            pl.BlockSpec((half, tm), col),
            pl.BlockSpec((half, tm), col),
            pl.BlockSpec((half, tm), col),
            pl.BlockSpec((LANES, tm), col),
            pl.BlockSpec((half, 1), fixed2),
            pl.BlockSpec((half, 1), fixed2),
            pl.BlockSpec((2 * half, D_MODEL), fixed2),
        ],
        out_specs=pl.BlockSpec((tm, D_MODEL), row),
        compiler_params=_cparams(("parallel",)),
        name="out_proj",
    )(x2, y_mlaT, o_cT, o_sT, o_wT, glT, mnw.reshape(-1, 1), nnw.reshape(-1, 1), wo)


def _mlp_kernel(h_ref, nw_ref, wu_ref, wd_ref, fw_ref, o_ref, n_scr, acc_scr, *, final):
    c = pl.program_id(1)

    @pl.when(c == 0)
    def _():
        n_scr[...] = _rms(h_ref[...], nw_ref[...]).astype(BF16)
        acc_scr[...] = jnp.zeros(acc_scr.shape, F32)

    u = jnp.dot(n_scr[...], wu_ref[...], preferred_element_type=F32)
    a = jnp.square(jnp.maximum(u, 0.0)).astype(BF16)
    acc_scr[...] += jnp.dot(a, wd_ref[...], preferred_element_type=F32)

    @pl.when(c == pl.num_programs(1) - 1)
    def _():
        h2 = h_ref[...] + acc_scr[...]
        o_ref[...] = _rms(h2, fw_ref[...]) if final else h2


def _mlp(h1, nw, wu, wd, fw, final, tm=512, tc=1024):
    T = h1.shape[0]
    tm = min(tm, T)
    return pl.pallas_call(
        functools.partial(_mlp_kernel, final=final),
        out_shape=jax.ShapeDtypeStruct((T, D_MODEL), F32),
        grid=(T // tm, MLP_HIDDEN // tc),
        in_specs=[
            pl.BlockSpec((tm, D_MODEL), lambda i, c: (i, 0)),
            pl.BlockSpec((1, D_MODEL), lambda i, c: (0, 0)),
            pl.BlockSpec((D_MODEL, tc), lambda i, c: (0, c)),
            pl.BlockSpec((tc, D_MODEL), lambda i, c: (c, 0)),
            pl.BlockSpec((1, D_MODEL), lambda i, c: (0, 0)),
        ],
        out_specs=pl.BlockSpec((tm, D_MODEL), lambda i, c: (i, 0)),
        scratch_shapes=[pltpu.VMEM((tm, D_MODEL), BF16), pltpu.VMEM((tm, D_MODEL), F32)],
        compiler_params=_cparams(("parallel", "arbitrary")),
        name="mlp",
    )(h1, nw.reshape(1, -1), wu, wd, fw.reshape(1, -1))


def _regroup_w_in(w_in):
    cq = w_in[:, 0:512]
    ckv = w_in[:, 512:768]
    kr = w_in[:, 768:832]
    q = w_in[:, 832:1856]
    kv_cmp = w_in[:, 1856:2112]
    kv_slc = w_in[:, 2112:2368].reshape(-1, NSA_G, 2, NSA_D)
    kv_win = w_in[:, 2368:2624].reshape(-1, NSA_G, 2, NSA_D)
    gates = w_in[:, 2624:2672]
    x1, x2 = kr[:, :32], kr[:, 32:]
    flat = lambda t: t.reshape(t.shape[0], NSA_G * NSA_D)
    w_tok = jnp.concatenate([cq, ckv, x1, x2, x2, x1, kv_cmp, flat(kv_slc[:, :, 0]), flat(kv_win[:, :, 0])], axis=1)
    zeros = jnp.zeros((w_in.shape[0], FM_ROWS - ROW_GATE - gates.shape[1]), w_in.dtype)
    w_fea = jnp.concatenate([q, flat(kv_slc[:, :, 1]), flat(kv_win[:, :, 1]), gates, zeros], axis=1)
    return w_tok.astype(BF16), w_fea.T.astype(BF16)


def _regroup_w_uq(w_uq):
    w = w_uq.reshape(MLA_Q_RANK, MLA_HEADS, MLA_NOPE + MLA_ROPE)
    nope = w[:, :, :MLA_NOPE]
    x1 = w[:, :, MLA_NOPE:MLA_NOPE + 32]
    x2 = w[:, :, MLA_NOPE + 32:]
    return jnp.concatenate([nope, x1, x2, x2, x1], axis=-1).reshape(MLA_Q_RANK, MLA_HEADS * MLA_QK_PAD).astype(BF16)


def kernel(x, positions, attn_norm_w, w_in, mla_q_norm_w, mla_w_uq, mla_kv_norm_w, mla_w_ukv, cmp_pos_k,
           cmp_pos_v, cmp_w1_k, cmp_w2_k, cmp_w1_v, cmp_w2_v, mla_out_norm_w, nsa_out_norm_w, w_o, mlp_norm_w,
           w_up, w_down, final_norm_w):
    B, T, _ = x.shape
    depth = w_in.shape[0]
    inv = 1.0 / (ROPE_THETA ** (jnp.arange(0, MLA_ROPE, 2, dtype=F32) / MLA_ROPE))
    inv128 = jnp.tile(inv, 4).reshape(1, LANES)
    arows, srow = _alibi_rows(NSA_TQ)
    arows_sel, srow_sel = _alibi_rows(SEL_TQ)
    nsel_pad = -(-(T // SEL_LEN) // LANES) * LANES

    outs = []
    for b in range(B):
        h = x[b]
        pos2 = positions[b].reshape(T, 1)
        for l in range(depth):
            w_tok, w_feaT = _regroup_w_in(w_in[l])
            pm, r4, ka_s, ka_w, qT_nsa, va_s, va_w, glT = _in_proj(h, attn_norm_w[l], w_tok, w_feaT, nsel_pad)
            wukv = mla_w_ukv[l].reshape(MLA_KV_RANK, MLA_HEADS, MLA_NOPE + MLA_V)
            wuk = wukv[:, :, :MLA_NOPE].reshape(MLA_KV_RANK, MLA_HEADS * MLA_NOPE).astype(BF16)
            wuvT = wukv[:, :, MLA_NOPE:].reshape(MLA_KV_RANK, MLA_HEADS * MLA_V).T.astype(BF16)
            qT, k, vT = _mla_prep(pm, pos2, mla_q_norm_w[l], mla_kv_norm_w[l], _regroup_w_uq(mla_w_uq[l]).T,
                                  wuk, wuvT, inv128)
            y_mla = _mla_attn(qT, k, vT)

            half = CMP_STRIDE * NSA_D
            pos_kv = jnp.stack([cmp_pos_k[l].reshape(2, half), cmp_pos_v[l].reshape(2, half)])
            kc, vcT = _compress(r4.reshape(2 * NSA_G, T // CMP_STRIDE, half), pos_kv,
                                jnp.stack([cmp_w1_k[l], cmp_w1_v[l]]), jnp.stack([cmp_w2_k[l], cmp_w2_v[l]]))
            o_cT, selT = _cmp_attn(qT_nsa, arows, kc, vcT, nsel_pad)
            o_sT = _sel_attn(qT_nsa, selT, arows_sel, srow_sel, ka_s, va_s)
            o_wT = _win_attn(qT_nsa, arows, srow, ka_w, va_w)

            h = _out_proj(h, y_mla, o_cT, o_sT, o_wT, glT, mla_out_norm_w[l], nsa_out_norm_w[l],
                          w_o[l].astype(BF16))
            h = _mlp(h, mlp_norm_w[l], w_up[l].astype(BF16), w_down[l].astype(BF16), final_norm_w,
                     final=(l == depth - 1))
        outs.append(h)
    return jnp.stack(outs)
```

```python
import functools

import numpy as np
import jax
import jax.numpy as jnp
from jax import lax
from jax.experimental import pallas as pl
from jax.experimental.pallas import tpu as pltpu

F32 = jnp.float32
BF16 = jnp.bfloat16

EPS = 1e-6
NEG = -1e30
REMOVED = -3e38

D_MODEL = 2048
MLA_HEADS = 8
MLA_Q_RANK = 512
MLA_KV_RANK = 256
MLA_NOPE = 128
MLA_ROPE = 64
MLA_V = 128
ROPE_THETA = 10000.0
MLA_QK_PAD = 256
MLA_VROWS = 144

NSA_HEADS = 16
NSA_G = 2
NSA_HPG = 8
NSA_D = 64
CMP_LEN = 32
CMP_STRIDE = 16
CMP_HIDDEN = 256
SEL_LEN = 64
SEL_SHIFT = 6
SEL_TOPK = 16
WINDOW = 512
FORCE_BONUS = 1e4
MLP_HIDDEN = 4 * D_MODEL

LOG2E = 1.4426950408889634
LANES = 128
SUBLANES = 8
BF16_ROWS = 16
NSA_TQ = 128
SEL_TQ = 256
SEL_TK = 512
MLA_TQ = 512
MLA_TK = 512
VMEM_LIMIT = 56 * 1024 * 1024

SEC_CQ = 0
SEC_CKV = 512
SEC_ROPE = 768
SEC_CMP = 896
SEC_KS = 1152
SEC_KW = 1280
TM_COLS = 1408
MLA_COLS = SEC_CMP
ROW_Q = 0
ROW_VS = 1024
ROW_VW = 1152
ROW_GATE = 1280
FM_ROWS = 1408


def _cparams(sem):
    return pltpu.CompilerParams(dimension_semantics=sem, vmem_limit_bytes=VMEM_LIMIT)


def _rms(x, w):
    return x * lax.rsqrt(jnp.mean(x * x, axis=-1, keepdims=True) + EPS) * w


ALIBI_PARTS = 4
ALIBI_HI = 16
NEG_BF16 = -1e30
SEL_VROWS = 80
NT_DIMS = (((1,), (1,)), ((), ()))
TN_DIMS = (((0,), (0,)), ((), ()))


def _digit_cols(offset, scale):
    lane = lax.broadcasted_iota(jnp.int32, offset.shape, 1)
    hi = jnp.right_shift(offset, 4) * scale
    lo = jnp.bitwise_and(offset, ALIBI_HI - 1) * scale
    return jnp.where(lane < ALIBI_PARTS, hi, jnp.where(lane < 2 * ALIBI_PARTS, lo, 0)).astype(F32)


def _in_proj_kernel(x_ref, nw_ref, w_ref, wT_ref, pm_ref, r_ref, kas_ref, kaw_ref, qT_ref, vas_ref, vaw_ref,
                    glT_ref, *, nsel_pad):
    c = pl.program_id(0)
    tm = x_ref.shape[0]
    n = _rms(x_ref[...], nw_ref[...]).astype(BF16)
    tok = jnp.dot(n, w_ref[...], preferred_element_type=F32)
    fea = lax.dot_general(wT_ref[...], n, NT_DIMS, preferred_element_type=F32)

    pm_ref[...] = tok[:, :MLA_COLS]
    for a in range(2 * NSA_G):
        r_ref[a] = tok[:, SEC_CMP + a * NSA_D:SEC_CMP + (a + 1) * NSA_D]
    r = lax.broadcasted_iota(jnp.int32, (tm, NSA_D), 0)
    blk = lax.broadcasted_iota(jnp.int32, (tm, nsel_pad), 1)
    kblk = jnp.right_shift(c * tm + lax.broadcasted_iota(jnp.int32, (tm, nsel_pad), 0), SEL_SHIFT)
    onehot = jnp.where(blk == kblk, 1.0, 0.0)
    sel_digits = _digit_cols(r, 1)
    win_digits = _digit_cols(jnp.bitwise_and(r, NSA_TQ - 1), 1)
    ones_row = jnp.where(lax.broadcasted_iota(jnp.int32, (SEL_VROWS - NSA_D, tm), 0) == 0, 1.0, 0.0)
    for g in range(NSA_G):
        ks = tok[:, SEC_KS + g * NSA_D:SEC_KS + (g + 1) * NSA_D]
        kw = tok[:, SEC_KW + g * NSA_D:SEC_KW + (g + 1) * NSA_D]
        kas_ref[g] = jnp.concatenate([ks, sel_digits, onehot], axis=-1).astype(BF16)
        kaw_ref[g] = jnp.concatenate([kw, win_digits], axis=-1).astype(BF16)
        vas_ref[g, 0] = jnp.concatenate([fea[ROW_VS + g * NSA_D:ROW_VS + (g + 1) * NSA_D, :], ones_row],
                                        axis=0).astype(BF16)
        vaw_ref[g] = jnp.concatenate([fea[ROW_VW + g * NSA_D:ROW_VW + (g + 1) * NSA_D, :], ones_row],
                                     axis=0).astype(BF16)
    qT_ref[...] = fea[ROW_Q:ROW_VS, :]
    glT_ref[...] = fea[ROW_GATE:, :]


def _in_proj(x2, attn_norm_w, w_tok, w_feaT, nsel_pad, tm=SEL_TK):
    T = x2.shape[0]
    tm = min(tm, T)
    kw = 2 * NSA_D + nsel_pad
    fixed = lambda i: (0, 0)
    return pl.pallas_call(
        functools.partial(_in_proj_kernel, nsel_pad=nsel_pad),
        out_shape=(
            jax.ShapeDtypeStruct((T, MLA_COLS), F32),
            jax.ShapeDtypeStruct((2 * NSA_G, T, NSA_D), F32),
            jax.ShapeDtypeStruct((NSA_G, T, kw), BF16),
            jax.ShapeDtypeStruct((NSA_G, T, 2 * NSA_D), BF16),
            jax.ShapeDtypeStruct((NSA_HEADS * NSA_D, T), F32),
            jax.ShapeDtypeStruct((NSA_G, T // tm, SEL_VROWS, tm), BF16),
            jax.ShapeDtypeStruct((NSA_G, SEL_VROWS, T), BF16),
            jax.ShapeDtypeStruct((LANES, T), F32),
        ),
        grid=(T // tm,),
        in_specs=[
            pl.BlockSpec((tm, D_MODEL), lambda i: (i, 0)),
            pl.BlockSpec((1, D_MODEL), fixed),
            pl.BlockSpec((D_MODEL, TM_COLS), fixed),
            pl.BlockSpec((FM_ROWS, D_MODEL), fixed),
        ],
        out_specs=(
            pl.BlockSpec((tm, MLA_COLS), lambda i: (i, 0)),
            pl.BlockSpec((2 * NSA_G, tm, NSA_D), lambda i: (0, i, 0)),
            pl.BlockSpec((NSA_G, tm, kw), lambda i: (0, i, 0)),
            pl.BlockSpec((NSA_G, tm, 2 * NSA_D), lambda i: (0, i, 0)),
            pl.BlockSpec((NSA_HEADS * NSA_D, tm), lambda i: (0, i)),
            pl.BlockSpec((NSA_G, 1, SEL_VROWS, tm), lambda i: (0, i, 0, 0)),
            pl.BlockSpec((NSA_G, SEL_VROWS, tm), lambda i: (0, 0, i)),
            pl.BlockSpec((LANES, tm), lambda i: (0, i)),
        ),
        compiler_params=_cparams(("parallel",)),
        name="in_proj",
    )(x2, attn_norm_w.reshape(1, D_MODEL), w_tok, w_feaT)


def _mla_prep_kernel(cq_ref, ckv_ref, kr_ref, pos_ref, qnw_ref, kvnw_ref, wuqT_ref, wuk_ref, wuvT_ref, inv_ref,
                     qT_ref, k_ref, vT_ref):
    scale = (MLA_NOPE + MLA_ROPE) ** -0.5 * LOG2E
    qn = _rms(cq_ref[...], qnw_ref[...]).astype(BF16)
    qT = lax.dot_general(wuqT_ref[...], qn, NT_DIMS, preferred_element_type=F32)
    kvn = _rms(ckv_ref[...], kvnw_ref[...]).astype(BF16)
    kn = jnp.dot(kvn, wuk_ref[...], preferred_element_type=F32)
    vT = lax.dot_general(wuvT_ref[...], kvn, NT_DIMS, preferred_element_type=F32)

    ang = pos_ref[...].astype(F32) * inv_ref[...]
    c = jnp.cos(ang)
    s = jnp.sin(ang)
    lane = lax.broadcasted_iota(jnp.int32, ang.shape, 1)
    coef = jnp.where(lane < 64, c, jnp.where(lane < 96, -s, s))
    coefT = coef.T

    ones_row = jnp.where(lax.broadcasted_iota(jnp.int32, (MLA_VROWS - MLA_V, coefT.shape[1]), 0) == 0, 1.0, 0.0)
    t = kr_ref[...] * coef
    kpe = jnp.where(lane < 64, t + pltpu.roll(t, 64, 1), 0.0)
    for h in range(MLA_HEADS):
        base = h * MLA_QK_PAD
        tT = qT[base + 128:base + 256, :] * coefT
        ropeT = tT + pltpu.roll(tT, 64, 0)
        qT_ref[h] = (jnp.concatenate([qT[base:base + 128, :], ropeT], axis=0) * scale).astype(BF16)
        k_ref[h] = jnp.concatenate([kn[:, h * 128:(h + 1) * 128], kpe], axis=-1).astype(BF16)
        vT_ref[h] = jnp.concatenate([vT[h * 128:(h + 1) * 128, :], ones_row], axis=0).astype(BF16)


def _mla_prep(pm, pos2, qnw, kvnw, wuqT, wuk, wuvT, inv128, tm=256):
    T = pm.shape[0]
    H = MLA_HEADS
    fixed = lambda i: (0, 0)
    return pl.pallas_call(
        _mla_prep_kernel,
        out_shape=(
            jax.ShapeDtypeStruct((H, MLA_QK_PAD, T), BF16),
            jax.ShapeDtypeStruct((H, T, MLA_QK_PAD), BF16),
            jax.ShapeDtypeStruct((H, MLA_VROWS, T), BF16),
        ),
        grid=(T // tm,),
        in_specs=[
            pl.BlockSpec((tm, MLA_Q_RANK), lambda i: (i, SEC_CQ // MLA_Q_RANK)),
            pl.BlockSpec((tm, MLA_KV_RANK), lambda i: (i, SEC_CKV // MLA_KV_RANK)),
            pl.BlockSpec((tm, LANES), lambda i: (i, SEC_ROPE // LANES)),
            pl.BlockSpec((tm, 1), lambda i: (i, 0)),
            pl.BlockSpec((1, MLA_Q_RANK), fixed),
            pl.BlockSpec((1, MLA_KV_RANK), fixed),
            pl.BlockSpec((H * MLA_QK_PAD, MLA_Q_RANK), fixed),
            pl.BlockSpec((MLA_KV_RANK, H * MLA_NOPE), fixed),
            pl.BlockSpec((H * MLA_V, MLA_KV_RANK), fixed),
            pl.BlockSpec((1, LANES), fixed),
        ],
        out_specs=(
            pl.BlockSpec((H, MLA_QK_PAD, tm), lambda i: (0, 0, i)),
            pl.BlockSpec((H, tm, MLA_QK_PAD), lambda i: (0, i, 0)),
            pl.BlockSpec((H, MLA_VROWS, tm), lambda i: (0, 0, i)),
        ),
        compiler_params=_cparams(("parallel",)),
        name="mla_prep",
    )(pm, pm, pm, pos2, qnw.reshape(1, -1), kvnw.reshape(1, -1), wuqT, wuk, wuvT, inv128)


def _mla_attn_kernel(qi_ref, ki_ref, qT_ref, k_ref, vT_ref, o_ref, m_scr, acc_scr,
                     s0_scr, s1_scr, top0_scr, top1_scr, p0_scr, p1_scr, al0_scr, al1_scr):
    p_id = pl.program_id(0)
    qi = qi_ref[p_id]
    ki = ki_ref[p_id]
    tk, tq = s0_scr.shape
    s_slots, top_slots = (s0_scr, s1_scr), (top0_scr, top1_scr)
    p_slots, al_slots = (p0_scr, p1_scr), (al0_scr, al1_scr)

    @pl.when(ki == 0)
    def _():
        m_scr[...] = jnp.full(m_scr.shape, NEG, F32)
        acc_scr[...] = jnp.zeros(acc_scr.shape, F32)

    def run(diag):
        def scores(h, slot):
            s = jnp.dot(k_ref[h], qT_ref[h], preferred_element_type=F32)
            if diag:
                key = lax.broadcasted_iota(jnp.int32, s.shape, 0)
                qry = lax.broadcasted_iota(jnp.int32, s.shape, 1)
                s = jnp.where(key <= qry, s, NEG)
            s_slots[slot][...] = s
            top_slots[slot][...] = jnp.max(s.reshape(tk // SUBLANES, SUBLANES, tq), axis=0)

        def softmax(h, slot):
            m_prev = m_scr[h]
            m_new = jnp.maximum(m_prev, jnp.max(top_slots[slot][...], axis=0, keepdims=True))
            for b in range(tk // BF16_ROWS):
                rows = slice(b * BF16_ROWS, (b + 1) * BF16_ROWS)
                p_slots[slot][rows, :] = jnp.exp2(s_slots[slot][rows, :] - m_new).astype(BF16)
            al_slots[slot][...] = jnp.exp2(m_prev - m_new)
            m_scr[h] = m_new

        def values(h, slot):
            acc = al_slots[slot][...] * acc_scr[h] + jnp.dot(vT_ref[h], p_slots[slot][...],
                                                             preferred_element_type=F32)
            if diag:
                o_ref[pl.ds(pl.multiple_of(h * MLA_V, MLA_V), MLA_V), :] = (
                    acc[:MLA_V, :] * (1.0 / acc[MLA_V:MLA_V + 1, :]))
            else:
                acc_scr[h] = acc

        scores(0, 0)
        scores(1, 1)
        softmax(0, 0)

        def body(hh, carry):
            h = 2 * hh
            scores(h, 0)
            softmax(h - 1, 1)
            values(h - 2, 0)
            scores(h + 1, 1)
            softmax(h, 0)
            values(h - 1, 1)
            return carry

        lax.fori_loop(1, MLA_HEADS // 2, body, 0)
        softmax(MLA_HEADS - 1, 1)
        values(MLA_HEADS - 2, 0)
        values(MLA_HEADS - 1, 1)

    @pl.when(ki < qi)
    def _():
        run(False)

    @pl.when(ki == qi)
    def _():
        run(True)


def _tri_pairs(nq, per_q):
    qi, ki = [], []
    for i in range(nq):
        for c in range(per_q(i) + 1):
            qi.append(i)
            ki.append(c)
    return jnp.asarray(qi, jnp.int32), jnp.asarray(ki, jnp.int32)


def _mla_attn(qT, k, vT):
    H, T, _ = k.shape
    tq = min(MLA_TQ, T)
    nq = T // tq
    qi, ki = _tri_pairs(nq, lambda i: i)
    grid_spec = pltpu.PrefetchScalarGridSpec(
        num_scalar_prefetch=2,
        grid=(int(qi.shape[0]),),
        in_specs=[
            pl.BlockSpec((H, MLA_QK_PAD, tq), lambda p, qi, ki: (0, 0, qi[p])),
            pl.BlockSpec((H, tq, MLA_QK_PAD), lambda p, qi, ki: (0, ki[p], 0)),
            pl.BlockSpec((H, MLA_VROWS, tq), lambda p, qi, ki: (0, 0, ki[p])),
        ],
        out_specs=pl.BlockSpec((H * MLA_V, tq), lambda p, qi, ki: (0, qi[p])),
        scratch_shapes=[
            pltpu.VMEM((H, 1, tq), F32),
            pltpu.VMEM((H, MLA_VROWS, tq), F32),
            pltpu.VMEM((tq, tq), F32),
            pltpu.VMEM((tq, tq), F32),
            pltpu.VMEM((SUBLANES, tq), F32),
            pltpu.VMEM((SUBLANES, tq), F32),
            pltpu.VMEM((tq, tq), BF16),
            pltpu.VMEM((tq, tq), BF16),
            pltpu.VMEM((1, tq), F32),
            pltpu.VMEM((1, tq), F32),
        ],
    )
    return pl.pallas_call(
        _mla_attn_kernel,
        out_shape=jax.ShapeDtypeStruct((H * MLA_V, T), F32),
        grid_spec=grid_spec,
        compiler_params=_cparams(("arbitrary",)),
        name="mla_attn",
    )(qi, ki, qT, k, vT)


def _gelu_tanh(x):
    return 0.5 * x * (1.0 + jnp.tanh(np.sqrt(2.0 / np.pi).astype(np.float32) * (x + 0.044715 * (x * x * x))))


def _compress_kernel(r_ref, pos_ref, w1_ref, w2_ref, kc_ref, vcT_ref):
    r = r_ref[0]
    nr = r.shape[0]
    half = CMP_STRIDE * NSA_D
    hp = lax.Precision.HIGHEST
    a = jnp.dot(r + pos_ref[0, 0:1, :], w1_ref[0, :half, :], precision=hp, preferred_element_type=F32)
    b = jnp.dot(r + pos_ref[0, 1:2, :], w1_ref[0, half:, :], precision=hp, preferred_element_type=F32)
    hid = _gelu_tanh(a + pltpu.roll(b, nr - 1, 0))
    out = jnp.dot(hid, w2_ref[0], precision=hp, preferred_element_type=F32)
    row = lax.broadcasted_iota(jnp.int32, out.shape, 0)
    out = jnp.where(row < nr - 1, out, 0.0)
    is_key = pl.program_id(0) % 2 == 0

    @pl.when(is_key)
    def _():
        kc_ref[0] = jnp.concatenate([out, _digit_cols(row, CMP_STRIDE)], axis=-1).astype(BF16)

    @pl.when(jnp.logical_not(is_key))
    def _():
        vcT_ref[0] = jnp.concatenate([out, jnp.zeros_like(out)], axis=-1).T[0:NSA_D, :].astype(BF16)


def _compress(r4, pos_kv, w1_kv, w2_kv):
    _, nr, width = r4.shape
    return pl.pallas_call(
        _compress_kernel,
        out_shape=(
            jax.ShapeDtypeStruct((NSA_G, nr, 2 * NSA_D), BF16),
            jax.ShapeDtypeStruct((NSA_G, NSA_D, nr), BF16),
        ),
        grid=(2 * NSA_G,),
        in_specs=[
            pl.BlockSpec((1, nr, width), lambda a: (a, 0, 0)),
            pl.BlockSpec((1, 2, width), lambda a: (a % 2, 0, 0)),
            pl.BlockSpec((1, 2 * width, CMP_HIDDEN), lambda a: (a % 2, 0, 0)),
            pl.BlockSpec((1, CMP_HIDDEN, NSA_D), lambda a: (a % 2, 0, 0)),
        ],
        out_specs=(
            pl.BlockSpec((1, nr, 2 * NSA_D), lambda a: (a // 2, 0, 0)),
            pl.BlockSpec((1, NSA_D, nr), lambda a: (a // 2, 0, 0)),
        ),
        compiler_params=_cparams(("arbitrary",)),
        name="nsa_compress",
    )(r4, pos_kv, w1_kv, w2_kv)


def _alibi_rows(tq):
    start = np.float32(2.0 ** (-8.0 / NSA_HEADS))
    slopes = (start ** np.arange(1, NSA_HEADS + 1, dtype=np.float32)).astype(np.float32)
    slopes = (slopes.astype(np.float64) * LOG2E).astype(np.float32)
    parts, rest = [], slopes
    for _ in range(ALIBI_PARTS):
        piece = (rest.view(np.uint32) & np.uint32(0xFFFF0000)).view(np.float32)
        parts.append(piece)
        rest = rest - piece
    assert not rest.any()
    pieces = np.stack(parts)
    rows = np.concatenate([pieces * ALIBI_HI, pieces, np.zeros((NSA_D - 2 * ALIBI_PARTS, NSA_HEADS), np.float32)])
    rows = np.repeat(rows.reshape(NSA_D, NSA_G, NSA_HPG), tq, axis=2).transpose(1, 0, 2)
    srow = np.repeat(slopes.reshape(NSA_G, 1, NSA_HPG), tq, axis=2)
    return jnp.asarray(rows, BF16), jnp.asarray(srow, F32)


def _nsa_queries(qT_ref, arow_ref):
    q = qT_ref[...] * (NSA_D ** -0.5 * LOG2E)
    top = jnp.concatenate([q[h * NSA_D:(h + 1) * NSA_D, :] for h in range(NSA_HPG)], axis=1).astype(BF16)
    return jnp.concatenate([top, arow_ref[0]], axis=0)


def _heads_to_rows(o):
    tq = o.shape[1] // NSA_HPG
    return jnp.concatenate([o[:, h * tq:(h + 1) * tq] for h in range(NSA_HPG)], axis=0)


def _token_pos(i, tq):
    return i * tq + jnp.bitwise_and(lax.broadcasted_iota(jnp.int32, (1, NSA_HPG * tq), 1), tq - 1)


def _bf16_pieces(x, n):
    pieces = []
    for _ in range(n - 1):
        top = pltpu.bitcast(jnp.bitwise_and(pltpu.bitcast(x, jnp.uint32), jnp.uint32(0xFFFF0000)), F32)
        pieces.append(top.astype(BF16))
        x = x - top
    return pieces + [x.astype(BF16)]


def _cmp_attn_kernel(qT_ref, arow_ref, kc_ref, vcT_ref, ov_ref, o_ref, selT_ref, *, nsel_pad):
    i = pl.program_id(1)
    tq = NSA_TQ
    n = NSA_HPG * tq
    nr = kc_ref.shape[1]
    s = jnp.dot(kc_ref[0], _nsa_queries(qT_ref, arow_ref), preferred_element_type=F32)
    tpos = _token_pos(i, tq)
    last_done = jnp.right_shift(tpos - (CMP_LEN - 1), 4)
    valid = lax.broadcasted_iota(jnp.int32, (nr, n), 0) <= last_done
    s = jnp.where(valid, s, NEG)
    e = jnp.where(valid, jnp.exp2(s - jnp.max(s, axis=0, keepdims=True)), 0.0)
    l = jnp.sum(e, axis=0, keepdims=True)
    inv = 1.0 / jnp.where(l > 0.0, l, 1.0)
    acc = jnp.dot(vcT_ref[0], e.astype(BF16), preferred_element_type=F32)
    o_ref[...] = _heads_to_rows(acc * inv)

    pn = e * inv
    psum = pn[:, 0:tq]
    for h in range(1, NSA_HPG):
        psum = psum + pn[:, h * tq:(h + 1) * tq]
    imp = sum(jnp.dot(ov_ref[...], piece, preferred_element_type=F32) for piece in _bf16_pieces(psum, 3))

    blk = lax.broadcasted_iota(jnp.int32, (nsel_pad, tq), 0)
    cur = jnp.right_shift(i * tq + lax.broadcasted_iota(jnp.int32, (nsel_pad, tq), 1), SEL_SHIFT)
    forced = jnp.logical_or(blk == 0, jnp.logical_or(blk == cur, blk == cur - 1))
    score = jnp.where(blk <= cur, imp + jnp.where(forced, FORCE_BONUS, 0.0), NEG)
    blk_f = blk.astype(F32)

    def pick(_, carry):
        sc, sel = carry
        mx = jnp.max(sc, axis=0, keepdims=True)
        first = jnp.min(jnp.where(sc == mx, blk_f, float(nsel_pad)), axis=0, keepdims=True)
        hit = blk_f == first
        sel = jnp.where(jnp.logical_and(hit, mx > NEG * 0.5), 1.0, sel)
        return jnp.where(hit, REMOVED, sc), sel

    _, sel = lax.fori_loop(0, SEL_TOPK, pick, (score, jnp.zeros((nsel_pad, tq), F32)))
    selT_ref[0] = sel


def _overlap_matrix(nsel_pad, nr):
    j = np.arange(nsel_pad)[:, None]
    n = np.arange(nr)[None, :]
    ov = (n * CMP_STRIDE <= j * SEL_LEN + (SEL_LEN - 1)) & (n * CMP_STRIDE + (CMP_LEN - 1) >= j * SEL_LEN) & (n < nr - 1)
    return jnp.asarray(ov, BF16)


def _cmp_attn(fm, arows, kc, vcT, nsel_pad):
    T = fm.shape[1]
    nr = kc.shape[1]
    tq = NSA_TQ
    n = NSA_HPG * tq
    return pl.pallas_call(
        functools.partial(_cmp_attn_kernel, nsel_pad=nsel_pad),
        out_shape=(
            jax.ShapeDtypeStruct((NSA_HEADS * NSA_D, T), F32),
            jax.ShapeDtypeStruct((NSA_G, nsel_pad, T), F32),
        ),
        grid=(NSA_G, T // tq),
        in_specs=[
            pl.BlockSpec((NSA_HPG * NSA_D, tq), lambda g, i: (g, i)),
            pl.BlockSpec((1, NSA_D, n), lambda g, i: (g, 0, 0)),
            pl.BlockSpec((1, nr, 2 * NSA_D), lambda g, i: (g, 0, 0)),
            pl.BlockSpec((1, NSA_D, nr), lambda g, i: (g, 0, 0)),
            pl.BlockSpec((nsel_pad, nr), lambda g, i: (0, 0)),
        ],
        out_specs=(
            pl.BlockSpec((NSA_HPG * NSA_D, tq), lambda g, i: (g, i)),
            pl.BlockSpec((1, nsel_pad, tq), lambda g, i: (g, 0, i)),
        ),
        compiler_params=_cparams(("parallel", "parallel")),
        name="nsa_cmp_attn",
    )(fm, arows, kc, vcT, _overlap_matrix(nsel_pad, nr))


def _sel_attn_kernel(qT_ref, selT_ref, arow_ref, srow_ref, ka_ref, va_ref, o_ref,
                     qa_scr, s0_scr, s1_scr, top0_scr, top1_scr, p0_scr, p1_scr, al0_scr, al1_scr, m_scr, acc_scr,
                     *, tk, n_chunks):
    i = pl.program_id(1)
    tq = SEL_TQ
    n = NSA_HPG * tq
    nsel_pad = selT_ref.shape[1]
    qa_scr[0:2 * NSA_D, :] = _nsa_queries(qT_ref, arow_ref)
    blk = lax.broadcasted_iota(jnp.int32, (nsel_pad, tq), 0)
    own = blk // (tq // SEL_LEN) == i
    chosen = jnp.logical_and(selT_ref[0] > 0.5, jnp.logical_not(own))
    qa_scr[2 * NSA_D:, :] = jnp.concatenate([jnp.where(chosen, 0.0, NEG_BF16).astype(BF16)] * NSA_HPG, axis=1)

    tpos = _token_pos(i, tq)
    slope_row = srow_ref[0]

    def rterm(c):
        return slope_row * (c * tk - tpos).astype(F32)

    c_own = (i * tq) // tk
    own_start = pl.multiple_of(i * tq, tq)
    s = jnp.dot(ka_ref[0, pl.ds(own_start, tq), 0:2 * NSA_D], qa_scr[0:2 * NSA_D, :], preferred_element_type=F32)
    kpos = i * tq + lax.broadcasted_iota(jnp.int32, s.shape, 0)
    s = jnp.where(kpos <= tpos, s, NEG)
    r_own = rterm(c_own)
    m0 = jnp.max(s, axis=0, keepdims=True) + r_own
    p = jnp.exp2(s - (m0 - r_own)).astype(BF16)
    v_own = va_ref[0, c_own, :, pl.ds(pl.multiple_of(i * tq - c_own * tk, tq), tq)]
    acc_scr[...] = jnp.dot(v_own, p, preferred_element_type=F32)
    m_scr[...] = m0

    s_slots = (s0_scr, s1_scr)
    p_slots = (p0_scr, p1_scr)
    al_slots = (al0_scr, al1_scr)

    top_slots = (top0_scr, top1_scr)

    def scores(j, slot):
        start = pl.multiple_of(j * tk, tk)
        s = jnp.dot(ka_ref[0, pl.ds(start, tk), :], qa_scr[...], preferred_element_type=F32)
        s_slots[slot][...] = s
        top_slots[slot][...] = jnp.max(s.reshape(tk // SUBLANES, SUBLANES, n), axis=0)

    def softmax(j, slot):
        r = rterm(j)
        m_prev = m_scr[...]
        s_ref, p_ref = s_slots[slot], p_slots[slot]
        m_new = jnp.maximum(m_prev, jnp.max(top_slots[slot][...], axis=0, keepdims=True) + r)
        shift = m_new - r
        for b in range(tk // BF16_ROWS):
            rows = slice(b * BF16_ROWS, (b + 1) * BF16_ROWS)
            p_ref[rows, :] = jnp.exp2(s_ref[rows, :] - shift).astype(BF16)
        al_slots[slot][...] = jnp.exp2(m_prev - m_new)
        m_scr[...] = m_new

    def values(j, slot):
        acc_scr[...] = (al_slots[slot][...] * acc_scr[...]
                        + jnp.dot(va_ref[0, j], p_slots[slot][...], preferred_element_type=F32))

    n_pairs = jnp.minimum(((i * tq + tq - 1) // tk + 2) // 2, n_chunks // 2)
    scores(0, 0)
    scores(1, 1)
    softmax(0, 0)

    def body(jj, carry):
        j = 2 * jj
        scores(j, 0)
        softmax(j - 1, 1)
        values(j - 2, 0)
        scores(j + 1, 1)
        softmax(j, 0)
        values(j - 1, 1)
        return carry

    lax.fori_loop(1, n_pairs, body, 0)
    softmax(2 * n_pairs - 1, 1)
    values(2 * n_pairs - 2, 0)
    values(2 * n_pairs - 1, 1)
    acc = acc_scr[...]
    o_ref[...] = _heads_to_rows(acc[0:NSA_D, :] * (1.0 / acc[NSA_D:NSA_D + 1, :]))


def _sel_attn(fm, selT, arows, srow, ka, va):
    _, T, kw = ka.shape
    tk = va.shape[-1]
    tq = min(SEL_TQ, T)
    nsel_pad = selT.shape[1]
    n = NSA_HPG * tq
    return pl.pallas_call(
        functools.partial(_sel_attn_kernel, tk=tk, n_chunks=T // tk),
        out_shape=jax.ShapeDtypeStruct((NSA_HEADS * NSA_D, T), F32),
        grid=(NSA_G, T // tq),
        in_specs=[
            pl.BlockSpec((NSA_HPG * NSA_D, tq), lambda g, i: (g, i)),
            pl.BlockSpec((1, nsel_pad, tq), lambda g, i: (g, 0, i)),
            pl.BlockSpec((1, NSA_D, n), lambda g, i: (g, 0, 0)),
            pl.BlockSpec((1, 1, n), lambda g, i: (g, 0, 0)),
            pl.BlockSpec((1, T, kw), lambda g, i: (g, 0, 0)),
            pl.BlockSpec((1, T // tk, SEL_VROWS, tk), lambda g, i: (g, 0, 0, 0)),
        ],
        out_specs=pl.BlockSpec((NSA_HPG * NSA_D, tq), lambda g, i: (g, i)),
        scratch_shapes=[
            pltpu.VMEM((kw, n), BF16),
            pltpu.VMEM((tk, n), F32),
            pltpu.VMEM((tk, n), F32),
            pltpu.VMEM((SUBLANES, n), F32),
            pltpu.VMEM((SUBLANES, n), F32),
            pltpu.VMEM((tk, n), BF16),
            pltpu.VMEM((tk, n), BF16),
            pltpu.VMEM((1, n), F32),
            pltpu.VMEM((1, n), F32),
            pltpu.VMEM((1, n), F32),
            pltpu.VMEM((SEL_VROWS, n), F32),
        ],
        compiler_params=_cparams(("parallel", "arbitrary")),
        name="nsa_sel_attn",
    )(fm, selT, arows, srow, ka, va)


WIN_KEYS = WINDOW + NSA_TQ


def _win_attn_kernel(qT_ref, arow_ref, srow_ref, ka_ref, va_ref, o_ref):
    i = pl.program_id(1)
    tq = NSA_TQ
    n = NSA_HPG * tq
    start = pl.multiple_of(jnp.maximum(i * tq - WINDOW, 0), tq)
    s = jnp.dot(ka_ref[0, pl.ds(start, WIN_KEYS), :], _nsa_queries(qT_ref, arow_ref), preferred_element_type=F32)
    tpos = _token_pos(i, tq)
    kpos = start + lax.broadcasted_iota(jnp.int32, s.shape, 0)
    allowed = jnp.logical_and(kpos <= tpos, kpos > tpos - WINDOW)
    s = jnp.where(allowed, s, NEG)
    slope_row = srow_ref[0]
    tiles = [s[j * tq:(j + 1) * tq, :] for j in range(WIN_KEYS // tq)]
    rterms = [slope_row * (start + j * tq - tpos).astype(F32) for j in range(WIN_KEYS // tq)]
    m = jnp.max(tiles[0], axis=0, keepdims=True) + rterms[0]
    for sj, rj in zip(tiles[1:], rterms[1:]):
        m = jnp.maximum(m, jnp.max(sj, axis=0, keepdims=True) + rj)
    p = jnp.concatenate([jnp.exp2(sj - (m - rj)) for sj, rj in zip(tiles, rterms)], axis=0).astype(BF16)
    acc = jnp.dot(va_ref[0, :, pl.ds(start, WIN_KEYS)], p, preferred_element_type=F32)
    o_ref[...] = _heads_to_rows(acc[0:NSA_D, :] * (1.0 / acc[NSA_D:NSA_D + 1, :]))


def _win_attn(fm, arows, srow, ka, va):
    _, T, kw = ka.shape
    assert T >= WIN_KEYS
    tq = NSA_TQ
    n = NSA_HPG * tq
    return pl.pallas_call(
        _win_attn_kernel,
        out_shape=jax.ShapeDtypeStruct((NSA_HEADS * NSA_D, T), F32),
        grid=(NSA_G, T // tq),
        in_specs=[
            pl.BlockSpec((NSA_HPG * NSA_D, tq), lambda g, i: (g, i)),
            pl.BlockSpec((1, NSA_D, n), lambda g, i: (g, 0, 0)),
            pl.BlockSpec((1, 1, n), lambda g, i: (g, 0, 0)),
            pl.BlockSpec((1, T, kw), lambda g, i: (g, 0, 0)),
            pl.BlockSpec((1, SEL_VROWS, T), lambda g, i: (g, 0, 0)),
        ],
        out_specs=pl.BlockSpec((NSA_HPG * NSA_D, tq), lambda g, i: (g, i)),
        compiler_params=_cparams(("parallel", "parallel")),
        name="nsa_win_attn",
    )(fm, arows, srow, ka, va)


def _out_proj_kernel(x_ref, ymT_ref, ocT_ref, osT_ref, owT_ref, glT_ref, mnw_ref, nnw_ref, wo_ref, o_ref):
    gates = 1.0 / (1.0 + jnp.exp(-glT_ref[...]))
    heads = []
    for h in range(NSA_HEADS):
        rows = slice(h * NSA_D, (h + 1) * NSA_D)
        heads.append(gates[3 * h:3 * h + 1, :] * ocT_ref[rows, :]
                     + gates[3 * h + 1:3 * h + 2, :] * osT_ref[rows, :]
                     + gates[3 * h + 2:3 * h + 3, :] * owT_ref[rows, :])
    ynT = jnp.concatenate(heads, axis=0)

    def rms_cols(y, w_col):
        return (y * lax.rsqrt(jnp.mean(y * y, axis=0, keepdims=True) + EPS) * w_col).astype(BF16)

    half = MLA_HEADS * MLA_V
    o_ref[...] = (x_ref[...]
                  + lax.dot_general(rms_cols(ymT_ref[...], mnw_ref[...]), wo_ref[:half, :], TN_DIMS,
                                    preferred_element_type=F32)
                  + lax.dot_general(rms_cols(ynT, nnw_ref[...]), wo_ref[half:, :], TN_DIMS,
                                    preferred_element_type=F32))


def _out_proj(x2, y_mlaT, o_cT, o_sT, o_wT, glT, mnw, nnw, wo, tm=512):
    T = x2.shape[0]
    half = MLA_HEADS * MLA_V
    row = lambda i: (i, 0)
    col = lambda i: (0, i)
    fixed2 = lambda i: (0, 0)
    return pl.pallas_call(
        _out_proj_kernel,
        out_shape=jax.ShapeDtypeStruct((T, D_MODEL), F32),
        grid=(T // tm,),
        in_specs=[
            pl.BlockSpec((tm, D_MODEL), row),
            pl.BlockSpec((half, tm), col),
            pl.BlockSpec((half, tm), col),
            pl.BlockSpec((half, tm), col),
            pl.BlockSpec((half, tm), col),
            pl.BlockSpec((LANES, tm), col),
            pl.BlockSpec((half, 1), fixed2),
            pl.BlockSpec((half, 1), fixed2),
            pl.BlockSpec((2 * half, D_MODEL), fixed2),
        ],
        out_specs=pl.BlockSpec((tm, D_MODEL), row),
        compiler_params=_cparams(("parallel",)),
        name="out_proj",
    )(x2, y_mlaT, o_cT, o_sT, o_wT, glT, mnw.reshape(-1, 1), nnw.reshape(-1, 1), wo)


def _mlp_kernel(h_ref, nw_ref, wu_ref, wd_ref, fw_ref, o_ref, n_scr, acc_scr, *, final):
    c = pl.program_id(1)

    @pl.when(c == 0)
    def _():
        n_scr[...] = _rms(h_ref[...], nw_ref[...]).astype(BF16)
        acc_scr[...] = jnp.zeros(acc_scr.shape, F32)

    u = jnp.dot(n_scr[...], wu_ref[...], preferred_element_type=F32)
    a = jnp.square(jnp.maximum(u, 0.0)).astype(BF16)
    acc_scr[...] += jnp.dot(a, wd_ref[...], preferred_element_type=F32)

    @pl.when(c == pl.num_programs(1) - 1)
    def _():
        h2 = h_ref[...] + acc_scr[...]
        o_ref[...] = _rms(h2, fw_ref[...]) if final else h2


def _mlp(h1, nw, wu, wd, fw, final, tm=512, tc=1024):
    T = h1.shape[0]
    tm = min(tm, T)
    return pl.pallas_call(
        functools.partial(_mlp_kernel, final=final),
        out_shape=jax.ShapeDtypeStruct((T, D_MODEL), F32),
        grid=(T // tm, MLP_HIDDEN // tc),
        in_specs=[
            pl.BlockSpec((tm, D_MODEL), lambda i, c: (i, 0)),
            pl.BlockSpec((1, D_MODEL), lambda i, c: (0, 0)),
            pl.BlockSpec((D_MODEL, tc), lambda i, c: (0, c)),
            pl.BlockSpec((tc, D_MODEL), lambda i, c: (c, 0)),
            pl.BlockSpec((1, D_MODEL), lambda i, c: (0, 0)),
        ],
        out_specs=pl.BlockSpec((tm, D_MODEL), lambda i, c: (i, 0)),
        scratch_shapes=[pltpu.VMEM((tm, D_MODEL), BF16), pltpu.VMEM((tm, D_MODEL), F32)],
        compiler_params=_cparams(("parallel", "arbitrary")),
        name="mlp",
    )(h1, nw.reshape(1, -1), wu, wd, fw.reshape(1, -1))


def _regroup_w_in(w_in):
    cq = w_in[:, 0:512]
    ckv = w_in[:, 512:768]
    kr = w_in[:, 768:832]
    q = w_in[:, 832:1856]
    kv_cmp = w_in[:, 1856:2112]
    kv_slc = w_in[:, 2112:2368].reshape(-1, NSA_G, 2, NSA_D)
    kv_win = w_in[:, 2368:2624].reshape(-1, NSA_G, 2, NSA_D)
    gates = w_in[:, 2624:2672]
    x1, x2 = kr[:, :32], kr[:, 32:]
    flat = lambda t: t.reshape(t.shape[0], NSA_G * NSA_D)
    w_tok = jnp.concatenate([cq, ckv, x1, x2, x2, x1, kv_cmp, flat(kv_slc[:, :, 0]), flat(kv_win[:, :, 0])], axis=1)
    zeros = jnp.zeros((w_in.shape[0], FM_ROWS - ROW_GATE - gates.shape[1]), w_in.dtype)
    w_fea = jnp.concatenate([q, flat(kv_slc[:, :, 1]), flat(kv_win[:, :, 1]), gates, zeros], axis=1)
    return w_tok.astype(BF16), w_fea.T.astype(BF16)


def _regroup_w_uq(w_uq):
    w = w_uq.reshape(MLA_Q_RANK, MLA_HEADS, MLA_NOPE + MLA_ROPE)
    nope = w[:, :, :MLA_NOPE]
    x1 = w[:, :, MLA_NOPE:MLA_NOPE + 32]
    x2 = w[:, :, MLA_NOPE + 32:]
    return jnp.concatenate([nope, x1, x2, x2, x1], axis=-1).reshape(MLA_Q_RANK, MLA_HEADS * MLA_QK_PAD).astype(BF16)


def kernel(x, positions, attn_norm_w, w_in, mla_q_norm_w, mla_w_uq, mla_kv_norm_w, mla_w_ukv, cmp_pos_k,
           cmp_pos_v, cmp_w1_k, cmp_w2_k, cmp_w1_v, cmp_w2_v, mla_out_norm_w, nsa_out_norm_w, w_o, mlp_norm_w,
           w_up, w_down, final_norm_w):
    B, T, _ = x.shape
    depth = w_in.shape[0]
    inv = 1.0 / (ROPE_THETA ** (jnp.arange(0, MLA_ROPE, 2, dtype=F32) / MLA_ROPE))
    inv128 = jnp.tile(inv, 4).reshape(1, LANES)
    arows, srow = _alibi_rows(NSA_TQ)
    arows_sel, srow_sel = _alibi_rows(SEL_TQ)
    nsel_pad = -(-(T // SEL_LEN) // LANES) * LANES

    outs = []
    for b in range(B):
        h = x[b]
        pos2 = positions[b].reshape(T, 1)
        for l in range(depth):
            w_tok, w_feaT = _regroup_w_in(w_in[l])
            pm, r4, ka_s, ka_w, qT_nsa, va_s, va_w, glT = _in_proj(h, attn_norm_w[l], w_tok, w_feaT, nsel_pad)
            wukv = mla_w_ukv[l].reshape(MLA_KV_RANK, MLA_HEADS, MLA_NOPE + MLA_V)
            wuk = wukv[:, :, :MLA_NOPE].reshape(MLA_KV_RANK, MLA_HEADS * MLA_NOPE).astype(BF16)
            wuvT = wukv[:, :, MLA_NOPE:].reshape(MLA_KV_RANK, MLA_HEADS * MLA_V).T.astype(BF16)
            qT, k, vT = _mla_prep(pm, pos2, mla_q_norm_w[l], mla_kv_norm_w[l], _regroup_w_uq(mla_w_uq[l]).T,
                                  wuk, wuvT, inv128)
            y_mla = _mla_attn(qT, k, vT)

            half = CMP_STRIDE * NSA_D
            pos_kv = jnp.stack([cmp_pos_k[l].reshape(2, half), cmp_pos_v[l].reshape(2, half)])
            kc, vcT = _compress(r4.reshape(2 * NSA_G, T // CMP_STRIDE, half), pos_kv,
                                jnp.stack([cmp_w1_k[l], cmp_w1_v[l]]), jnp.stack([cmp_w2_k[l], cmp_w2_v[l]]))
            o_cT, selT = _cmp_attn(qT_nsa, arows, kc, vcT, nsel_pad)
            o_sT = _sel_attn(qT_nsa, selT, arows_sel, srow_sel, ka_s, va_s)
            o_wT = _win_attn(qT_nsa, arows, srow, ka_w, va_w)

            h = _out_proj(h, y_mla, o_cT, o_sT, o_wT, glT, mla_out_norm_w[l], nsa_out_norm_w[l],
                          w_o[l].astype(BF16))
            h = _mlp(h, mlp_norm_w[l], w_up[l].astype(BF16), w_down[l].astype(BF16), final_norm_w,
                     final=(l == depth - 1))
        outs.append(h)
    return jnp.stack(outs)
```

```python
import functools

import numpy as np
import jax
import jax.numpy as jnp
from jax import lax
from jax.experimental import pallas as pl
from jax.experimental.pallas import tpu as pltpu

F32 = jnp.float32
BF16 = jnp.bfloat16

EPS = 1e-6
NEG = -1e30
REMOVED = -3e38

D_MODEL = 2048
MLA_HEADS = 8
MLA_Q_RANK = 512
MLA_KV_RANK = 256
MLA_NOPE = 128
MLA_ROPE = 64
MLA_V = 128
ROPE_THETA = 10000.0
MLA_QK_PAD = 256
MLA_VROWS = 144

NSA_HEADS = 16
NSA_G = 2
NSA_HPG = 8
NSA_D = 64
CMP_LEN = 32
CMP_STRIDE = 16
CMP_HIDDEN = 256
SEL_LEN = 64
SEL_SHIFT = 6
SEL_TOPK = 16
WINDOW = 512
FORCE_BONUS = 1e4
MLP_HIDDEN = 4 * D_MODEL

LOG2E = 1.4426950408889634
LANES = 128
SUBLANES = 8
BF16_ROWS = 16
NSA_TQ = 128
SEL_TQ = 256
SEL_TK = 512
MLA_TQ = 1024
MLA_TK = 512
VMEM_LIMIT = 56 * 1024 * 1024

SEC_CQ = 0
SEC_CKV = 512
SEC_ROPE = 768
SEC_CMP = 896
SEC_KS = 1152
SEC_KW = 1280
TM_COLS = 1408
MLA_COLS = SEC_CMP
ROW_Q = 0
ROW_VS = 1024
ROW_VW = 1152
ROW_GATE = 1280
FM_ROWS = 1408


def _cparams(sem):
    return pltpu.CompilerParams(dimension_semantics=sem, vmem_limit_bytes=VMEM_LIMIT)


def _rms(x, w):
    return x * lax.rsqrt(jnp.mean(x * x, axis=-1, keepdims=True) + EPS) * w


ALIBI_PARTS = 4
ALIBI_HI = 16
NEG_BF16 = -1e30
SEL_VROWS = 80
NT_DIMS = (((1,), (1,)), ((), ()))
TN_DIMS = (((0,), (0,)), ((), ()))


def _digit_cols(offset, scale):
    lane = lax.broadcasted_iota(jnp.int32, offset.shape, 1)
    hi = jnp.right_shift(offset, 4) * scale
    lo = jnp.bitwise_and(offset, ALIBI_HI - 1) * scale
    return jnp.where(lane < ALIBI_PARTS, hi, jnp.where(lane < 2 * ALIBI_PARTS, lo, 0)).astype(F32)


def _in_proj_kernel(x_ref, nw_ref, w_ref, wT_ref, pm_ref, r_ref, kas_ref, kaw_ref, qT_ref, vas_ref, vaw_ref,
                    glT_ref, *, nsel_pad):
    c = pl.program_id(0)
    tm = x_ref.shape[0]
    n = _rms(x_ref[...], nw_ref[...]).astype(BF16)
    tok = jnp.dot(n, w_ref[...], preferred_element_type=F32)
    fea = lax.dot_general(wT_ref[...], n, NT_DIMS, preferred_element_type=F32)

    pm_ref[...] = tok[:, :MLA_COLS]
    for a in range(2 * NSA_G):
        r_ref[a] = tok[:, SEC_CMP + a * NSA_D:SEC_CMP + (a + 1) * NSA_D]
    r = lax.broadcasted_iota(jnp.int32, (tm, NSA_D), 0)
    blk = lax.broadcasted_iota(jnp.int32, (tm, nsel_pad), 1)
    kblk = jnp.right_shift(c * tm + lax.broadcasted_iota(jnp.int32, (tm, nsel_pad), 0), SEL_SHIFT)
    onehot = jnp.where(blk == kblk, 1.0, 0.0)
    sel_digits = _digit_cols(r, 1)
    win_digits = _digit_cols(jnp.bitwise_and(r, NSA_TQ - 1), 1)
    ones_row = jnp.where(lax.broadcasted_iota(jnp.int32, (SEL_VROWS - NSA_D, tm), 0) == 0, 1.0, 0.0)
    for g in range(NSA_G):
        ks = tok[:, SEC_KS + g * NSA_D:SEC_KS + (g + 1) * NSA_D]
        kw = tok[:, SEC_KW + g * NSA_D:SEC_KW + (g + 1) * NSA_D]
        kas_ref[g] = jnp.concatenate([ks, sel_digits, onehot], axis=-1).astype(BF16)
        kaw_ref[g] = jnp.concatenate([kw, win_digits], axis=-1).astype(BF16)
        vas_ref[g, 0] = jnp.concatenate([fea[ROW_VS + g * NSA_D:ROW_VS + (g + 1) * NSA_D, :], ones_row],
                                        axis=0).astype(BF16)
        vaw_ref[g] = jnp.concatenate([fea[ROW_VW + g * NSA_D:ROW_VW + (g + 1) * NSA_D, :], ones_row],
                                     axis=0).astype(BF16)
    qT_ref[...] = fea[ROW_Q:ROW_VS, :]
    glT_ref[...] = fea[ROW_GATE:, :]


def _in_proj(x2, attn_norm_w, w_tok, w_feaT, nsel_pad, tm=SEL_TK):
    T = x2.shape[0]
    tm = min(tm, T)
    kw = 2 * NSA_D + nsel_pad
    fixed = lambda i: (0, 0)
    return pl.pallas_call(
        functools.partial(_in_proj_kernel, nsel_pad=nsel_pad),
        out_shape=(
            jax.ShapeDtypeStruct((T, MLA_COLS), F32),
            jax.ShapeDtypeStruct((2 * NSA_G, T, NSA_D), F32),
            jax.ShapeDtypeStruct((NSA_G, T, kw), BF16),
            jax.ShapeDtypeStruct((NSA_G, T, 2 * NSA_D), BF16),
            jax.ShapeDtypeStruct((NSA_HEADS * NSA_D, T), F32),
            jax.ShapeDtypeStruct((NSA_G, T // tm, SEL_VROWS, tm), BF16),
            jax.ShapeDtypeStruct((NSA_G, SEL_VROWS, T), BF16),
            jax.ShapeDtypeStruct((LANES, T), F32),
        ),
        grid=(T // tm,),
        in_specs=[
            pl.BlockSpec((tm, D_MODEL), lambda i: (i, 0)),
            pl.BlockSpec((1, D_MODEL), fixed),
            pl.BlockSpec((D_MODEL, TM_COLS), fixed),
            pl.BlockSpec((FM_ROWS, D_MODEL), fixed),
        ],
        out_specs=(
            pl.BlockSpec((tm, MLA_COLS), lambda i: (i, 0)),
            pl.BlockSpec((2 * NSA_G, tm, NSA_D), lambda i: (0, i, 0)),
            pl.BlockSpec((NSA_G, tm, kw), lambda i: (0, i, 0)),
            pl.BlockSpec((NSA_G, tm, 2 * NSA_D), lambda i: (0, i, 0)),
            pl.BlockSpec((NSA_HEADS * NSA_D, tm), lambda i: (0, i)),
            pl.BlockSpec((NSA_G, 1, SEL_VROWS, tm), lambda i: (0, i, 0, 0)),
            pl.BlockSpec((NSA_G, SEL_VROWS, tm), lambda i: (0, 0, i)),
            pl.BlockSpec((LANES, tm), lambda i: (0, i)),
        ),
        compiler_params=_cparams(("parallel",)),
        name="in_proj",
    )(x2, attn_norm_w.reshape(1, D_MODEL), w_tok, w_feaT)


def _mla_prep_kernel(cq_ref, ckv_ref, kr_ref, pos_ref, qnw_ref, kvnw_ref, wuqT_ref, wuk_ref, wuvT_ref, inv_ref,
                     qT_ref, k_ref, vT_ref):
    scale = (MLA_NOPE + MLA_ROPE) ** -0.5 * LOG2E
    qn = _rms(cq_ref[...], qnw_ref[...]).astype(BF16)
    qT = lax.dot_general(wuqT_ref[...], qn, NT_DIMS, preferred_element_type=F32)
    kvn = _rms(ckv_ref[...], kvnw_ref[...]).astype(BF16)
    kn = jnp.dot(kvn, wuk_ref[...], preferred_element_type=F32)
    vT = lax.dot_general(wuvT_ref[...], kvn, NT_DIMS, preferred_element_type=F32)

    ang = pos_ref[...].astype(F32) * inv_ref[...]
    c = jnp.cos(ang)
    s = jnp.sin(ang)
    lane = lax.broadcasted_iota(jnp.int32, ang.shape, 1)
    coef = jnp.where(lane < 64, c, jnp.where(lane < 96, -s, s))
    coefT = coef.T

    ones_row = jnp.where(lax.broadcasted_iota(jnp.int32, (MLA_VROWS - MLA_V, coefT.shape[1]), 0) == 0, 1.0, 0.0)
    t = kr_ref[...] * coef
    kpe = jnp.where(lane < 64, t + pltpu.roll(t, 64, 1), 0.0)
    for h in range(MLA_HEADS):
        base = h * MLA_QK_PAD
        tT = qT[base + 128:base + 256, :] * coefT
        ropeT = tT + pltpu.roll(tT, 64, 0)
        qT_ref[h] = (jnp.concatenate([qT[base:base + 128, :], ropeT], axis=0) * scale).astype(BF16)
        k_ref[h] = jnp.concatenate([kn[:, h * 128:(h + 1) * 128], kpe], axis=-1).astype(BF16)
        vT_ref[h] = jnp.concatenate([vT[h * 128:(h + 1) * 128, :], ones_row], axis=0).astype(BF16)


def _mla_prep(pm, pos2, qnw, kvnw, wuqT, wuk, wuvT, inv128, tm=256):
    T = pm.shape[0]
    H = MLA_HEADS
    fixed = lambda i: (0, 0)
    return pl.pallas_call(
        _mla_prep_kernel,
        out_shape=(
            jax.ShapeDtypeStruct((H, MLA_QK_PAD, T), BF16),
            jax.ShapeDtypeStruct((H, T, MLA_QK_PAD), BF16),
            jax.ShapeDtypeStruct((H, MLA_VROWS, T), BF16),
        ),
        grid=(T // tm,),
        in_specs=[
            pl.BlockSpec((tm, MLA_Q_RANK), lambda i: (i, SEC_CQ // MLA_Q_RANK)),
            pl.BlockSpec((tm, MLA_KV_RANK), lambda i: (i, SEC_CKV // MLA_KV_RANK)),
            pl.BlockSpec((tm, LANES), lambda i: (i, SEC_ROPE // LANES)),
            pl.BlockSpec((tm, 1), lambda i: (i, 0)),
            pl.BlockSpec((1, MLA_Q_RANK), fixed),
            pl.BlockSpec((1, MLA_KV_RANK), fixed),
            pl.BlockSpec((H * MLA_QK_PAD, MLA_Q_RANK), fixed),
            pl.BlockSpec((MLA_KV_RANK, H * MLA_NOPE), fixed),
            pl.BlockSpec((H * MLA_V, MLA_KV_RANK), fixed),
            pl.BlockSpec((1, LANES), fixed),
        ],
        out_specs=(
            pl.BlockSpec((H, MLA_QK_PAD, tm), lambda i: (0, 0, i)),
            pl.BlockSpec((H, tm, MLA_QK_PAD), lambda i: (0, i, 0)),
            pl.BlockSpec((H, MLA_VROWS, tm), lambda i: (0, 0, i)),
        ),
        compiler_params=_cparams(("parallel",)),
        name="mla_prep",
    )(pm, pm, pm, pos2, qnw.reshape(1, -1), kvnw.reshape(1, -1), wuqT, wuk, wuvT, inv128)


def _mla_attn_kernel(qi_ref, ki_ref, qT_ref, k_ref, vT_ref, o_ref, m_scr, acc_scr,
                     s0_scr, s1_scr, top0_scr, top1_scr, p0_scr, p1_scr, al0_scr, al1_scr):
    p_id = pl.program_id(0)
    qi = qi_ref[p_id]
    ki = ki_ref[p_id]
    tk, tq = s0_scr.shape
    last = (qi * tq + tq - 1) // tk
    s_slots, top_slots = (s0_scr, s1_scr), (top0_scr, top1_scr)
    p_slots, al_slots = (p0_scr, p1_scr), (al0_scr, al1_scr)

    @pl.when(ki == 0)
    def _():
        m_scr[...] = jnp.full(m_scr.shape, NEG, F32)
        acc_scr[...] = jnp.zeros(acc_scr.shape, F32)

    def run(masked):
        def scores(h, slot):
            s = jnp.dot(k_ref[h], qT_ref[h], preferred_element_type=F32)
            if masked:
                key = ki * tk + lax.broadcasted_iota(jnp.int32, s.shape, 0)
                qry = qi * tq + lax.broadcasted_iota(jnp.int32, s.shape, 1)
                s = jnp.where(key <= qry, s, NEG)
            s_slots[slot][...] = s
            top_slots[slot][...] = jnp.max(s.reshape(tk // SUBLANES, SUBLANES, tq), axis=0)

        def softmax(h, slot):
            m_prev = m_scr[h]
            m_new = jnp.maximum(m_prev, jnp.max(top_slots[slot][...], axis=0, keepdims=True))
            for b in range(tk // BF16_ROWS):
                rows = slice(b * BF16_ROWS, (b + 1) * BF16_ROWS)
                p_slots[slot][rows, :] = jnp.exp2(s_slots[slot][rows, :] - m_new).astype(BF16)
            al_slots[slot][...] = jnp.exp2(m_prev - m_new)
            m_scr[h] = m_new

        def values(h, slot):
            acc = al_slots[slot][...] * acc_scr[h] + jnp.dot(vT_ref[h], p_slots[slot][...],
                                                             preferred_element_type=F32)
            acc_scr[h] = acc

        scores(0, 0)
        scores(1, 1)
        softmax(0, 0)

        def body(hh, carry):
            h = 2 * hh
            scores(h, 0)
            softmax(h - 1, 1)
            values(h - 2, 0)
            scores(h + 1, 1)
            softmax(h, 0)
            values(h - 1, 1)
            return carry

        lax.fori_loop(1, MLA_HEADS // 2, body, 0)
        softmax(MLA_HEADS - 1, 1)
        values(MLA_HEADS - 2, 0)
        values(MLA_HEADS - 1, 1)

    @pl.when(ki * tk + tk - 1 <= qi * tq)
    def _():
        run(False)

    @pl.when(ki * tk + tk - 1 > qi * tq)
    def _():
        run(True)

    @pl.when(ki == last)
    def _():
        for h in range(MLA_HEADS):
            acc = acc_scr[h]
            o_ref[h * MLA_V:(h + 1) * MLA_V, :] = acc[:MLA_V, :] * (1.0 / acc[MLA_V:MLA_V + 1, :])


def _tri_pairs(nq, per_q):
    qi, ki = [], []
    for i in range(nq):
        for c in range(per_q(i) + 1):
            qi.append(i)
            ki.append(c)
    return jnp.asarray(qi, jnp.int32), jnp.asarray(ki, jnp.int32)


def _mla_attn(qT, k, vT):
    H, T, _ = k.shape
    tq = min(MLA_TQ, T)
    tk = min(MLA_TK, T)
    qi, ki = _tri_pairs(T // tq, lambda i: (i * tq + tq - 1) // tk)
    grid_spec = pltpu.PrefetchScalarGridSpec(
        num_scalar_prefetch=2,
        grid=(int(qi.shape[0]),),
        in_specs=[
            pl.BlockSpec((H, MLA_QK_PAD, tq), lambda p, qi, ki: (0, 0, qi[p])),
            pl.BlockSpec((H, tk, MLA_QK_PAD), lambda p, qi, ki: (0, ki[p], 0)),
            pl.BlockSpec((H, MLA_VROWS, tk), lambda p, qi, ki: (0, 0, ki[p])),
        ],
        out_specs=pl.BlockSpec((H * MLA_V, tq), lambda p, qi, ki: (0, qi[p])),
        scratch_shapes=[
            pltpu.VMEM((H, 1, tq), F32),
            pltpu.VMEM((H, MLA_VROWS, tq), F32),
            pltpu.VMEM((tk, tq), F32),
            pltpu.VMEM((tk, tq), F32),
            pltpu.VMEM((SUBLANES, tq), F32),
            pltpu.VMEM((SUBLANES, tq), F32),
            pltpu.VMEM((tk, tq), BF16),
            pltpu.VMEM((tk, tq), BF16),
            pltpu.VMEM((1, tq), F32),
            pltpu.VMEM((1, tq), F32),
        ],
    )
    return pl.pallas_call(
        _mla_attn_kernel,
        out_shape=jax.ShapeDtypeStruct((H * MLA_V, T), F32),
        grid_spec=grid_spec,
        compiler_params=_cparams(("arbitrary",)),
        name="mla_attn",
    )(qi, ki, qT, k, vT)


def _gelu_tanh(x):
    return 0.5 * x * (1.0 + jnp.tanh(np.sqrt(2.0 / np.pi).astype(np.float32) * (x + 0.044715 * (x * x * x))))


def _compress_kernel(r_ref, pos_ref, w1_ref, w2_ref, kc_ref, vcT_ref):
    r = r_ref[0]
    nr = r.shape[0]
    half = CMP_STRIDE * NSA_D
    hp = lax.Precision.HIGHEST
    a = jnp.dot(r + pos_ref[0, 0:1, :], w1_ref[0, :half, :], precision=hp, preferred_element_type=F32)
    b = jnp.dot(r + pos_ref[0, 1:2, :], w1_ref[0, half:, :], precision=hp, preferred_element_type=F32)
    hid = _gelu_tanh(a + pltpu.roll(b, nr - 1, 0))
    out = jnp.dot(hid, w2_ref[0], precision=hp, preferred_element_type=F32)
    row = lax.broadcasted_iota(jnp.int32, out.shape, 0)
    out = jnp.where(row < nr - 1, out, 0.0)
    is_key = pl.program_id(0) % 2 == 0

    @pl.when(is_key)
    def _():
        kc_ref[0] = jnp.concatenate([out, _digit_cols(row, CMP_STRIDE)], axis=-1).astype(BF16)

    @pl.when(jnp.logical_not(is_key))
    def _():
        vcT_ref[0] = jnp.concatenate([out, jnp.zeros_like(out)], axis=-1).T[0:NSA_D, :].astype(BF16)


def _compress(r4, pos_kv, w1_kv, w2_kv):
    _, nr, width = r4.shape
    return pl.pallas_call(
        _compress_kernel,
        out_shape=(
            jax.ShapeDtypeStruct((NSA_G, nr, 2 * NSA_D), BF16),
            jax.ShapeDtypeStruct((NSA_G, NSA_D, nr), BF16),
        ),
        grid=(2 * NSA_G,),
        in_specs=[
            pl.BlockSpec((1, nr, width), lambda a: (a, 0, 0)),
            pl.BlockSpec((1, 2, width), lambda a: (a % 2, 0, 0)),
            pl.BlockSpec((1, 2 * width, CMP_HIDDEN), lambda a: (a % 2, 0, 0)),
            pl.BlockSpec((1, CMP_HIDDEN, NSA_D), lambda a: (a % 2, 0, 0)),
        ],
        out_specs=(
            pl.BlockSpec((1, nr, 2 * NSA_D), lambda a: (a // 2, 0, 0)),
            pl.BlockSpec((1, NSA_D, nr), lambda a: (a // 2, 0, 0)),
        ),
        compiler_params=_cparams(("arbitrary",)),
        name="nsa_compress",
    )(r4, pos_kv, w1_kv, w2_kv)


def _alibi_rows(tq):
    start = np.float32(2.0 ** (-8.0 / NSA_HEADS))
    slopes = (start ** np.arange(1, NSA_HEADS + 1, dtype=np.float32)).astype(np.float32)
    slopes = (slopes.astype(np.float64) * LOG2E).astype(np.float32)
    parts, rest = [], slopes
    for _ in range(ALIBI_PARTS):
        piece = (rest.view(np.uint32) & np.uint32(0xFFFF0000)).view(np.float32)
        parts.append(piece)
        rest = rest - piece
    assert not rest.any()
    pieces = np.stack(parts)
    rows = np.concatenate([pieces * ALIBI_HI, pieces, np.zeros((NSA_D - 2 * ALIBI_PARTS, NSA_HEADS), np.float32)])
    rows = np.repeat(rows.reshape(NSA_D, NSA_G, NSA_HPG), tq, axis=2).transpose(1, 0, 2)
    srow = np.repeat(slopes.reshape(NSA_G, 1, NSA_HPG), tq, axis=2)
    return jnp.asarray(rows, BF16), jnp.asarray(srow, F32)


def _nsa_queries(qT_ref, arow_ref):
    q = qT_ref[...] * (NSA_D ** -0.5 * LOG2E)
    top = jnp.concatenate([q[h * NSA_D:(h + 1) * NSA_D, :] for h in range(NSA_HPG)], axis=1).astype(BF16)
    return jnp.concatenate([top, arow_ref[0]], axis=0)


def _heads_to_rows(o):
    tq = o.shape[1] // NSA_HPG
    return jnp.concatenate([o[:, h * tq:(h + 1) * tq] for h in range(NSA_HPG)], axis=0)


def _token_pos(i, tq):
    return i * tq + jnp.bitwise_and(lax.broadcasted_iota(jnp.int32, (1, NSA_HPG * tq), 1), tq - 1)


def _bf16_pieces(x, n):
    pieces = []
    for _ in range(n - 1):
        top = pltpu.bitcast(jnp.bitwise_and(pltpu.bitcast(x, jnp.uint32), jnp.uint32(0xFFFF0000)), F32)
        pieces.append(top.astype(BF16))
        x = x - top
    return pieces + [x.astype(BF16)]


def _cmp_attn_kernel(qT_ref, arow_ref, kc_ref, vcT_ref, ov_ref, o_ref, selT_ref, *, nsel_pad):
    i = pl.program_id(1)
    tq = NSA_TQ
    n = NSA_HPG * tq
    nr = kc_ref.shape[1]
    s = jnp.dot(kc_ref[0], _nsa_queries(qT_ref, arow_ref), preferred_element_type=F32)
    tpos = _token_pos(i, tq)
    last_done = jnp.right_shift(tpos - (CMP_LEN - 1), 4)
    valid = lax.broadcasted_iota(jnp.int32, (nr, n), 0) <= last_done
    s = jnp.where(valid, s, NEG)
    e = jnp.where(valid, jnp.exp2(s - jnp.max(s, axis=0, keepdims=True)), 0.0)
    l = jnp.sum(e, axis=0, keepdims=True)
    inv = 1.0 / jnp.where(l > 0.0, l, 1.0)
    acc = jnp.dot(vcT_ref[0], e.astype(BF16), preferred_element_type=F32)
    o_ref[...] = _heads_to_rows(acc * inv)

    pn = e * inv
    psum = pn[:, 0:tq]
    for h in range(1, NSA_HPG):
        psum = psum + pn[:, h * tq:(h + 1) * tq]
    imp = sum(jnp.dot(ov_ref[...], piece, preferred_element_type=F32) for piece in _bf16_pieces(psum, 3))

    blk = lax.broadcasted_iota(jnp.int32, (nsel_pad, tq), 0)
    cur = jnp.right_shift(i * tq + lax.broadcasted_iota(jnp.int32, (nsel_pad, tq), 1), SEL_SHIFT)
    forced = jnp.logical_or(blk == 0, jnp.logical_or(blk == cur, blk == cur - 1))
    score = jnp.where(blk <= cur, imp + jnp.where(forced, FORCE_BONUS, 0.0), NEG)
    blk_f = blk.astype(F32)

    def pick(_, carry):
        sc, sel = carry
        mx = jnp.max(sc, axis=0, keepdims=True)
        first = jnp.min(jnp.where(sc == mx, blk_f, float(nsel_pad)), axis=0, keepdims=True)
        hit = blk_f == first
        sel = jnp.where(jnp.logical_and(hit, mx > NEG * 0.5), 1.0, sel)
        return jnp.where(hit, REMOVED, sc), sel

    _, sel = lax.fori_loop(0, SEL_TOPK, pick, (score, jnp.zeros((nsel_pad, tq), F32)))
    selT_ref[0] = sel


def _overlap_matrix(nsel_pad, nr):
    j = np.arange(nsel_pad)[:, None]
    n = np.arange(nr)[None, :]
    ov = (n * CMP_STRIDE <= j * SEL_LEN + (SEL_LEN - 1)) & (n * CMP_STRIDE + (CMP_LEN - 1) >= j * SEL_LEN) & (n < nr - 1)
    return jnp.asarray(ov, BF16)


def _cmp_attn(fm, arows, kc, vcT, nsel_pad):
    T = fm.shape[1]
    nr = kc.shape[1]
    tq = NSA_TQ
    n = NSA_HPG * tq
    return pl.pallas_call(
        functools.partial(_cmp_attn_kernel, nsel_pad=nsel_pad),
        out_shape=(
            jax.ShapeDtypeStruct((NSA_HEADS * NSA_D, T), F32),
            jax.ShapeDtypeStruct((NSA_G, nsel_pad, T), F32),
        ),
        grid=(NSA_G, T // tq),
        in_specs=[
            pl.BlockSpec((NSA_HPG * NSA_D, tq), lambda g, i: (g, i)),
            pl.BlockSpec((1, NSA_D, n), lambda g, i: (g, 0, 0)),
            pl.BlockSpec((1, nr, 2 * NSA_D), lambda g, i: (g, 0, 0)),
            pl.BlockSpec((1, NSA_D, nr), lambda g, i: (g, 0, 0)),
            pl.BlockSpec((nsel_pad, nr), lambda g, i: (0, 0)),
        ],
        out_specs=(
            pl.BlockSpec((NSA_HPG * NSA_D, tq), lambda g, i: (g, i)),
            pl.BlockSpec((1, nsel_pad, tq), lambda g, i: (g, 0, i)),
        ),
        compiler_params=_cparams(("parallel", "parallel")),
        name="nsa_cmp_attn",
    )(fm, arows, kc, vcT, _overlap_matrix(nsel_pad, nr))


def _sel_attn_kernel(qT_ref, selT_ref, arow_ref, srow_ref, ka_ref, va_ref, o_ref,
                     qa_scr, s0_scr, s1_scr, top0_scr, top1_scr, p0_scr, p1_scr, al0_scr, al1_scr, m_scr, acc_scr,
                     *, tk, n_chunks):
    i = pl.program_id(1)
    tq = SEL_TQ
    n = NSA_HPG * tq
    nsel_pad = selT_ref.shape[1]
    qa_scr[0:2 * NSA_D, :] = _nsa_queries(qT_ref, arow_ref)
    blk = lax.broadcasted_iota(jnp.int32, (nsel_pad, tq), 0)
    own = blk // (tq // SEL_LEN) == i
    chosen = jnp.logical_and(selT_ref[0] > 0.5, jnp.logical_not(own))
    qa_scr[2 * NSA_D:, :] = jnp.concatenate([jnp.where(chosen, 0.0, NEG_BF16).astype(BF16)] * NSA_HPG, axis=1)

    tpos = _token_pos(i, tq)
    slope_row = srow_ref[0]

    def rterm(c):
        return slope_row * (c * tk - tpos).astype(F32)

    c_own = (i * tq) // tk
    own_start = pl.multiple_of(i * tq, tq)
    s = jnp.dot(ka_ref[0, pl.ds(own_start, tq), 0:2 * NSA_D], qa_scr[0:2 * NSA_D, :], preferred_element_type=F32)
    kpos = i * tq + lax.broadcasted_iota(jnp.int32, s.shape, 0)
    s = jnp.where(kpos <= tpos, s, NEG)
    r_own = rterm(c_own)
    m0 = jnp.max(s, axis=0, keepdims=True) + r_own
    p = jnp.exp2(s - (m0 - r_own)).astype(BF16)
    v_own = va_ref[0, c_own, :, pl.ds(pl.multiple_of(i * tq - c_own * tk, tq), tq)]
    acc_scr[...] = jnp.dot(v_own, p, preferred_element_type=F32)
    m_scr[...] = m0

    s_slots = (s0_scr, s1_scr)
    p_slots = (p0_scr, p1_scr)
    al_slots = (al0_scr, al1_scr)

    top_slots = (top0_scr, top1_scr)

    def scores(j, slot):
        start = pl.multiple_of(j * tk, tk)
        s = jnp.dot(ka_ref[0, pl.ds(start, tk), :], qa_scr[...], preferred_element_type=F32)
        s_slots[slot][...] = s
        top_slots[slot][...] = jnp.max(s.reshape(tk // SUBLANES, SUBLANES, n), axis=0)

    def softmax(j, slot):
        r = rterm(j)
        m_prev = m_scr[...]
        s_ref, p_ref = s_slots[slot], p_slots[slot]
        m_new = jnp.maximum(m_prev, jnp.max(top_slots[slot][...], axis=0, keepdims=True) + r)
        shift = m_new - r
        for b in range(tk // BF16_ROWS):
            rows = slice(b * BF16_ROWS, (b + 1) * BF16_ROWS)
            p_ref[rows, :] = jnp.exp2(s_ref[rows, :] - shift).astype(BF16)
        al_slots[slot][...] = jnp.exp2(m_prev - m_new)
        m_scr[...] = m_new

    def values(j, slot):
        acc_scr[...] = (al_slots[slot][...] * acc_scr[...]
                        + jnp.dot(va_ref[0, j], p_slots[slot][...], preferred_element_type=F32))

    n_pairs = jnp.minimum(((i * tq + tq - 1) // tk + 2) // 2, n_chunks // 2)
    scores(0, 0)
    scores(1, 1)
    softmax(0, 0)

    def body(jj, carry):
        j = 2 * jj
        scores(j, 0)
        softmax(j - 1, 1)
        values(j - 2, 0)
        scores(j + 1, 1)
        softmax(j, 0)
        values(j - 1, 1)
        return carry

    lax.fori_loop(1, n_pairs, body, 0)
    softmax(2 * n_pairs - 1, 1)
    values(2 * n_pairs - 2, 0)
    values(2 * n_pairs - 1, 1)
    acc = acc_scr[...]
    o_ref[...] = _heads_to_rows(acc[0:NSA_D, :] * (1.0 / acc[NSA_D:NSA_D + 1, :]))


def _sel_attn(fm, selT, arows, srow, ka, va):
    _, T, kw = ka.shape
    tk = va.shape[-1]
    tq = min(SEL_TQ, T)
    nsel_pad = selT.shape[1]
    n = NSA_HPG * tq
    return pl.pallas_call(
        functools.partial(_sel_attn_kernel, tk=tk, n_chunks=T // tk),
        out_shape=jax.ShapeDtypeStruct((NSA_HEADS * NSA_D, T), F32),
        grid=(NSA_G, T // tq),
        in_specs=[
            pl.BlockSpec((NSA_HPG * NSA_D, tq), lambda g, i: (g, i)),
            pl.BlockSpec((1, nsel_pad, tq), lambda g, i: (g, 0, i)),
            pl.BlockSpec((1, NSA_D, n), lambda g, i: (g, 0, 0)),
            pl.BlockSpec((1, 1, n), lambda g, i: (g, 0, 0)),
            pl.BlockSpec((1, T, kw), lambda g, i: (g, 0, 0)),
            pl.BlockSpec((1, T // tk, SEL_VROWS, tk), lambda g, i: (g, 0, 0, 0)),
        ],
        out_specs=pl.BlockSpec((NSA_HPG * NSA_D, tq), lambda g, i: (g, i)),
        scratch_shapes=[
            pltpu.VMEM((kw, n), BF16),
            pltpu.VMEM((tk, n), F32),
            pltpu.VMEM((tk, n), F32),
            pltpu.VMEM((SUBLANES, n), F32),
            pltpu.VMEM((SUBLANES, n), F32),
            pltpu.VMEM((tk, n), BF16),
            pltpu.VMEM((tk, n), BF16),
            pltpu.VMEM((1, n), F32),
            pltpu.VMEM((1, n), F32),
            pltpu.VMEM((1, n), F32),
            pltpu.VMEM((SEL_VROWS, n), F32),
        ],
        compiler_params=_cparams(("parallel", "arbitrary")),
        name="nsa_sel_attn",
    )(fm, selT, arows, srow, ka, va)


WIN_KEYS = WINDOW + NSA_TQ


def _win_attn_kernel(qT_ref, arow_ref, srow_ref, ka_ref, va_ref, o_ref):
    i = pl.program_id(1)
    tq = NSA_TQ
    n = NSA_HPG * tq
    start = pl.multiple_of(jnp.maximum(i * tq - WINDOW, 0), tq)
    s = jnp.dot(ka_ref[0, pl.ds(start, WIN_KEYS), :], _nsa_queries(qT_ref, arow_ref), preferred_element_type=F32)
    tpos = _token_pos(i, tq)
    kpos = start + lax.broadcasted_iota(jnp.int32, s.shape, 0)
    allowed = jnp.logical_and(kpos <= tpos, kpos > tpos - WINDOW)
    s = jnp.where(allowed, s, NEG)
    slope_row = srow_ref[0]
    tiles = [s[j * tq:(j + 1) * tq, :] for j in range(WIN_KEYS // tq)]
    rterms = [slope_row * (start + j * tq - tpos).astype(F32) for j in range(WIN_KEYS // tq)]
    m = jnp.max(tiles[0], axis=0, keepdims=True) + rterms[0]
    for sj, rj in zip(tiles[1:], rterms[1:]):
        m = jnp.maximum(m, jnp.max(sj, axis=0, keepdims=True) + rj)
    p = jnp.concatenate([jnp.exp2(sj - (m - rj)) for sj, rj in zip(tiles, rterms)], axis=0).astype(BF16)
    acc = jnp.dot(va_ref[0, :, pl.ds(start, WIN_KEYS)], p, preferred_element_type=F32)
    o_ref[...] = _heads_to_rows(acc[0:NSA_D, :] * (1.0 / acc[NSA_D:NSA_D + 1, :]))


def _win_attn(fm, arows, srow, ka, va):
    _, T, kw = ka.shape
    assert T >= WIN_KEYS
    tq = NSA_TQ
    n = NSA_HPG * tq
    return pl.pallas_call(
        _win_attn_kernel,
        out_shape=jax.ShapeDtypeStruct((NSA_HEADS * NSA_D, T), F32),
        grid=(NSA_G, T // tq),
        in_specs=[
            pl.BlockSpec((NSA_HPG * NSA_D, tq), lambda g, i: (g, i)),
            pl.BlockSpec((1, NSA_D, n), lambda g, i: (g, 0, 0)),
            pl.BlockSpec((1, 1, n), lambda g, i: (g, 0, 0)),
            pl.BlockSpec((1, T, kw), lambda g, i: (g, 0, 0)),
            pl.BlockSpec((1, SEL_VROWS, T), lambda g, i: (g, 0, 0)),
        ],
        out_specs=pl.BlockSpec((NSA_HPG * NSA_D, tq), lambda g, i: (g, i)),
        compiler_params=_cparams(("parallel", "parallel")),
        name="nsa_win_attn",
    )(fm, arows, srow, ka, va)


def _out_proj_kernel(x_ref, ymT_ref, ocT_ref, osT_ref, owT_ref, glT_ref, mnw_ref, nnw_ref, wo_ref, o_ref):
    gates = 1.0 / (1.0 + jnp.exp(-glT_ref[...]))
    heads = []
    for h in range(NSA_HEADS):
        rows = slice(h * NSA_D, (h + 1) * NSA_D)
        heads.append(gates[3 * h:3 * h + 1, :] * ocT_ref[rows, :]
                     + gates[3 * h + 1:3 * h + 2, :] * osT_ref[rows, :]
                     + gates[3 * h + 2:3 * h + 3, :] * owT_ref[rows, :])
    ynT = jnp.concatenate(heads, axis=0)

    def rms_cols(y, w_col):
        return (y * lax.rsqrt(jnp.mean(y * y, axis=0, keepdims=True) + EPS) * w_col).astype(BF16)

    half = MLA_HEADS * MLA_V
    o_ref[...] = (x_ref[...]
                  + lax.dot_general(rms_cols(ymT_ref[...], mnw_ref[...]), wo_ref[:half, :], TN_DIMS,
                                    preferred_element_type=F32)
                  + lax.dot_general(rms_cols(ynT, nnw_ref[...]), wo_ref[half:, :], TN_DIMS,
                                    preferred_element_type=F32))


def _out_proj(x2, y_mlaT, o_cT, o_sT, o_wT, glT, mnw, nnw, wo, tm=512):
    T = x2.shape[0]
    half = MLA_HEADS * MLA_V
    row = lambda i: (i, 0)
    col = lambda i: (0, i)
    fixed2 = lambda i: (0, 0)
    return pl.pallas_call(
        _out_proj_kernel,
        out_shape=jax.ShapeDtypeStruct((T, D_MODEL), F32),
        grid=(T // tm,),
        in_specs=[
            pl.BlockSpec((tm, D_MODEL), row),
            pl.BlockSpec((half, tm), col),
            pl.BlockSpec((half, tm), col),
            pl.BlockSpec((half, tm), col),
            pl.BlockSpec((half, tm), col),
            pl.BlockSpec((LANES, tm), col),
            pl.BlockSpec((half, 1), fixed2),
            pl.BlockSpec((half, 1), fixed2),
            pl.BlockSpec((2 * half, D_MODEL), fixed2),
        ],
        out_specs=pl.BlockSpec((tm, D_MODEL), row),
        compiler_params=_cparams(("parallel",)),
        name="out_proj",
    )(x2, y_mlaT, o_cT, o_sT, o_wT, glT, mnw.reshape(-1, 1), nnw.reshape(-1, 1), wo)


def _mlp_kernel(h_ref, nw_ref, wu_ref, wd_ref, fw_ref, o_ref, n_scr, acc_scr, *, final):
    c = pl.program_id(1)

    @pl.when(c == 0)
    def _():
        n_scr[...] = _rms(h_ref[...], nw_ref[...]).astype(BF16)
        acc_scr[...] = jnp.zeros(acc_scr.shape, F32)

    u = jnp.dot(n_scr[...], wu_ref[...], preferred_element_type=F32)
    a = jnp.square(jnp.maximum(u, 0.0)).astype(BF16)
    acc_scr[...] += jnp.dot(a, wd_ref[...], preferred_element_type=F32)

    @pl.when(c == pl.num_programs(1) - 1)
    def _():
        h2 = h_ref[...] + acc_scr[...]
        o_ref[...] = _rms(h2, fw_ref[...]) if final else h2


def _mlp(h1, nw, wu, wd, fw, final, tm=512, tc=1024):
    T = h1.shape[0]
    tm = min(tm, T)
    return pl.pallas_call(
        functools.partial(_mlp_kernel, final=final),
        out_shape=jax.ShapeDtypeStruct((T, D_MODEL), F32),
        grid=(T // tm, MLP_HIDDEN // tc),
        in_specs=[
            pl.BlockSpec((tm, D_MODEL), lambda i, c: (i, 0)),
            pl.BlockSpec((1, D_MODEL), lambda i, c: (0, 0)),
            pl.BlockSpec((D_MODEL, tc), lambda i, c: (0, c)),
            pl.BlockSpec((tc, D_MODEL), lambda i, c: (c, 0)),
            pl.BlockSpec((1, D_MODEL), lambda i, c: (0, 0)),
        ],
        out_specs=pl.BlockSpec((tm, D_MODEL), lambda i, c: (i, 0)),
        scratch_shapes=[pltpu.VMEM((tm, D_MODEL), BF16), pltpu.VMEM((tm, D_MODEL), F32)],
        compiler_params=_cparams(("parallel", "arbitrary")),
        name="mlp",
    )(h1, nw.reshape(1, -1), wu, wd, fw.reshape(1, -1))


def _regroup_w_in(w_in):
    cq = w_in[:, 0:512]
    ckv = w_in[:, 512:768]
    kr = w_in[:, 768:832]
    q = w_in[:, 832:1856]
    kv_cmp = w_in[:, 1856:2112]
    kv_slc = w_in[:, 2112:2368].reshape(-1, NSA_G, 2, NSA_D)
    kv_win = w_in[:, 2368:2624].reshape(-1, NSA_G, 2, NSA_D)
    gates = w_in[:, 2624:2672]
    x1, x2 = kr[:, :32], kr[:, 32:]
    flat = lambda t: t.reshape(t.shape[0], NSA_G * NSA_D)
    w_tok = jnp.concatenate([cq, ckv, x1, x2, x2, x1, kv_cmp, flat(kv_slc[:, :, 0]), flat(kv_win[:, :, 0])], axis=1)
    zeros = jnp.zeros((w_in.shape[0], FM_ROWS - ROW_GATE - gates.shape[1]), w_in.dtype)
    w_fea = jnp.concatenate([q, flat(kv_slc[:, :, 1]), flat(kv_win[:, :, 1]), gates, zeros], axis=1)
    return w_tok.astype(BF16), w_fea.T.astype(BF16)


def _regroup_w_uq(w_uq):
    w = w_uq.reshape(MLA_Q_RANK, MLA_HEADS, MLA_NOPE + MLA_ROPE)
    nope = w[:, :, :MLA_NOPE]
    x1 = w[:, :, MLA_NOPE:MLA_NOPE + 32]
    x2 = w[:, :, MLA_NOPE + 32:]
    return jnp.concatenate([nope, x1, x2, x2, x1], axis=-1).reshape(MLA_Q_RANK, MLA_HEADS * MLA_QK_PAD).astype(BF16)


def kernel(x, positions, attn_norm_w, w_in, mla_q_norm_w, mla_w_uq, mla_kv_norm_w, mla_w_ukv, cmp_pos_k,
           cmp_pos_v, cmp_w1_k, cmp_w2_k, cmp_w1_v, cmp_w2_v, mla_out_norm_w, nsa_out_norm_w, w_o, mlp_norm_w,
           w_up, w_down, final_norm_w):
    B, T, _ = x.shape
    depth = w_in.shape[0]
    inv = 1.0 / (ROPE_THETA ** (jnp.arange(0, MLA_ROPE, 2, dtype=F32) / MLA_ROPE))
    inv128 = jnp.tile(inv, 4).reshape(1, LANES)
    arows, srow = _alibi_rows(NSA_TQ)
    arows_sel, srow_sel = _alibi_rows(SEL_TQ)
    nsel_pad = -(-(T // SEL_LEN) // LANES) * LANES

    outs = []
    for b in range(B):
        h = x[b]
        pos2 = positions[b].reshape(T, 1)
        for l in range(depth):
            w_tok, w_feaT = _regroup_w_in(w_in[l])
            pm, r4, ka_s, ka_w, qT_nsa, va_s, va_w, glT = _in_proj(h, attn_norm_w[l], w_tok, w_feaT, nsel_pad)
            wukv = mla_w_ukv[l].reshape(MLA_KV_RANK, MLA_HEADS, MLA_NOPE + MLA_V)
            wuk = wukv[:, :, :MLA_NOPE].reshape(MLA_KV_RANK, MLA_HEADS * MLA_NOPE).astype(BF16)
            wuvT = wukv[:, :, MLA_NOPE:].reshape(MLA_KV_RANK, MLA_HEADS * MLA_V).T.astype(BF16)
            qT, k, vT = _mla_prep(pm, pos2, mla_q_norm_w[l], mla_kv_norm_w[l], _regroup_w_uq(mla_w_uq[l]).T,
                                  wuk, wuvT, inv128)
            y_mla = _mla_attn(qT, k, vT)

            half = CMP_STRIDE * NSA_D
            pos_kv = jnp.stack([cmp_pos_k[l].reshape(2, half), cmp_pos_v[l].reshape(2, half)])
            kc, vcT = _compress(r4.reshape(2 * NSA_G, T // CMP_STRIDE, half), pos_kv,
                                jnp.stack([cmp_w1_k[l], cmp_w1_v[l]]), jnp.stack([cmp_w2_k[l], cmp_w2_v[l]]))
            o_cT, selT = _cmp_attn(qT_nsa, arows, kc, vcT, nsel_pad)
            o_sT = _sel_attn(qT_nsa, selT, arows_sel, srow_sel, ka_s, va_s)
            o_wT = _win_attn(qT_nsa, arows, srow, ka_w, va_w)

            h = _out_proj(h, y_mla, o_cT, o_sT, o_wT, glT, mla_out_norm_w[l], nsa_out_norm_w[l],
                          w_o[l].astype(BF16))
            h = _mlp(h, mlp_norm_w[l], w_up[l].astype(BF16), w_down[l].astype(BF16), final_norm_w,
                     final=(l == depth - 1))
        outs.append(h)
    return jnp.stack(outs)
```

```python
import functools

import numpy as np
import jax
import jax.numpy as jnp
from jax import lax
from jax.experimental import pallas as pl
from jax.experimental.pallas import tpu as pltpu

F32 = jnp.float32
BF16 = jnp.bfloat16

EPS = 1e-6
NEG = -1e30
REMOVED = -3e38

D_MODEL = 2048
MLA_HEADS = 8
MLA_Q_RANK = 512
MLA_KV_RANK = 256
MLA_NOPE = 128
MLA_ROPE = 64
MLA_V = 128
ROPE_THETA = 10000.0
MLA_QK_PAD = 256
MLA_VROWS = 144

NSA_HEADS = 16
NSA_G = 2
NSA_HPG = 8
NSA_D = 64
CMP_LEN = 32
CMP_STRIDE = 16
CMP_HIDDEN = 256
SEL_LEN = 64
SEL_SHIFT = 6
SEL_TOPK = 16
WINDOW = 512
FORCE_BONUS = 1e4
MLP_HIDDEN = 4 * D_MODEL

LOG2E = 1.4426950408889634
LANES = 128
SUBLANES = 8
BF16_ROWS = 16
NSA_TQ = 128
SEL_TQ = 256
SEL_TK = 512
MLA_TQ = 1024
MLA_TK = 1024
VMEM_LIMIT = 56 * 1024 * 1024

SEC_CQ = 0
SEC_CKV = 512
SEC_ROPE = 768
SEC_CMP = 896
SEC_KS = 1152
SEC_KW = 1280
TM_COLS = 1408
MLA_COLS = SEC_CMP
ROW_Q = 0
ROW_VS = 1024
ROW_VW = 1152
ROW_GATE = 1280
FM_ROWS = 1408


def _cparams(sem):
    return pltpu.CompilerParams(dimension_semantics=sem, vmem_limit_bytes=VMEM_LIMIT)


def _rms(x, w):
    return x * lax.rsqrt(jnp.mean(x * x, axis=-1, keepdims=True) + EPS) * w


ALIBI_PARTS = 4
ALIBI_HI = 16
NEG_BF16 = -1e30
SEL_VROWS = 80
NT_DIMS = (((1,), (1,)), ((), ()))
TN_DIMS = (((0,), (0,)), ((), ()))


def _digit_cols(offset, scale):
    lane = lax.broadcasted_iota(jnp.int32, offset.shape, 1)
    hi = jnp.right_shift(offset, 4) * scale
    lo = jnp.bitwise_and(offset, ALIBI_HI - 1) * scale
    return jnp.where(lane < ALIBI_PARTS, hi, jnp.where(lane < 2 * ALIBI_PARTS, lo, 0)).astype(F32)


def _in_proj_kernel(x_ref, nw_ref, w_ref, wT_ref, pm_ref, r_ref, kas_ref, kaw_ref, qT_ref, vas_ref, vaw_ref,
                    glT_ref, *, nsel_pad):
    c = pl.program_id(0)
    tm = x_ref.shape[0]
    n = _rms(x_ref[...], nw_ref[...]).astype(BF16)
    tok = jnp.dot(n, w_ref[...], preferred_element_type=F32)
    fea = lax.dot_general(wT_ref[...], n, NT_DIMS, preferred_element_type=F32)

    pm_ref[...] = tok[:, :MLA_COLS]
    for a in range(2 * NSA_G):
        r_ref[a] = tok[:, SEC_CMP + a * NSA_D:SEC_CMP + (a + 1) * NSA_D]
    r = lax.broadcasted_iota(jnp.int32, (tm, NSA_D), 0)
    blk = lax.broadcasted_iota(jnp.int32, (tm, nsel_pad), 1)
    kblk = jnp.right_shift(c * tm + lax.broadcasted_iota(jnp.int32, (tm, nsel_pad), 0), SEL_SHIFT)
    onehot = jnp.where(blk == kblk, 1.0, 0.0)
    sel_digits = _digit_cols(r, 1)
    win_digits = _digit_cols(jnp.bitwise_and(r, NSA_TQ - 1), 1)
    ones_row = jnp.where(lax.broadcasted_iota(jnp.int32, (SEL_VROWS - NSA_D, tm), 0) == 0, 1.0, 0.0)
    for g in range(NSA_G):
        ks = tok[:, SEC_KS + g * NSA_D:SEC_KS + (g + 1) * NSA_D]
        kw = tok[:, SEC_KW + g * NSA_D:SEC_KW + (g + 1) * NSA_D]
        kas_ref[g] = jnp.concatenate([ks, sel_digits, onehot], axis=-1).astype(BF16)
        kaw_ref[g] = jnp.concatenate([kw, win_digits], axis=-1).astype(BF16)
        vas_ref[g, 0] = jnp.concatenate([fea[ROW_VS + g * NSA_D:ROW_VS + (g + 1) * NSA_D, :], ones_row],
                                        axis=0).astype(BF16)
        vaw_ref[g] = jnp.concatenate([fea[ROW_VW + g * NSA_D:ROW_VW + (g + 1) * NSA_D, :], ones_row],
                                     axis=0).astype(BF16)
    qT_ref[...] = fea[ROW_Q:ROW_VS, :]
    glT_ref[...] = fea[ROW_GATE:, :]


def _in_proj(x2, attn_norm_w, w_tok, w_feaT, nsel_pad, tm=SEL_TK):
    T = x2.shape[0]
    tm = min(tm, T)
    kw = 2 * NSA_D + nsel_pad
    fixed = lambda i: (0, 0)
    return pl.pallas_call(
        functools.partial(_in_proj_kernel, nsel_pad=nsel_pad),
        out_shape=(
            jax.ShapeDtypeStruct((T, MLA_COLS), F32),
            jax.ShapeDtypeStruct((2 * NSA_G, T, NSA_D), F32),
            jax.ShapeDtypeStruct((NSA_G, T, kw), BF16),
            jax.ShapeDtypeStruct((NSA_G, T, 2 * NSA_D), BF16),
            jax.ShapeDtypeStruct((NSA_HEADS * NSA_D, T), F32),
            jax.ShapeDtypeStruct((NSA_G, T // tm, SEL_VROWS, tm), BF16),
            jax.ShapeDtypeStruct((NSA_G, SEL_VROWS, T), BF16),
            jax.ShapeDtypeStruct((LANES, T), F32),
        ),
        grid=(T // tm,),
        in_specs=[
            pl.BlockSpec((tm, D_MODEL), lambda i: (i, 0)),
            pl.BlockSpec((1, D_MODEL), fixed),
            pl.BlockSpec((D_MODEL, TM_COLS), fixed),
            pl.BlockSpec((FM_ROWS, D_MODEL), fixed),
        ],
        out_specs=(
            pl.BlockSpec((tm, MLA_COLS), lambda i: (i, 0)),
            pl.BlockSpec((2 * NSA_G, tm, NSA_D), lambda i: (0, i, 0)),
            pl.BlockSpec((NSA_G, tm, kw), lambda i: (0, i, 0)),
            pl.BlockSpec((NSA_G, tm, 2 * NSA_D), lambda i: (0, i, 0)),
            pl.BlockSpec((NSA_HEADS * NSA_D, tm), lambda i: (0, i)),
            pl.BlockSpec((NSA_G, 1, SEL_VROWS, tm), lambda i: (0, i, 0, 0)),
            pl.BlockSpec((NSA_G, SEL_VROWS, tm), lambda i: (0, 0, i)),
            pl.BlockSpec((LANES, tm), lambda i: (0, i)),
        ),
        compiler_params=_cparams(("parallel",)),
        name="in_proj",
    )(x2, attn_norm_w.reshape(1, D_MODEL), w_tok, w_feaT)


def _mla_prep_kernel(cq_ref, ckv_ref, kr_ref, pos_ref, qnw_ref, kvnw_ref, wuqT_ref, wuk_ref, wuvT_ref, inv_ref,
                     qT_ref, k_ref, vT_ref):
    scale = (MLA_NOPE + MLA_ROPE) ** -0.5 * LOG2E
    qn = _rms(cq_ref[...], qnw_ref[...]).astype(BF16)
    qT = lax.dot_general(wuqT_ref[...], qn, NT_DIMS, preferred_element_type=F32)
    kvn = _rms(ckv_ref[...], kvnw_ref[...]).astype(BF16)
    kn = jnp.dot(kvn, wuk_ref[...], preferred_element_type=F32)
    vT = lax.dot_general(wuvT_ref[...], kvn, NT_DIMS, preferred_element_type=F32)

    ang = pos_ref[...].astype(F32) * inv_ref[...]
    c = jnp.cos(ang)
    s = jnp.sin(ang)
    lane = lax.broadcasted_iota(jnp.int32, ang.shape, 1)
    coef = jnp.where(lane < 64, c, jnp.where(lane < 96, -s, s))
    coefT = coef.T

    ones_row = jnp.where(lax.broadcasted_iota(jnp.int32, (MLA_VROWS - MLA_V, coefT.shape[1]), 0) == 0, 1.0, 0.0)
    t = kr_ref[...] * coef
    kpe = jnp.where(lane < 64, t + pltpu.roll(t, 64, 1), 0.0)
    for h in range(MLA_HEADS):
        base = h * MLA_QK_PAD
        tT = qT[base + 128:base + 256, :] * coefT
        ropeT = tT + pltpu.roll(tT, 64, 0)
        qT_ref[h] = (jnp.concatenate([qT[base:base + 128, :], ropeT], axis=0) * scale).astype(BF16)
        k_ref[h] = jnp.concatenate([kn[:, h * 128:(h + 1) * 128], kpe], axis=-1).astype(BF16)
        vT_ref[h] = jnp.concatenate([vT[h * 128:(h + 1) * 128, :], ones_row], axis=0).astype(BF16)


def _mla_prep(pm, pos2, qnw, kvnw, wuqT, wuk, wuvT, inv128, tm=256):
    T = pm.shape[0]
    H = MLA_HEADS
    fixed = lambda i: (0, 0)
    return pl.pallas_call(
        _mla_prep_kernel,
        out_shape=(
            jax.ShapeDtypeStruct((H, MLA_QK_PAD, T), BF16),
            jax.ShapeDtypeStruct((H, T, MLA_QK_PAD), BF16),
            jax.ShapeDtypeStruct((H, MLA_VROWS, T), BF16),
        ),
        grid=(T // tm,),
        in_specs=[
            pl.BlockSpec((tm, MLA_Q_RANK), lambda i: (i, SEC_CQ // MLA_Q_RANK)),
            pl.BlockSpec((tm, MLA_KV_RANK), lambda i: (i, SEC_CKV // MLA_KV_RANK)),
            pl.BlockSpec((tm, LANES), lambda i: (i, SEC_ROPE // LANES)),
            pl.BlockSpec((tm, 1), lambda i: (i, 0)),
            pl.BlockSpec((1, MLA_Q_RANK), fixed),
            pl.BlockSpec((1, MLA_KV_RANK), fixed),
            pl.BlockSpec((H * MLA_QK_PAD, MLA_Q_RANK), fixed),
            pl.BlockSpec((MLA_KV_RANK, H * MLA_NOPE), fixed),
            pl.BlockSpec((H * MLA_V, MLA_KV_RANK), fixed),
            pl.BlockSpec((1, LANES), fixed),
        ],
        out_specs=(
            pl.BlockSpec((H, MLA_QK_PAD, tm), lambda i: (0, 0, i)),
            pl.BlockSpec((H, tm, MLA_QK_PAD), lambda i: (0, i, 0)),
            pl.BlockSpec((H, MLA_VROWS, tm), lambda i: (0, 0, i)),
        ),
        compiler_params=_cparams(("parallel",)),
        name="mla_prep",
    )(pm, pm, pm, pos2, qnw.reshape(1, -1), kvnw.reshape(1, -1), wuqT, wuk, wuvT, inv128)


def _mla_attn_kernel(qi_ref, ki_ref, qT_ref, k_ref, vT_ref, o_ref, m_scr, acc_scr,
                     s0_scr, s1_scr, top0_scr, top1_scr, p0_scr, p1_scr, al0_scr, al1_scr):
    p_id = pl.program_id(0)
    qi = qi_ref[p_id]
    ki = ki_ref[p_id]
    tk, tq = s0_scr.shape
    last = (qi * tq + tq - 1) // tk
    s_slots, top_slots = (s0_scr, s1_scr), (top0_scr, top1_scr)
    p_slots, al_slots = (p0_scr, p1_scr), (al0_scr, al1_scr)

    @pl.when(ki == 0)
    def _():
        m_scr[...] = jnp.full(m_scr.shape, NEG, F32)
        acc_scr[...] = jnp.zeros(acc_scr.shape, F32)

    def run(masked):
        def scores(h, slot):
            s = jnp.dot(k_ref[h], qT_ref[h], preferred_element_type=F32)
            if masked:
                key = ki * tk + lax.broadcasted_iota(jnp.int32, s.shape, 0)
                qry = qi * tq + lax.broadcasted_iota(jnp.int32, s.shape, 1)
                s = jnp.where(key <= qry, s, NEG)
            s_slots[slot][...] = s
            top_slots[slot][...] = jnp.max(s.reshape(tk // SUBLANES, SUBLANES, tq), axis=0)

        def softmax(h, slot):
            m_prev = m_scr[h]
            m_new = jnp.maximum(m_prev, jnp.max(top_slots[slot][...], axis=0, keepdims=True))
            for b in range(tk // BF16_ROWS):
                rows = slice(b * BF16_ROWS, (b + 1) * BF16_ROWS)
                p_slots[slot][rows, :] = jnp.exp2(s_slots[slot][rows, :] - m_new).astype(BF16)
            al_slots[slot][...] = jnp.exp2(m_prev - m_new)
            m_scr[h] = m_new

        def values(h, slot):
            acc = al_slots[slot][...] * acc_scr[h] + jnp.dot(vT_ref[h], p_slots[slot][...],
                                                             preferred_element_type=F32)
            acc_scr[h] = acc

        scores(0, 0)
        scores(1, 1)
        softmax(0, 0)

        def body(hh, carry):
            h = 2 * hh
            scores(h, 0)
            softmax(h - 1, 1)
            values(h - 2, 0)
            scores(h + 1, 1)
            softmax(h, 0)
            values(h - 1, 1)
            return carry

        lax.fori_loop(1, MLA_HEADS // 2, body, 0)
        softmax(MLA_HEADS - 1, 1)
        values(MLA_HEADS - 2, 0)
        values(MLA_HEADS - 1, 1)

    @pl.when(ki * tk + tk - 1 <= qi * tq)
    def _():
        run(False)

    @pl.when(ki * tk + tk - 1 > qi * tq)
    def _():
        run(True)

    @pl.when(ki == last)
    def _():
        for h in range(MLA_HEADS):
            acc = acc_scr[h]
            o_ref[h * MLA_V:(h + 1) * MLA_V, :] = acc[:MLA_V, :] * (1.0 / acc[MLA_V:MLA_V + 1, :])


def _tri_pairs(nq, per_q):
    qi, ki = [], []
    for i in range(nq):
        for c in range(per_q(i) + 1):
            qi.append(i)
            ki.append(c)
    return jnp.asarray(qi, jnp.int32), jnp.asarray(ki, jnp.int32)


def _mla_attn(qT, k, vT):
    H, T, _ = k.shape
    tq = min(MLA_TQ, T)
    tk = min(MLA_TK, T)
    qi, ki = _tri_pairs(T // tq, lambda i: (i * tq + tq - 1) // tk)
    grid_spec = pltpu.PrefetchScalarGridSpec(
        num_scalar_prefetch=2,
        grid=(int(qi.shape[0]),),
        in_specs=[
            pl.BlockSpec((H, MLA_QK_PAD, tq), lambda p, qi, ki: (0, 0, qi[p])),
            pl.BlockSpec((H, tk, MLA_QK_PAD), lambda p, qi, ki: (0, ki[p], 0)),
            pl.BlockSpec((H, MLA_VROWS, tk), lambda p, qi, ki: (0, 0, ki[p])),
        ],
        out_specs=pl.BlockSpec((H * MLA_V, tq), lambda p, qi, ki: (0, qi[p])),
        scratch_shapes=[
            pltpu.VMEM((H, 1, tq), F32),
            pltpu.VMEM((H, MLA_VROWS, tq), F32),
            pltpu.VMEM((tk, tq), F32),
            pltpu.VMEM((tk, tq), F32),
            pltpu.VMEM((SUBLANES, tq), F32),
            pltpu.VMEM((SUBLANES, tq), F32),
            pltpu.VMEM((tk, tq), BF16),
            pltpu.VMEM((tk, tq), BF16),
            pltpu.VMEM((1, tq), F32),
            pltpu.VMEM((1, tq), F32),
        ],
    )
    return pl.pallas_call(
        _mla_attn_kernel,
        out_shape=jax.ShapeDtypeStruct((H * MLA_V, T), F32),
        grid_spec=grid_spec,
        compiler_params=_cparams(("arbitrary",)),
        name="mla_attn",
    )(qi, ki, qT, k, vT)


def _gelu_tanh(x):
    return 0.5 * x * (1.0 + jnp.tanh(np.sqrt(2.0 / np.pi).astype(np.float32) * (x + 0.044715 * (x * x * x))))


def _compress_kernel(r_ref, pos_ref, w1_ref, w2_ref, kc_ref, vcT_ref):
    r = r_ref[0]
    nr = r.shape[0]
    half = CMP_STRIDE * NSA_D
    hp = lax.Precision.HIGHEST
    a = jnp.dot(r + pos_ref[0, 0:1, :], w1_ref[0, :half, :], precision=hp, preferred_element_type=F32)
    b = jnp.dot(r + pos_ref[0, 1:2, :], w1_ref[0, half:, :], precision=hp, preferred_element_type=F32)
    hid = _gelu_tanh(a + pltpu.roll(b, nr - 1, 0))
    out = jnp.dot(hid, w2_ref[0], precision=hp, preferred_element_type=F32)
    row = lax.broadcasted_iota(jnp.int32, out.shape, 0)
    out = jnp.where(row < nr - 1, out, 0.0)
    is_key = pl.program_id(0) % 2 == 0

    @pl.when(is_key)
    def _():
        kc_ref[0] = jnp.concatenate([out, _digit_cols(row, CMP_STRIDE)], axis=-1).astype(BF16)

    @pl.when(jnp.logical_not(is_key))
    def _():
        vcT_ref[0] = jnp.concatenate([out, jnp.zeros_like(out)], axis=-1).T[0:NSA_D, :].astype(BF16)


def _compress(r4, pos_kv, w1_kv, w2_kv):
    _, nr, width = r4.shape
    return pl.pallas_call(
        _compress_kernel,
        out_shape=(
            jax.ShapeDtypeStruct((NSA_G, nr, 2 * NSA_D), BF16),
            jax.ShapeDtypeStruct((NSA_G, NSA_D, nr), BF16),
        ),
        grid=(2 * NSA_G,),
        in_specs=[
            pl.BlockSpec((1, nr, width), lambda a: (a, 0, 0)),
            pl.BlockSpec((1, 2, width), lambda a: (a % 2, 0, 0)),
            pl.BlockSpec((1, 2 * width, CMP_HIDDEN), lambda a: (a % 2, 0, 0)),
            pl.BlockSpec((1, CMP_HIDDEN, NSA_D), lambda a: (a % 2, 0, 0)),
        ],
        out_specs=(
            pl.BlockSpec((1, nr, 2 * NSA_D), lambda a: (a // 2, 0, 0)),
            pl.BlockSpec((1, NSA_D, nr), lambda a: (a // 2, 0, 0)),
        ),
        compiler_params=_cparams(("arbitrary",)),
        name="nsa_compress",
    )(r4, pos_kv, w1_kv, w2_kv)


def _alibi_rows(tq):
    start = np.float32(2.0 ** (-8.0 / NSA_HEADS))
    slopes = (start ** np.arange(1, NSA_HEADS + 1, dtype=np.float32)).astype(np.float32)
    slopes = (slopes.astype(np.float64) * LOG2E).astype(np.float32)
    parts, rest = [], slopes
    for _ in range(ALIBI_PARTS):
        piece = (rest.view(np.uint32) & np.uint32(0xFFFF0000)).view(np.float32)
        parts.append(piece)
        rest = rest - piece
    assert not rest.any()
    pieces = np.stack(parts)
    rows = np.concatenate([pieces * ALIBI_HI, pieces, np.zeros((NSA_D - 2 * ALIBI_PARTS, NSA_HEADS), np.float32)])
    rows = np.repeat(rows.reshape(NSA_D, NSA_G, NSA_HPG), tq, axis=2).transpose(1, 0, 2)
    srow = np.repeat(slopes.reshape(NSA_G, 1, NSA_HPG), tq, axis=2)
    return jnp.asarray(rows, BF16), jnp.asarray(srow, F32)


def _nsa_queries(qT_ref, arow_ref):
    q = qT_ref[...] * (NSA_D ** -0.5 * LOG2E)
    top = jnp.concatenate([q[h * NSA_D:(h + 1) * NSA_D, :] for h in range(NSA_HPG)], axis=1).astype(BF16)
    return jnp.concatenate([top, arow_ref[0]], axis=0)


def _heads_to_rows(o):
    tq = o.shape[1] // NSA_HPG
    return jnp.concatenate([o[:, h * tq:(h + 1) * tq] for h in range(NSA_HPG)], axis=0)


def _token_pos(i, tq):
    return i * tq + jnp.bitwise_and(lax.broadcasted_iota(jnp.int32, (1, NSA_HPG * tq), 1), tq - 1)


def _bf16_pieces(x, n):
    pieces = []
    for _ in range(n - 1):
        top = pltpu.bitcast(jnp.bitwise_and(pltpu.bitcast(x, jnp.uint32), jnp.uint32(0xFFFF0000)), F32)
        pieces.append(top.astype(BF16))
        x = x - top
    return pieces + [x.astype(BF16)]


CMP_BUCKET = 128


def _cmp_attn_kernel(qT_ref, arow_ref, kc_ref, vcT_ref, ov_ref, o_ref, selT_ref, imp_scr, *, nsel_pad):
    i = pl.program_id(1)
    tq = NSA_TQ
    n = NSA_HPG * tq
    nr = kc_ref.shape[1]
    tpos = _token_pos(i, tq)
    last_done = jnp.right_shift(tpos - (CMP_LEN - 1), 4)

    def attend(rows):
        s = jnp.dot(kc_ref[0, 0:rows, :], _nsa_queries(qT_ref, arow_ref), preferred_element_type=F32)
        valid = lax.broadcasted_iota(jnp.int32, (rows, n), 0) <= last_done
        s = jnp.where(valid, s, NEG)
        e = jnp.where(valid, jnp.exp2(s - jnp.max(s, axis=0, keepdims=True)), 0.0)
        l = jnp.sum(e, axis=0, keepdims=True)
        inv = 1.0 / jnp.where(l > 0.0, l, 1.0)
        acc = jnp.dot(vcT_ref[0, :, 0:rows], e.astype(BF16), preferred_element_type=F32)
        o_ref[...] = _heads_to_rows(acc * inv)

        pn = e * inv
        psum = pn[:, 0:tq]
        for h in range(1, NSA_HPG):
            psum = psum + pn[:, h * tq:(h + 1) * tq]
        imp_scr[...] = sum(jnp.dot(ov_ref[:, 0:rows], piece, preferred_element_type=F32)
                           for piece in _bf16_pieces(psum, 3))

    n_buckets = -(-nr // CMP_BUCKET)
    bucket = jnp.maximum((i * tq + tq - CMP_LEN) // CMP_STRIDE, 0) // CMP_BUCKET
    for b in range(n_buckets):
        pl.when(bucket == b)(functools.partial(attend, min((b + 1) * CMP_BUCKET, nr)))
    imp = imp_scr[...]

    blk = lax.broadcasted_iota(jnp.int32, (nsel_pad, tq), 0)
    cur = jnp.right_shift(i * tq + lax.broadcasted_iota(jnp.int32, (nsel_pad, tq), 1), SEL_SHIFT)
    forced = jnp.logical_or(blk == 0, jnp.logical_or(blk == cur, blk == cur - 1))
    score = jnp.where(blk <= cur, imp + jnp.where(forced, FORCE_BONUS, 0.0), NEG)
    blk_f = blk.astype(F32)

    def pick(_, carry):
        sc, sel = carry
        mx = jnp.max(sc, axis=0, keepdims=True)
        first = jnp.min(jnp.where(sc == mx, blk_f, float(nsel_pad)), axis=0, keepdims=True)
        hit = blk_f == first
        sel = jnp.where(jnp.logical_and(hit, mx > NEG * 0.5), 1.0, sel)
        return jnp.where(hit, REMOVED, sc), sel

    _, sel = lax.fori_loop(0, SEL_TOPK, pick, (score, jnp.zeros((nsel_pad, tq), F32)))
    selT_ref[0] = sel


def _overlap_matrix(nsel_pad, nr):
    j = np.arange(nsel_pad)[:, None]
    n = np.arange(nr)[None, :]
    ov = (n * CMP_STRIDE <= j * SEL_LEN + (SEL_LEN - 1)) & (n * CMP_STRIDE + (CMP_LEN - 1) >= j * SEL_LEN) & (n < nr - 1)
    return jnp.asarray(ov, BF16)


def _cmp_attn(fm, arows, kc, vcT, nsel_pad):
    T = fm.shape[1]
    nr = kc.shape[1]
    tq = NSA_TQ
    n = NSA_HPG * tq
    return pl.pallas_call(
        functools.partial(_cmp_attn_kernel, nsel_pad=nsel_pad),
        out_shape=(
            jax.ShapeDtypeStruct((NSA_HEADS * NSA_D, T), F32),
            jax.ShapeDtypeStruct((NSA_G, nsel_pad, T), F32),
        ),
        grid=(NSA_G, T // tq),
        in_specs=[
            pl.BlockSpec((NSA_HPG * NSA_D, tq), lambda g, i: (g, i)),
            pl.BlockSpec((1, NSA_D, n), lambda g, i: (g, 0, 0)),
            pl.BlockSpec((1, nr, 2 * NSA_D), lambda g, i: (g, 0, 0)),
            pl.BlockSpec((1, NSA_D, nr), lambda g, i: (g, 0, 0)),
            pl.BlockSpec((nsel_pad, nr), lambda g, i: (0, 0)),
        ],
        out_specs=(
            pl.BlockSpec((NSA_HPG * NSA_D, tq), lambda g, i: (g, i)),
            pl.BlockSpec((1, nsel_pad, tq), lambda g, i: (g, 0, i)),
        ),
        scratch_shapes=[pltpu.VMEM((nsel_pad, tq), F32)],
        compiler_params=_cparams(("parallel", "parallel")),
        name="nsa_cmp_attn",
    )(fm, arows, kc, vcT, _overlap_matrix(nsel_pad, nr))


def _sel_attn_kernel(qT_ref, selT_ref, arow_ref, srow_ref, ka_ref, va_ref, o_ref,
                     qa_scr, s0_scr, s1_scr, top0_scr, top1_scr, p0_scr, p1_scr, al0_scr, al1_scr, m_scr, acc_scr,
                     *, tk, n_chunks):
    i = pl.program_id(1)
    tq = SEL_TQ
    n = NSA_HPG * tq
    nsel_pad = selT_ref.shape[1]
    qa_scr[0:2 * NSA_D, :] = _nsa_queries(qT_ref, arow_ref)
    blk = lax.broadcasted_iota(jnp.int32, (nsel_pad, tq), 0)
    own = blk // (tq // SEL_LEN) == i
    chosen = jnp.logical_and(selT_ref[0] > 0.5, jnp.logical_not(own))
    qa_scr[2 * NSA_D:, :] = jnp.concatenate([jnp.where(chosen, 0.0, NEG_BF16).astype(BF16)] * NSA_HPG, axis=1)

    tpos = _token_pos(i, tq)
    slope_row = srow_ref[0]

    def rterm(c):
        return slope_row * (c * tk - tpos).astype(F32)

    c_own = (i * tq) // tk
    own_start = pl.multiple_of(i * tq, tq)
    s = jnp.dot(ka_ref[0, pl.ds(own_start, tq), 0:2 * NSA_D], qa_scr[0:2 * NSA_D, :], preferred_element_type=F32)
    kpos = i * tq + lax.broadcasted_iota(jnp.int32, s.shape, 0)
    s = jnp.where(kpos <= tpos, s, NEG)
    r_own = rterm(c_own)
    m0 = jnp.max(s, axis=0, keepdims=True) + r_own
    p = jnp.exp2(s - (m0 - r_own)).astype(BF16)
    v_own = va_ref[0, c_own, :, pl.ds(pl.multiple_of(i * tq - c_own * tk, tq), tq)]
    acc_scr[...] = jnp.dot(v_own, p, preferred_element_type=F32)
    m_scr[...] = m0

    s_slots = (s0_scr, s1_scr)
    p_slots = (p0_scr, p1_scr)
    al_slots = (al0_scr, al1_scr)

    top_slots = (top0_scr, top1_scr)

    def scores(j, slot):
        start = pl.multiple_of(j * tk, tk)
        s = jnp.dot(ka_ref[0, pl.ds(start, tk), :], qa_scr[...], preferred_element_type=F32)
        s_slots[slot][...] = s
        top_slots[slot][...] = jnp.max(s.reshape(tk // SUBLANES, SUBLANES, n), axis=0)

    def softmax(j, slot):
        r = rterm(j)
        m_prev = m_scr[...]
        s_ref, p_ref = s_slots[slot], p_slots[slot]
        m_new = jnp.maximum(m_prev, jnp.max(top_slots[slot][...], axis=0, keepdims=True) + r)
        shift = m_new - r
        for b in range(tk // BF16_ROWS):
            rows = slice(b * BF16_ROWS, (b + 1) * BF16_ROWS)
            p_ref[rows, :] = jnp.exp2(s_ref[rows, :] - shift).astype(BF16)
        al_slots[slot][...] = jnp.exp2(m_prev - m_new)
        m_scr[...] = m_new

    def values(j, slot):
        acc_scr[...] = (al_slots[slot][...] * acc_scr[...]
                        + jnp.dot(va_ref[0, j], p_slots[slot][...], preferred_element_type=F32))

    n_pairs = jnp.minimum(((i * tq + tq - 1) // tk + 2) // 2, n_chunks // 2)
    scores(0, 0)
    scores(1, 1)
    softmax(0, 0)

    def body(jj, carry):
        j = 2 * jj
        scores(j, 0)
        softmax(j - 1, 1)
        values(j - 2, 0)
        scores(j + 1, 1)
        softmax(j, 0)
        values(j - 1, 1)
        return carry

    lax.fori_loop(1, n_pairs, body, 0)
    softmax(2 * n_pairs - 1, 1)
    values(2 * n_pairs - 2, 0)
    values(2 * n_pairs - 1, 1)
    acc = acc_scr[...]
    o_ref[...] = _heads_to_rows(acc[0:NSA_D, :] * (1.0 / acc[NSA_D:NSA_D + 1, :]))


def _sel_attn(fm, selT, arows, srow, ka, va):
    _, T, kw = ka.shape
    tk = va.shape[-1]
    tq = min(SEL_TQ, T)
    nsel_pad = selT.shape[1]
    n = NSA_HPG * tq
    return pl.pallas_call(
        functools.partial(_sel_attn_kernel, tk=tk, n_chunks=T // tk),
        out_shape=jax.ShapeDtypeStruct((NSA_HEADS * NSA_D, T), F32),
        grid=(NSA_G, T // tq),
        in_specs=[
            pl.BlockSpec((NSA_HPG * NSA_D, tq), lambda g, i: (g, i)),
            pl.BlockSpec((1, nsel_pad, tq), lambda g, i: (g, 0, i)),
            pl.BlockSpec((1, NSA_D, n), lambda g, i: (g, 0, 0)),
            pl.BlockSpec((1, 1, n), lambda g, i: (g, 0, 0)),
            pl.BlockSpec((1, T, kw), lambda g, i: (g, 0, 0)),
            pl.BlockSpec((1, T // tk, SEL_VROWS, tk), lambda g, i: (g, 0, 0, 0)),
        ],
        out_specs=pl.BlockSpec((NSA_HPG * NSA_D, tq), lambda g, i: (g, i)),
        scratch_shapes=[
            pltpu.VMEM((kw, n), BF16),
            pltpu.VMEM((tk, n), F32),
            pltpu.VMEM((tk, n), F32),
            pltpu.VMEM((SUBLANES, n), F32),
            pltpu.VMEM((SUBLANES, n), F32),
            pltpu.VMEM((tk, n), BF16),
            pltpu.VMEM((tk, n), BF16),
            pltpu.VMEM((1, n), F32),
            pltpu.VMEM((1, n), F32),
            pltpu.VMEM((1, n), F32),
            pltpu.VMEM((SEL_VROWS, n), F32),
        ],
        compiler_params=_cparams(("parallel", "arbitrary")),
        name="nsa_sel_attn",
    )(fm, selT, arows, srow, ka, va)


WIN_KEYS = WINDOW + NSA_TQ


def _win_attn_kernel(qT_ref, arow_ref, srow_ref, ka_ref, va_ref, o_ref):
    i = pl.program_id(1)
    tq = NSA_TQ
    n = NSA_HPG * tq
    n_tiles = WIN_KEYS // tq
    start = pl.multiple_of(jnp.maximum(i * tq - WINDOW, 0), tq)
    tpos = _token_pos(i, tq)
    slope_row = srow_ref[0]

    def run(edges_only):
        s = jnp.dot(ka_ref[0, pl.ds(start, WIN_KEYS), :], _nsa_queries(qT_ref, arow_ref),
                    preferred_element_type=F32)
        tiles = [s[j * tq:(j + 1) * tq, :] for j in range(n_tiles)]
        if edges_only:
            k_first = start + lax.broadcasted_iota(jnp.int32, tiles[0].shape, 0)
            tiles[0] = jnp.where(k_first > tpos - WINDOW, tiles[0], NEG)
            k_last = start + (n_tiles - 1) * tq + lax.broadcasted_iota(jnp.int32, tiles[-1].shape, 0)
            tiles[-1] = jnp.where(k_last <= tpos, tiles[-1], NEG)
        else:
            kpos = start + lax.broadcasted_iota(jnp.int32, s.shape, 0)
            s = jnp.where(jnp.logical_and(kpos <= tpos, kpos > tpos - WINDOW), s, NEG)
            tiles = [s[j * tq:(j + 1) * tq, :] for j in range(n_tiles)]
        rterms = [slope_row * (start + j * tq - tpos).astype(F32) for j in range(n_tiles)]
        m = jnp.max(tiles[0], axis=0, keepdims=True) + rterms[0]
        for sj, rj in zip(tiles[1:], rterms[1:]):
            m = jnp.maximum(m, jnp.max(sj, axis=0, keepdims=True) + rj)
        p = jnp.concatenate([jnp.exp2(sj - (m - rj)) for sj, rj in zip(tiles, rterms)], axis=0).astype(BF16)
        acc = jnp.dot(va_ref[0, :, pl.ds(start, WIN_KEYS)], p, preferred_element_type=F32)
        o_ref[...] = _heads_to_rows(acc[0:NSA_D, :] * (1.0 / acc[NSA_D:NSA_D + 1, :]))

    @pl.when(i * tq >= WINDOW)
    def _():
        run(True)

    @pl.when(i * tq < WINDOW)
    def _():
        run(False)


def _win_attn(fm, arows, srow, ka, va):
    _, T, kw = ka.shape
    assert T >= WIN_KEYS
    tq = NSA_TQ
    n = NSA_HPG * tq
    return pl.pallas_call(
        _win_attn_kernel,
        out_shape=jax.ShapeDtypeStruct((NSA_HEADS * NSA_D, T), F32),
        grid=(NSA_G, T // tq),
        in_specs=[
            pl.BlockSpec((NSA_HPG * NSA_D, tq), lambda g, i: (g, i)),
            pl.BlockSpec((1, NSA_D, n), lambda g, i: (g, 0, 0)),
            pl.BlockSpec((1, 1, n), lambda g, i: (g, 0, 0)),
            pl.BlockSpec((1, T, kw), lambda g, i: (g, 0, 0)),
            pl.BlockSpec((1, SEL_VROWS, T), lambda g, i: (g, 0, 0)),
        ],
        out_specs=pl.BlockSpec((NSA_HPG * NSA_D, tq), lambda g, i: (g, i)),
        compiler_params=_cparams(("parallel", "parallel")),
        name="nsa_win_attn",
    )(fm, arows, srow, ka, va)


def _out_proj_kernel(x_ref, ymT_ref, ocT_ref, osT_ref, owT_ref, glT_ref, mnw_ref, nnw_ref, wo_ref, o_ref):
    gates = 1.0 / (1.0 + jnp.exp(-glT_ref[...]))
    heads = []
    for h in range(NSA_HEADS):
        rows = slice(h * NSA_D, (h + 1) * NSA_D)
        heads.append(gates[3 * h:3 * h + 1, :] * ocT_ref[rows, :]
                     + gates[3 * h + 1:3 * h + 2, :] * osT_ref[rows, :]
                     + gates[3 * h + 2:3 * h + 3, :] * owT_ref[rows, :])
    ynT = jnp.concatenate(heads, axis=0)

    def rms_cols(y, w_col):
        return (y * lax.rsqrt(jnp.mean(y * y, axis=0, keepdims=True) + EPS) * w_col).astype(BF16)

    half = MLA_HEADS * MLA_V
    o_ref[...] = (x_ref[...]
                  + lax.dot_general(rms_cols(ymT_ref[...], mnw_ref[...]), wo_ref[:half, :], TN_DIMS,
                                    preferred_element_type=F32)
                  + lax.dot_general(rms_cols(ynT, nnw_ref[...]), wo_ref[half:, :], TN_DIMS,
                                    preferred_element_type=F32))


def _out_proj(x2, y_mlaT, o_cT, o_sT, o_wT, glT, mnw, nnw, wo, tm=512):
    T = x2.shape[0]
    half = MLA_HEADS * MLA_V
    row = lambda i: (i, 0)
    col = lambda i: (0, i)
    fixed2 = lambda i: (0, 0)
    return pl.pallas_call(
        _out_proj_kernel,
        out_shape=jax.ShapeDtypeStruct((T, D_MODEL), F32),
        grid=(T // tm,),
        in_specs=[
            pl.BlockSpec((tm, D_MODEL), row),
            pl.BlockSpec((half, tm), col),
            pl.BlockSpec((half, tm), col),
            pl.BlockSpec((half, tm), col),
            pl.BlockSpec((half, tm), col),
            pl.BlockSpec((LANES, tm), col),
            pl.BlockSpec((half, 1), fixed2),
            pl.BlockSpec((half, 1), fixed2),
            pl.BlockSpec((2 * half, D_MODEL), fixed2),
        ],
        out_specs=pl.BlockSpec((tm, D_MODEL), row),
        compiler_params=_cparams(("parallel",)),
        name="out_proj",
    )(x2, y_mlaT, o_cT, o_sT, o_wT, glT, mnw.reshape(-1, 1), nnw.reshape(-1, 1), wo)


def _mlp_kernel(h_ref, nw_ref, wu_ref, wd_ref, fw_ref, o_ref, n_scr, acc_scr, *, final):
    c = pl.program_id(1)

    @pl.when(c == 0)
    def _():
        n_scr[...] = _rms(h_ref[...], nw_ref[...]).astype(BF16)
        acc_scr[...] = jnp.zeros(acc_scr.shape, F32)

    u = jnp.dot(n_scr[...], wu_ref[...], preferred_element_type=F32)
    a = jnp.square(jnp.maximum(u, 0.0)).astype(BF16)
    acc_scr[...] += jnp.dot(a, wd_ref[...], preferred_element_type=F32)

    @pl.when(c == pl.num_programs(1) - 1)
    def _():
        h2 = h_ref[...] + acc_scr[...]
        o_ref[...] = _rms(h2, fw_ref[...]) if final else h2


def _mlp(h1, nw, wu, wd, fw, final, tm=512, tc=1024):
    T = h1.shape[0]
    tm = min(tm, T)
    return pl.pallas_call(
        functools.partial(_mlp_kernel, final=final),
        out_shape=jax.ShapeDtypeStruct((T, D_MODEL), F32),
        grid=(T // tm, MLP_HIDDEN // tc),
        in_specs=[
            pl.BlockSpec((tm, D_MODEL), lambda i, c: (i, 0)),
            pl.BlockSpec((1, D_MODEL), lambda i, c: (0, 0)),
            pl.BlockSpec((D_MODEL, tc), lambda i, c: (0, c)),
            pl.BlockSpec((tc, D_MODEL), lambda i, c: (c, 0)),
            pl.BlockSpec((1, D_MODEL), lambda i, c: (0, 0)),
        ],
        out_specs=pl.BlockSpec((tm, D_MODEL), lambda i, c: (i, 0)),
        scratch_shapes=[pltpu.VMEM((tm, D_MODEL), BF16), pltpu.VMEM((tm, D_MODEL), F32)],
        compiler_params=_cparams(("parallel", "arbitrary")),
        name="mlp",
    )(h1, nw.reshape(1, -1), wu, wd, fw.reshape(1, -1))


def _regroup_w_in(w_in):
    cq = w_in[:, 0:512]
    ckv = w_in[:, 512:768]
    kr = w_in[:, 768:832]
    q = w_in[:, 832:1856]
    kv_cmp = w_in[:, 1856:2112]
    kv_slc = w_in[:, 2112:2368].reshape(-1, NSA_G, 2, NSA_D)
    kv_win = w_in[:, 2368:2624].reshape(-1, NSA_G, 2, NSA_D)
    gates = w_in[:, 2624:2672]
    x1, x2 = kr[:, :32], kr[:, 32:]
    flat = lambda t: t.reshape(t.shape[0], NSA_G * NSA_D)
    w_tok = jnp.concatenate([cq, ckv, x1, x2, x2, x1, kv_cmp, flat(kv_slc[:, :, 0]), flat(kv_win[:, :, 0])], axis=1)
    zeros = jnp.zeros((w_in.shape[0], FM_ROWS - ROW_GATE - gates.shape[1]), w_in.dtype)
    w_fea = jnp.concatenate([q, flat(kv_slc[:, :, 1]), flat(kv_win[:, :, 1]), gates, zeros], axis=1)
    return w_tok.astype(BF16), w_fea.T.astype(BF16)


def _regroup_w_uq(w_uq):
    w = w_uq.reshape(MLA_Q_RANK, MLA_HEADS, MLA_NOPE + MLA_ROPE)
    nope = w[:, :, :MLA_NOPE]
    x1 = w[:, :, MLA_NOPE:MLA_NOPE + 32]
    x2 = w[:, :, MLA_NOPE + 32:]
    return jnp.concatenate([nope, x1, x2, x2, x1], axis=-1).reshape(MLA_Q_RANK, MLA_HEADS * MLA_QK_PAD).astype(BF16)


def kernel(x, positions, attn_norm_w, w_in, mla_q_norm_w, mla_w_uq, mla_kv_norm_w, mla_w_ukv, cmp_pos_k,
           cmp_pos_v, cmp_w1_k, cmp_w2_k, cmp_w1_v, cmp_w2_v, mla_out_norm_w, nsa_out_norm_w, w_o, mlp_norm_w,
           w_up, w_down, final_norm_w):
    B, T, _ = x.shape
    depth = w_in.shape[0]
    inv = 1.0 / (ROPE_THETA ** (jnp.arange(0, MLA_ROPE, 2, dtype=F32) / MLA_ROPE))
    inv128 = jnp.tile(inv, 4).reshape(1, LANES)
    arows, srow = _alibi_rows(NSA_TQ)
    arows_sel, srow_sel = _alibi_rows(SEL_TQ)
    nsel_pad = -(-(T // SEL_LEN) // LANES) * LANES

    outs = []
    for b in range(B):
        h = x[b]
        pos2 = positions[b].reshape(T, 1)
        for l in range(depth):
            w_tok, w_feaT = _regroup_w_in(w_in[l])
            pm, r4, ka_s, ka_w, qT_nsa, va_s, va_w, glT = _in_proj(h, attn_norm_w[l], w_tok, w_feaT, nsel_pad)
            wukv = mla_w_ukv[l].reshape(MLA_KV_RANK, MLA_HEADS, MLA_NOPE + MLA_V)
            wuk = wukv[:, :, :MLA_NOPE].reshape(MLA_KV_RANK, MLA_HEADS * MLA_NOPE).astype(BF16)
            wuvT = wukv[:, :, MLA_NOPE:].reshape(MLA_KV_RANK, MLA_HEADS * MLA_V).T.astype(BF16)
            qT, k, vT = _mla_prep(pm, pos2, mla_q_norm_w[l], mla_kv_norm_w[l], _regroup_w_uq(mla_w_uq[l]).T,
                                  wuk, wuvT, inv128)
            y_mla = _mla_attn(qT, k, vT)

            half = CMP_STRIDE * NSA_D
            pos_kv = jnp.stack([cmp_pos_k[l].reshape(2, half), cmp_pos_v[l].reshape(2, half)])
            kc, vcT = _compress(r4.reshape(2 * NSA_G, T // CMP_STRIDE, half), pos_kv,
                                jnp.stack([cmp_w1_k[l], cmp_w1_v[l]]), jnp.stack([cmp_w2_k[l], cmp_w2_v[l]]))
            o_cT, selT = _cmp_attn(qT_nsa, arows, kc, vcT, nsel_pad)
            o_sT = _sel_attn(qT_nsa, selT, arows_sel, srow_sel, ka_s, va_s)
            o_wT = _win_attn(qT_nsa, arows, srow, ka_w, va_w)

            h = _out_proj(h, y_mla, o_cT, o_sT, o_wT, glT, mla_out_norm_w[l], nsa_out_norm_w[l],
                          w_o[l].astype(BF16))
            h = _mlp(h, mlp_norm_w[l], w_up[l].astype(BF16), w_down[l].astype(BF16), final_norm_w,
                     final=(l == depth - 1))
        outs.append(h)
    return jnp.stack(outs)
```

```python
import functools

import numpy as np
import jax
import jax.numpy as jnp
from jax import lax
from jax.experimental import pallas as pl
from jax.experimental.pallas import tpu as pltpu

F32 = jnp.float32
BF16 = jnp.bfloat16

EPS = 1e-6
NEG = -1e30
REMOVED = -3e38

D_MODEL = 2048
MLA_HEADS = 8
MLA_Q_RANK = 512
MLA_KV_RANK = 256
MLA_NOPE = 128
MLA_ROPE = 64
MLA_V = 128
ROPE_THETA = 10000.0
MLA_QK_PAD = 256
MLA_VROWS = 144

NSA_HEADS = 16
NSA_G = 2
NSA_HPG = 8
NSA_D = 64
CMP_LEN = 32
CMP_STRIDE = 16
CMP_HIDDEN = 256
SEL_LEN = 64
SEL_SHIFT = 6
SEL_TOPK = 16
WINDOW = 512
FORCE_BONUS = 1e4
MLP_HIDDEN = 4 * D_MODEL

LOG2E = 1.4426950408889634
LANES = 128
SUBLANES = 8
BF16_ROWS = 16
NSA_TQ = 128
SEL_TQ = 256
SEL_TK = 512
MLA_TQ = 1024
MLA_TK = 1024
VMEM_LIMIT = 56 * 1024 * 1024

SEC_CQ = 0
SEC_CKV = 512
SEC_ROPE = 768
SEC_CMP = 896
SEC_KS = 1152
SEC_KW = 1280
TM_COLS = 1408
MLA_COLS = SEC_CMP
ROW_Q = 0
ROW_VS = 1024
ROW_VW = 1152
ROW_GATE = 1280
FM_ROWS = 1408


def _cparams(sem):
    return pltpu.CompilerParams(dimension_semantics=sem, vmem_limit_bytes=VMEM_LIMIT)


def _rms(x, w):
    return x * lax.rsqrt(jnp.mean(x * x, axis=-1, keepdims=True) + EPS) * w


ALIBI_PARTS = 4
ALIBI_HI = 16
NEG_BF16 = -1e30
SEL_VROWS = 80
NT_DIMS = (((1,), (1,)), ((), ()))
TN_DIMS = (((0,), (0,)), ((), ()))


def _digit_cols(offset, scale):
    lane = lax.broadcasted_iota(jnp.int32, offset.shape, 1)
    hi = jnp.right_shift(offset, 4) * scale
    lo = jnp.bitwise_and(offset, ALIBI_HI - 1) * scale
    return jnp.where(lane < ALIBI_PARTS, hi, jnp.where(lane < 2 * ALIBI_PARTS, lo, 0)).astype(F32)


def _in_proj_kernel(x_ref, nw_ref, w_ref, wT_ref, pm_ref, r_ref, kas_ref, kaw_ref, qT_ref, vas_ref, vaw_ref,
                    glT_ref, *, nsel_pad):
    c = pl.program_id(0)
    tm = x_ref.shape[0]
    n = _rms(x_ref[...], nw_ref[...]).astype(BF16)
    tok = jnp.dot(n, w_ref[...], preferred_element_type=F32)
    fea = lax.dot_general(wT_ref[...], n, NT_DIMS, preferred_element_type=F32)

    pm_ref[...] = tok[:, :MLA_COLS]
    for a in range(2 * NSA_G):
        r_ref[a] = tok[:, SEC_CMP + a * NSA_D:SEC_CMP + (a + 1) * NSA_D]
    r = lax.broadcasted_iota(jnp.int32, (tm, NSA_D), 0)
    blk = lax.broadcasted_iota(jnp.int32, (tm, nsel_pad), 1)
    kblk = jnp.right_shift(c * tm + lax.broadcasted_iota(jnp.int32, (tm, nsel_pad), 0), SEL_SHIFT)
    onehot = jnp.where(blk == kblk, 1.0, 0.0)
    sel_digits = _digit_cols(r, 1)
    win_digits = _digit_cols(jnp.bitwise_and(r, NSA_TQ - 1), 1)
    ones_row = jnp.where(lax.broadcasted_iota(jnp.int32, (SEL_VROWS - NSA_D, tm), 0) == 0, 1.0, 0.0)
    for g in range(NSA_G):
        ks = tok[:, SEC_KS + g * NSA_D:SEC_KS + (g + 1) * NSA_D]
        kw = tok[:, SEC_KW + g * NSA_D:SEC_KW + (g + 1) * NSA_D]
        kas_ref[g] = jnp.concatenate([ks, sel_digits, onehot], axis=-1).astype(BF16)
        kaw_ref[g] = jnp.concatenate([kw, win_digits], axis=-1).astype(BF16)
        vas_ref[g, 0] = jnp.concatenate([fea[ROW_VS + g * NSA_D:ROW_VS + (g + 1) * NSA_D, :], ones_row],
                                        axis=0).astype(BF16)
        vaw_ref[g] = jnp.concatenate([fea[ROW_VW + g * NSA_D:ROW_VW + (g + 1) * NSA_D, :], ones_row],
                                     axis=0).astype(BF16)
    qT_ref[...] = fea[ROW_Q:ROW_VS, :]
    glT_ref[...] = fea[ROW_GATE:, :]


def _in_proj(x2, attn_norm_w, w_tok, w_feaT, nsel_pad, tm=SEL_TK):
    T = x2.shape[0]
    tm = min(tm, T)
    kw = 2 * NSA_D + nsel_pad
    fixed = lambda i: (0, 0)
    return pl.pallas_call(
        functools.partial(_in_proj_kernel, nsel_pad=nsel_pad),
        out_shape=(
            jax.ShapeDtypeStruct((T, MLA_COLS), F32),
            jax.ShapeDtypeStruct((2 * NSA_G, T, NSA_D), F32),
            jax.ShapeDtypeStruct((NSA_G, T, kw), BF16),
            jax.ShapeDtypeStruct((NSA_G, T, 2 * NSA_D), BF16),
            jax.ShapeDtypeStruct((NSA_HEADS * NSA_D, T), F32),
            jax.ShapeDtypeStruct((NSA_G, T // tm, SEL_VROWS, tm), BF16),
            jax.ShapeDtypeStruct((NSA_G, SEL_VROWS, T), BF16),
            jax.ShapeDtypeStruct((LANES, T), F32),
        ),
        grid=(T // tm,),
        in_specs=[
            pl.BlockSpec((tm, D_MODEL), lambda i: (i, 0)),
            pl.BlockSpec((1, D_MODEL), fixed),
            pl.BlockSpec((D_MODEL, TM_COLS), fixed),
            pl.BlockSpec((FM_ROWS, D_MODEL), fixed),
        ],
        out_specs=(
            pl.BlockSpec((tm, MLA_COLS), lambda i: (i, 0)),
            pl.BlockSpec((2 * NSA_G, tm, NSA_D), lambda i: (0, i, 0)),
            pl.BlockSpec((NSA_G, tm, kw), lambda i: (0, i, 0)),
            pl.BlockSpec((NSA_G, tm, 2 * NSA_D), lambda i: (0, i, 0)),
            pl.BlockSpec((NSA_HEADS * NSA_D, tm), lambda i: (0, i)),
            pl.BlockSpec((NSA_G, 1, SEL_VROWS, tm), lambda i: (0, i, 0, 0)),
            pl.BlockSpec((NSA_G, SEL_VROWS, tm), lambda i: (0, 0, i)),
            pl.BlockSpec((LANES, tm), lambda i: (0, i)),
        ),
        compiler_params=_cparams(("parallel",)),
        name="in_proj",
    )(x2, attn_norm_w.reshape(1, D_MODEL), w_tok, w_feaT)


def _mla_prep_kernel(cq_ref, ckv_ref, kr_ref, pos_ref, qnw_ref, kvnw_ref, wuqT_ref, wuk_ref, wuvT_ref, inv_ref,
                     qT_ref, k_ref, vT_ref):
    scale = (MLA_NOPE + MLA_ROPE) ** -0.5 * LOG2E
    qn = _rms(cq_ref[...], qnw_ref[...]).astype(BF16)
    qT = lax.dot_general(wuqT_ref[...], qn, NT_DIMS, preferred_element_type=F32)
    kvn = _rms(ckv_ref[...], kvnw_ref[...]).astype(BF16)
    kn = jnp.dot(kvn, wuk_ref[...], preferred_element_type=F32)
    vT = lax.dot_general(wuvT_ref[...], kvn, NT_DIMS, preferred_element_type=F32)

    ang = pos_ref[...].astype(F32) * inv_ref[...]
    c = jnp.cos(ang)
    s = jnp.sin(ang)
    lane = lax.broadcasted_iota(jnp.int32, ang.shape, 1)
    coef = jnp.where(lane < 64, c, jnp.where(lane < 96, -s, s))
    coefT = coef.T

    ones_row = jnp.where(lax.broadcasted_iota(jnp.int32, (MLA_VROWS - MLA_V, coefT.shape[1]), 0) == 0, 1.0, 0.0)
    t = kr_ref[...] * coef
    kpe = jnp.where(lane < 64, t + pltpu.roll(t, 64, 1), 0.0)
    for h in range(MLA_HEADS):
        base = h * MLA_QK_PAD
        tT = qT[base + 128:base + 256, :] * coefT
        ropeT = tT + pltpu.roll(tT, 64, 0)
        qT_ref[h] = (jnp.concatenate([qT[base:base + 128, :], ropeT], axis=0) * scale).astype(BF16)
        k_ref[h] = jnp.concatenate([kn[:, h * 128:(h + 1) * 128], kpe], axis=-1).astype(BF16)
        vT_ref[h] = jnp.concatenate([vT[h * 128:(h + 1) * 128, :], ones_row], axis=0).astype(BF16)


def _mla_prep(pm, pos2, qnw, kvnw, wuqT, wuk, wuvT, inv128, tm=256):
    T = pm.shape[0]
    H = MLA_HEADS
    fixed = lambda i: (0, 0)
    return pl.pallas_call(
        _mla_prep_kernel,
        out_shape=(
            jax.ShapeDtypeStruct((H, MLA_QK_PAD, T), BF16),
            jax.ShapeDtypeStruct((H, T, MLA_QK_PAD), BF16),
            jax.ShapeDtypeStruct((H, MLA_VROWS, T), BF16),
        ),
        grid=(T // tm,),
        in_specs=[
            pl.BlockSpec((tm, MLA_Q_RANK), lambda i: (i, SEC_CQ // MLA_Q_RANK)),
            pl.BlockSpec((tm, MLA_KV_RANK), lambda i: (i, SEC_CKV // MLA_KV_RANK)),
            pl.BlockSpec((tm, LANES), lambda i: (i, SEC_ROPE // LANES)),
            pl.BlockSpec((tm, 1), lambda i: (i, 0)),
            pl.BlockSpec((1, MLA_Q_RANK), fixed),
            pl.BlockSpec((1, MLA_KV_RANK), fixed),
            pl.BlockSpec((H * MLA_QK_PAD, MLA_Q_RANK), fixed),
            pl.BlockSpec((MLA_KV_RANK, H * MLA_NOPE), fixed),
            pl.BlockSpec((H * MLA_V, MLA_KV_RANK), fixed),
            pl.BlockSpec((1, LANES), fixed),
        ],
        out_specs=(
            pl.BlockSpec((H, MLA_QK_PAD, tm), lambda i: (0, 0, i)),
            pl.BlockSpec((H, tm, MLA_QK_PAD), lambda i: (0, i, 0)),
            pl.BlockSpec((H, MLA_VROWS, tm), lambda i: (0, 0, i)),
        ),
        compiler_params=_cparams(("parallel",)),
        name="mla_prep",
    )(pm, pm, pm, pos2, qnw.reshape(1, -1), kvnw.reshape(1, -1), wuqT, wuk, wuvT, inv128)


def _mla_attn_kernel(qi_ref, ki_ref, qT_ref, k_ref, vT_ref, o_ref, m_scr, acc_scr,
                     s0_scr, s1_scr, top0_scr, top1_scr, p0_scr, p1_scr, al0_scr, al1_scr):
    p_id = pl.program_id(0)
    qi = qi_ref[p_id]
    ki = ki_ref[p_id]
    tk, tq = s0_scr.shape
    last = (qi * tq + tq - 1) // tk
    s_slots, top_slots = (s0_scr, s1_scr), (top0_scr, top1_scr)
    p_slots, al_slots = (p0_scr, p1_scr), (al0_scr, al1_scr)

    @pl.when(ki == 0)
    def _():
        m_scr[...] = jnp.full(m_scr.shape, NEG, F32)
        acc_scr[...] = jnp.zeros(acc_scr.shape, F32)

    def run(masked):
        def scores(h, slot):
            s = jnp.dot(k_ref[h], qT_ref[h], preferred_element_type=F32)
            if masked:
                key = ki * tk + lax.broadcasted_iota(jnp.int32, s.shape, 0)
                qry = qi * tq + lax.broadcasted_iota(jnp.int32, s.shape, 1)
                s = jnp.where(key <= qry, s, NEG)
            s_slots[slot][...] = s
            top_slots[slot][...] = jnp.max(s.reshape(tk // SUBLANES, SUBLANES, tq), axis=0)

        def softmax(h, slot):
            m_prev = m_scr[h]
            m_new = jnp.maximum(m_prev, jnp.max(top_slots[slot][...], axis=0, keepdims=True))
            for b in range(tk // BF16_ROWS):
                rows = slice(b * BF16_ROWS, (b + 1) * BF16_ROWS)
                p_slots[slot][rows, :] = jnp.exp2(s_slots[slot][rows, :] - m_new).astype(BF16)
            al_slots[slot][...] = jnp.exp2(m_prev - m_new)
            m_scr[h] = m_new

        def values(h, slot):
            acc = al_slots[slot][...] * acc_scr[h] + jnp.dot(vT_ref[h], p_slots[slot][...],
                                                             preferred_element_type=F32)
            acc_scr[h] = acc

        scores(0, 0)
        scores(1, 1)
        softmax(0, 0)

        def body(hh, carry):
            h = 2 * hh
            scores(h, 0)
            softmax(h - 1, 1)
            values(h - 2, 0)
            scores(h + 1, 1)
            softmax(h, 0)
            values(h - 1, 1)
            return carry

        lax.fori_loop(1, MLA_HEADS // 2, body, 0)
        softmax(MLA_HEADS - 1, 1)
        values(MLA_HEADS - 2, 0)
        values(MLA_HEADS - 1, 1)

    @pl.when(ki * tk + tk - 1 <= qi * tq)
    def _():
        run(False)

    @pl.when(ki * tk + tk - 1 > qi * tq)
    def _():
        run(True)

    @pl.when(ki == last)
    def _():
        for h in range(MLA_HEADS):
            acc = acc_scr[h]
            o_ref[h * MLA_V:(h + 1) * MLA_V, :] = acc[:MLA_V, :] * (1.0 / acc[MLA_V:MLA_V + 1, :])


def _tri_pairs(nq, per_q):
    qi, ki = [], []
    for i in range(nq):
        for c in range(per_q(i) + 1):
            qi.append(i)
            ki.append(c)
    return jnp.asarray(qi, jnp.int32), jnp.asarray(ki, jnp.int32)


def _mla_attn(qT, k, vT):
    H, T, _ = k.shape
    tq = min(MLA_TQ, T)
    tk = min(MLA_TK, T)
    qi, ki = _tri_pairs(T // tq, lambda i: (i * tq + tq - 1) // tk)
    grid_spec = pltpu.PrefetchScalarGridSpec(
        num_scalar_prefetch=2,
        grid=(int(qi.shape[0]),),
        in_specs=[
            pl.BlockSpec((H, MLA_QK_PAD, tq), lambda p, qi, ki: (0, 0, qi[p])),
            pl.BlockSpec((H, tk, MLA_QK_PAD), lambda p, qi, ki: (0, ki[p], 0)),
            pl.BlockSpec((H, MLA_VROWS, tk), lambda p, qi, ki: (0, 0, ki[p])),
        ],
        out_specs=pl.BlockSpec((H * MLA_V, tq), lambda p, qi, ki: (0, qi[p])),
        scratch_shapes=[
            pltpu.VMEM((H, 1, tq), F32),
            pltpu.VMEM((H, MLA_VROWS, tq), F32),
            pltpu.VMEM((tk, tq), F32),
            pltpu.VMEM((tk, tq), F32),
            pltpu.VMEM((SUBLANES, tq), F32),
            pltpu.VMEM((SUBLANES, tq), F32),
            pltpu.VMEM((tk, tq), BF16),
            pltpu.VMEM((tk, tq), BF16),
            pltpu.VMEM((1, tq), F32),
            pltpu.VMEM((1, tq), F32),
        ],
    )
    return pl.pallas_call(
        _mla_attn_kernel,
        out_shape=jax.ShapeDtypeStruct((H * MLA_V, T), F32),
        grid_spec=grid_spec,
        compiler_params=_cparams(("arbitrary",)),
        name="mla_attn",
    )(qi, ki, qT, k, vT)


def _gelu_tanh(x):
    return 0.5 * x * (1.0 + jnp.tanh(np.sqrt(2.0 / np.pi).astype(np.float32) * (x + 0.044715 * (x * x * x))))


def _compress_kernel(r_ref, pos_ref, w1_ref, w2_ref, kc_ref, vcT_ref):
    r = r_ref[0]
    nr = r.shape[0]
    half = CMP_STRIDE * NSA_D
    hp = lax.Precision.HIGHEST
    a = jnp.dot(r + pos_ref[0, 0:1, :], w1_ref[0, :half, :], precision=hp, preferred_element_type=F32)
    b = jnp.dot(r + pos_ref[0, 1:2, :], w1_ref[0, half:, :], precision=hp, preferred_element_type=F32)
    hid = _gelu_tanh(a + pltpu.roll(b, nr - 1, 0))
    out = jnp.dot(hid, w2_ref[0], precision=hp, preferred_element_type=F32)
    row = lax.broadcasted_iota(jnp.int32, out.shape, 0)
    out = jnp.where(row < nr - 1, out, 0.0)
    is_key = pl.program_id(0) % 2 == 0

    @pl.when(is_key)
    def _():
        kc_ref[0] = jnp.concatenate([out, _digit_cols(row, CMP_STRIDE)], axis=-1).astype(BF16)

    @pl.when(jnp.logical_not(is_key))
    def _():
        vcT_ref[0] = jnp.concatenate([out, jnp.zeros_like(out)], axis=-1).T[0:NSA_D, :].astype(BF16)


def _compress(r4, pos_kv, w1_kv, w2_kv):
    _, nr, width = r4.shape
    return pl.pallas_call(
        _compress_kernel,
        out_shape=(
            jax.ShapeDtypeStruct((NSA_G, nr, 2 * NSA_D), BF16),
            jax.ShapeDtypeStruct((NSA_G, NSA_D, nr), BF16),
        ),
        grid=(2 * NSA_G,),
        in_specs=[
            pl.BlockSpec((1, nr, width), lambda a: (a, 0, 0)),
            pl.BlockSpec((1, 2, width), lambda a: (a % 2, 0, 0)),
            pl.BlockSpec((1, 2 * width, CMP_HIDDEN), lambda a: (a % 2, 0, 0)),
            pl.BlockSpec((1, CMP_HIDDEN, NSA_D), lambda a: (a % 2, 0, 0)),
        ],
        out_specs=(
            pl.BlockSpec((1, nr, 2 * NSA_D), lambda a: (a // 2, 0, 0)),
            pl.BlockSpec((1, NSA_D, nr), lambda a: (a // 2, 0, 0)),
        ),
        compiler_params=_cparams(("arbitrary",)),
        name="nsa_compress",
    )(r4, pos_kv, w1_kv, w2_kv)


def _alibi_rows(tq):
    start = np.float32(2.0 ** (-8.0 / NSA_HEADS))
    slopes = (start ** np.arange(1, NSA_HEADS + 1, dtype=np.float32)).astype(np.float32)
    slopes = (slopes.astype(np.float64) * LOG2E).astype(np.float32)
    parts, rest = [], slopes
    for _ in range(ALIBI_PARTS):
        piece = (rest.view(np.uint32) & np.uint32(0xFFFF0000)).view(np.float32)
        parts.append(piece)
        rest = rest - piece
    assert not rest.any()
    pieces = np.stack(parts)
    rows = np.concatenate([pieces * ALIBI_HI, pieces, np.zeros((NSA_D - 2 * ALIBI_PARTS, NSA_HEADS), np.float32)])
    rows = np.repeat(rows.reshape(NSA_D, NSA_G, NSA_HPG), tq, axis=2).transpose(1, 0, 2)
    srow = np.repeat(slopes.reshape(NSA_G, 1, NSA_HPG), tq, axis=2)
    return jnp.asarray(rows, BF16), jnp.asarray(srow, F32)


def _nsa_queries(qT_ref, arow_ref):
    q = qT_ref[...] * (NSA_D ** -0.5 * LOG2E)
    top = jnp.concatenate([q[h * NSA_D:(h + 1) * NSA_D, :] for h in range(NSA_HPG)], axis=1).astype(BF16)
    return jnp.concatenate([top, arow_ref[0]], axis=0)


def _heads_to_rows(o):
    tq = o.shape[1] // NSA_HPG
    return jnp.concatenate([o[:, h * tq:(h + 1) * tq] for h in range(NSA_HPG)], axis=0)


def _token_pos(i, tq):
    return i * tq + jnp.bitwise_and(lax.broadcasted_iota(jnp.int32, (1, NSA_HPG * tq), 1), tq - 1)


def _bf16_pieces(x, n):
    pieces = []
    for _ in range(n - 1):
        top = pltpu.bitcast(jnp.bitwise_and(pltpu.bitcast(x, jnp.uint32), jnp.uint32(0xFFFF0000)), F32)
        pieces.append(top.astype(BF16))
        x = x - top
    return pieces + [x.astype(BF16)]


CMP_BUCKET = 128


def _cmp_attn_kernel(qT_ref, arow_ref, kc_ref, vcT_ref, ov_ref, o_ref, selT_ref, *, nsel_pad):
    i = pl.program_id(1)
    tq = NSA_TQ
    n = NSA_HPG * tq
    nr = kc_ref.shape[1]
    tpos = _token_pos(i, tq)
    last_done = jnp.right_shift(tpos - (CMP_LEN - 1), 4)

    def attend(rows):
        s = jnp.dot(kc_ref[0, 0:rows, :], _nsa_queries(qT_ref, arow_ref), preferred_element_type=F32)
        valid = lax.broadcasted_iota(jnp.int32, (rows, n), 0) <= last_done
        s = jnp.where(valid, s, NEG)
        e = jnp.where(valid, jnp.exp2(s - jnp.max(s, axis=0, keepdims=True)), 0.0)
        l = jnp.sum(e, axis=0, keepdims=True)
        inv = 1.0 / jnp.where(l > 0.0, l, 1.0)
        acc = jnp.dot(vcT_ref[0, :, 0:rows], e.astype(BF16), preferred_element_type=F32)
        o_ref[...] = _heads_to_rows(acc * inv)

        pn = e * inv
        psum = pn[:, 0:tq]
        for h in range(1, NSA_HPG):
            psum = psum + pn[:, h * tq:(h + 1) * tq]
        nb = min(nsel_pad, rows * CMP_STRIDE // SEL_LEN)
        imp = sum(jnp.dot(ov_ref[0:nb, 0:rows], piece, preferred_element_type=F32)
                  for piece in _bf16_pieces(psum, 3))

        blk = lax.broadcasted_iota(jnp.int32, (nb, tq), 0)
        cur = jnp.right_shift(i * tq + lax.broadcasted_iota(jnp.int32, (nb, tq), 1), SEL_SHIFT)
        forced = jnp.logical_or(blk == 0, jnp.logical_or(blk == cur, blk == cur - 1))
        score = jnp.where(blk <= cur, imp + jnp.where(forced, FORCE_BONUS, 0.0), NEG)
        blk_f = blk.astype(F32)

        def pick(_, carry):
            sc, sel = carry
            mx = jnp.max(sc, axis=0, keepdims=True)
            first = jnp.min(jnp.where(sc == mx, blk_f, float(nb)), axis=0, keepdims=True)
            hit = blk_f == first
            sel = jnp.where(jnp.logical_and(hit, mx > NEG * 0.5), 1.0, sel)
            return jnp.where(hit, REMOVED, sc), sel

        _, sel = lax.fori_loop(0, SEL_TOPK, pick, (score, jnp.zeros((nb, tq), F32)))
        selT_ref[0, 0:nb, :] = sel
        if nb < nsel_pad:
            selT_ref[0, nb:, :] = jnp.zeros((nsel_pad - nb, tq), F32)

    n_buckets = -(-nr // CMP_BUCKET)
    bucket = jnp.maximum((i * tq + tq - CMP_LEN) // CMP_STRIDE, 0) // CMP_BUCKET
    for b in range(n_buckets):
        pl.when(bucket == b)(functools.partial(attend, min((b + 1) * CMP_BUCKET, nr)))


def _overlap_matrix(nsel_pad, nr):
    j = np.arange(nsel_pad)[:, None]
    n = np.arange(nr)[None, :]
    ov = (n * CMP_STRIDE <= j * SEL_LEN + (SEL_LEN - 1)) & (n * CMP_STRIDE + (CMP_LEN - 1) >= j * SEL_LEN) & (n < nr - 1)
    return jnp.asarray(ov, BF16)


def _cmp_attn(fm, arows, kc, vcT, nsel_pad):
    T = fm.shape[1]
    nr = kc.shape[1]
    tq = NSA_TQ
    n = NSA_HPG * tq
    return pl.pallas_call(
        functools.partial(_cmp_attn_kernel, nsel_pad=nsel_pad),
        out_shape=(
            jax.ShapeDtypeStruct((NSA_HEADS * NSA_D, T), F32),
            jax.ShapeDtypeStruct((NSA_G, nsel_pad, T), F32),
        ),
        grid=(NSA_G, T // tq),
        in_specs=[
            pl.BlockSpec((NSA_HPG * NSA_D, tq), lambda g, i: (g, i)),
            pl.BlockSpec((1, NSA_D, n), lambda g, i: (g, 0, 0)),
            pl.BlockSpec((1, nr, 2 * NSA_D), lambda g, i: (g, 0, 0)),
            pl.BlockSpec((1, NSA_D, nr), lambda g, i: (g, 0, 0)),
            pl.BlockSpec((nsel_pad, nr), lambda g, i: (0, 0)),
        ],
        out_specs=(
            pl.BlockSpec((NSA_HPG * NSA_D, tq), lambda g, i: (g, i)),
            pl.BlockSpec((1, nsel_pad, tq), lambda g, i: (g, 0, i)),
        ),
        compiler_params=_cparams(("parallel", "parallel")),
        name="nsa_cmp_attn",
    )(fm, arows, kc, vcT, _overlap_matrix(nsel_pad, nr))


def _sel_attn_kernel(qT_ref, selT_ref, arow_ref, srow_ref, ka_ref, va_ref, o_ref,
                     qa_scr, s0_scr, s1_scr, top0_scr, top1_scr, p0_scr, p1_scr, al0_scr, al1_scr, m_scr, acc_scr,
                     *, tk, n_chunks):
    i = pl.program_id(1)
    tq = SEL_TQ
    n = NSA_HPG * tq
    nsel_pad = selT_ref.shape[1]
    qa_scr[0:2 * NSA_D, :] = _nsa_queries(qT_ref, arow_ref)
    blk = lax.broadcasted_iota(jnp.int32, (nsel_pad, tq), 0)
    own = blk // (tq // SEL_LEN) == i
    chosen = jnp.logical_and(selT_ref[0] > 0.5, jnp.logical_not(own))
    qa_scr[2 * NSA_D:, :] = jnp.concatenate([jnp.where(chosen, 0.0, NEG_BF16).astype(BF16)] * NSA_HPG, axis=1)

    tpos = _token_pos(i, tq)
    slope_row = srow_ref[0]

    def rterm(c):
        return slope_row * (c * tk - tpos).astype(F32)

    c_own = (i * tq) // tk
    own_start = pl.multiple_of(i * tq, tq)
    s = jnp.dot(ka_ref[0, pl.ds(own_start, tq), 0:2 * NSA_D], qa_scr[0:2 * NSA_D, :], preferred_element_type=F32)
    kpos = i * tq + lax.broadcasted_iota(jnp.int32, s.shape, 0)
    s = jnp.where(kpos <= tpos, s, NEG)
    r_own = rterm(c_own)
    m0 = jnp.max(s, axis=0, keepdims=True) + r_own
    p = jnp.exp2(s - (m0 - r_own)).astype(BF16)
    v_own = va_ref[0, c_own, :, pl.ds(pl.multiple_of(i * tq - c_own * tk, tq), tq)]
    acc_scr[...] = jnp.dot(v_own, p, preferred_element_type=F32)
    m_scr[...] = m0

    s_slots = (s0_scr, s1_scr)
    p_slots = (p0_scr, p1_scr)
    al_slots = (al0_scr, al1_scr)

    top_slots = (top0_scr, top1_scr)

    def scores(j, slot):
        start = pl.multiple_of(j * tk, tk)
        s = jnp.dot(ka_ref[0, pl.ds(start, tk), :], qa_scr[...], preferred_element_type=F32)
        s_slots[slot][...] = s
        top_slots[slot][...] = jnp.max(s.reshape(tk // SUBLANES, SUBLANES, n), axis=0)

    def softmax(j, slot):
        r = rterm(j)
        m_prev = m_scr[...]
        s_ref, p_ref = s_slots[slot], p_slots[slot]
        m_new = jnp.maximum(m_prev, jnp.max(top_slots[slot][...], axis=0, keepdims=True) + r)
        shift = m_new - r
        for b in range(tk // BF16_ROWS):
            rows = slice(b * BF16_ROWS, (b + 1) * BF16_ROWS)
            p_ref[rows, :] = jnp.exp2(s_ref[rows, :] - shift).astype(BF16)
        al_slots[slot][...] = jnp.exp2(m_prev - m_new)
        m_scr[...] = m_new

    def values(j, slot):
        acc_scr[...] = (al_slots[slot][...] * acc_scr[...]
                        + jnp.dot(va_ref[0, j], p_slots[slot][...], preferred_element_type=F32))

    n_pairs = jnp.minimum(((i * tq + tq - 1) // tk + 2) // 2, n_chunks // 2)
    scores(0, 0)
    scores(1, 1)
    softmax(0, 0)

    def body(jj, carry):
        j = 2 * jj
        scores(j, 0)
        softmax(j - 1, 1)
        values(j - 2, 0)
        scores(j + 1, 1)
        softmax(j, 0)
        values(j - 1, 1)
        return carry

    lax.fori_loop(1, n_pairs, body, 0)
    softmax(2 * n_pairs - 1, 1)
    values(2 * n_pairs - 2, 0)
    values(2 * n_pairs - 1, 1)
    acc = acc_scr[...]
    o_ref[...] = _heads_to_rows(acc[0:NSA_D, :] * (1.0 / acc[NSA_D:NSA_D + 1, :]))


def _sel_attn(fm, selT, arows, srow, ka, va):
    _, T, kw = ka.shape
    tk = va.shape[-1]
    tq = min(SEL_TQ, T)
    nsel_pad = selT.shape[1]
    n = NSA_HPG * tq
    return pl.pallas_call(
        functools.partial(_sel_attn_kernel, tk=tk, n_chunks=T // tk),
        out_shape=jax.ShapeDtypeStruct((NSA_HEADS * NSA_D, T), F32),
        grid=(NSA_G, T // tq),
        in_specs=[
            pl.BlockSpec((NSA_HPG * NSA_D, tq), lambda g, i: (g, i)),
            pl.BlockSpec((1, nsel_pad, tq), lambda g, i: (g, 0, i)),
            pl.BlockSpec((1, NSA_D, n), lambda g, i: (g, 0, 0)),
            pl.BlockSpec((1, 1, n), lambda g, i: (g, 0, 0)),
            pl.BlockSpec((1, T, kw), lambda g, i: (g, 0, 0)),
            pl.BlockSpec((1, T // tk, SEL_VROWS, tk), lambda g, i: (g, 0, 0, 0)),
        ],
        out_specs=pl.BlockSpec((NSA_HPG * NSA_D, tq), lambda g, i: (g, i)),
        scratch_shapes=[
            pltpu.VMEM((kw, n), BF16),
            pltpu.VMEM((tk, n), F32),
            pltpu.VMEM((tk, n), F32),
            pltpu.VMEM((SUBLANES, n), F32),
            pltpu.VMEM((SUBLANES, n), F32),
            pltpu.VMEM((tk, n), BF16),
            pltpu.VMEM((tk, n), BF16),
            pltpu.VMEM((1, n), F32),
            pltpu.VMEM((1, n), F32),
            pltpu.VMEM((1, n), F32),
            pltpu.VMEM((SEL_VROWS, n), F32),
        ],
        compiler_params=_cparams(("parallel", "arbitrary")),
        name="nsa_sel_attn",
    )(fm, selT, arows, srow, ka, va)


WIN_KEYS = WINDOW + NSA_TQ


def _win_attn_kernel(qT_ref, arow_ref, srow_ref, ka_ref, va_ref, o_ref):
    i = pl.program_id(1)
    tq = NSA_TQ
    n = NSA_HPG * tq
    n_tiles = WIN_KEYS // tq
    start = pl.multiple_of(jnp.maximum(i * tq - WINDOW, 0), tq)
    tpos = _token_pos(i, tq)
    slope_row = srow_ref[0]

    def run(edges_only):
        s = jnp.dot(ka_ref[0, pl.ds(start, WIN_KEYS), :], _nsa_queries(qT_ref, arow_ref),
                    preferred_element_type=F32)
        tiles = [s[j * tq:(j + 1) * tq, :] for j in range(n_tiles)]
        if edges_only:
            k_first = start + lax.broadcasted_iota(jnp.int32, tiles[0].shape, 0)
            tiles[0] = jnp.where(k_first > tpos - WINDOW, tiles[0], NEG)
            k_last = start + (n_tiles - 1) * tq + lax.broadcasted_iota(jnp.int32, tiles[-1].shape, 0)
            tiles[-1] = jnp.where(k_last <= tpos, tiles[-1], NEG)
        else:
            kpos = start + lax.broadcasted_iota(jnp.int32, s.shape, 0)
            s = jnp.where(jnp.logical_and(kpos <= tpos, kpos > tpos - WINDOW), s, NEG)
            tiles = [s[j * tq:(j + 1) * tq, :] for j in range(n_tiles)]
        rterms = [slope_row * (start + j * tq - tpos).astype(F32) for j in range(n_tiles)]
        m = jnp.max(tiles[0], axis=0, keepdims=True) + rterms[0]
        for sj, rj in zip(tiles[1:], rterms[1:]):
            m = jnp.maximum(m, jnp.max(sj, axis=0, keepdims=True) + rj)
        p = jnp.concatenate([jnp.exp2(sj - (m - rj)) for sj, rj in zip(tiles, rterms)], axis=0).astype(BF16)
        acc = jnp.dot(va_ref[0, :, pl.ds(start, WIN_KEYS)], p, preferred_element_type=F32)
        o_ref[...] = _heads_to_rows(acc[0:NSA_D, :] * (1.0 / acc[NSA_D:NSA_D + 1, :]))

    @pl.when(i * tq >= WINDOW)
    def _():
        run(True)

    @pl.when(i * tq < WINDOW)
    def _():
        run(False)


def _win_attn(fm, arows, srow, ka, va):
    _, T, kw = ka.shape
    assert T >= WIN_KEYS
    tq = NSA_TQ
    n = NSA_HPG * tq
    return pl.pallas_call(
        _win_attn_kernel,
        out_shape=jax.ShapeDtypeStruct((NSA_HEADS * NSA_D, T), F32),
        grid=(NSA_G, T // tq),
        in_specs=[
            pl.BlockSpec((NSA_HPG * NSA_D, tq), lambda g, i: (g, i)),
            pl.BlockSpec((1, NSA_D, n), lambda g, i: (g, 0, 0)),
            pl.BlockSpec((1, 1, n), lambda g, i: (g, 0, 0)),
            pl.BlockSpec((1, T, kw), lambda g, i: (g, 0, 0)),
            pl.BlockSpec((1, SEL_VROWS, T), lambda g, i: (g, 0, 0)),
        ],
        out_specs=pl.BlockSpec((NSA_HPG * NSA_D, tq), lambda g, i: (g, i)),
        compiler_params=_cparams(("parallel", "parallel")),
        name="nsa_win_attn",
    )(fm, arows, srow, ka, va)


def _out_proj_kernel(x_ref, ymT_ref, ocT_ref, osT_ref, owT_ref, glT_ref, mnw_ref, nnw_ref, wo_ref, o_ref):
    gates = 1.0 / (1.0 + jnp.exp(-glT_ref[...]))
    heads = []
    for h in range(NSA_HEADS):
        rows = slice(h * NSA_D, (h + 1) * NSA_D)
        heads.append(gates[3 * h:3 * h + 1, :] * ocT_ref[rows, :]
                     + gates[3 * h + 1:3 * h + 2, :] * osT_ref[rows, :]
                     + gates[3 * h + 2:3 * h + 3, :] * owT_ref[rows, :])
    ynT = jnp.concatenate(heads, axis=0)

    def rms_cols(y, w_col):
        return (y * lax.rsqrt(jnp.mean(y * y, axis=0, keepdims=True) + EPS) * w_col).astype(BF16)

    half = MLA_HEADS * MLA_V
    o_ref[...] = (x_ref[...]
                  + lax.dot_general(rms_cols(ymT_ref[...], mnw_ref[...]), wo_ref[:half, :], TN_DIMS,
                                    preferred_element_type=F32)
                  + lax.dot_general(rms_cols(ynT, nnw_ref[...]), wo_ref[half:, :], TN_DIMS,
                                    preferred_element_type=F32))


def _out_proj(x2, y_mlaT, o_cT, o_sT, o_wT, glT, mnw, nnw, wo, tm=512):
    T = x2.shape[0]
    half = MLA_HEADS * MLA_V
    row = lambda i: (i, 0)
    col = lambda i: (0, i)
    fixed2 = lambda i: (0, 0)
    return pl.pallas_call(
        _out_proj_kernel,
        out_shape=jax.ShapeDtypeStruct((T, D_MODEL), F32),
        grid=(T // tm,),
        in_specs=[
            pl.BlockSpec((tm, D_MODEL), row),
            pl.BlockSpec((half, tm), col),
            pl.BlockSpec((half, tm), col),
            pl.BlockSpec((half, tm), col),
            pl.BlockSpec((half, tm), col),
            pl.BlockSpec((LANES, tm), col),
            pl.BlockSpec((half, 1), fixed2),
            pl.BlockSpec((half, 1), fixed2),
            pl.BlockSpec((2 * half, D_MODEL), fixed2),
        ],
        out_specs=pl.BlockSpec((tm, D_MODEL), row),
        compiler_params=_cparams(("parallel",)),
        name="out_proj",
    )(x2, y_mlaT, o_cT, o_sT, o_wT, glT, mnw.reshape(-1, 1), nnw.reshape(-1, 1), wo)


def _mlp_kernel(h_ref, nw_ref, wu_ref, wd_ref, fw_ref, o_ref, n_scr, acc_scr, *, final):
    c = pl.program_id(1)

    @pl.when(c == 0)
    def _():
        n_scr[...] = _rms(h_ref[...], nw_ref[...]).astype(BF16)
        acc_scr[...] = jnp.zeros(acc_scr.shape, F32)

    u = jnp.dot(n_scr[...], wu_ref[...], preferred_element_type=F32)
    a = jnp.square(jnp.maximum(u, 0.0)).astype(BF16)
    acc_scr[...] += jnp.dot(a, wd_ref[...], preferred_element_type=F32)

    @pl.when(c == pl.num_programs(1) - 1)
    def _():
        h2 = h_ref[...] + acc_scr[...]
        o_ref[...] = _rms(h2, fw_ref[...]) if final else h2


def _mlp(h1, nw, wu, wd, fw, final, tm=512, tc=1024):
    T = h1.shape[0]
    tm = min(tm, T)
    return pl.pallas_call(
        functools.partial(_mlp_kernel, final=final),
        out_shape=jax.ShapeDtypeStruct((T, D_MODEL), F32),
        grid=(T // tm, MLP_HIDDEN // tc),
        in_specs=[
            pl.BlockSpec((tm, D_MODEL), lambda i, c: (i, 0)),
            pl.BlockSpec((1, D_MODEL), lambda i, c: (0, 0)),
            pl.BlockSpec((D_MODEL, tc), lambda i, c: (0, c)),
            pl.BlockSpec((tc, D_MODEL), lambda i, c: (c, 0)),
            pl.BlockSpec((1, D_MODEL), lambda i, c: (0, 0)),
        ],
        out_specs=pl.BlockSpec((tm, D_MODEL), lambda i, c: (i, 0)),
        scratch_shapes=[pltpu.VMEM((tm, D_MODEL), BF16), pltpu.VMEM((tm, D_MODEL), F32)],
        compiler_params=_cparams(("parallel", "arbitrary")),
        name="mlp",
    )(h1, nw.reshape(1, -1), wu, wd, fw.reshape(1, -1))


def _regroup_w_in(w_in):
    cq = w_in[:, 0:512]
    ckv = w_in[:, 512:768]
    kr = w_in[:, 768:832]
    q = w_in[:, 832:1856]
    kv_cmp = w_in[:, 1856:2112]
    kv_slc = w_in[:, 2112:2368].reshape(-1, NSA_G, 2, NSA_D)
    kv_win = w_in[:, 2368:2624].reshape(-1, NSA_G, 2, NSA_D)
    gates = w_in[:, 2624:2672]
    x1, x2 = kr[:, :32], kr[:, 32:]
    flat = lambda t: t.reshape(t.shape[0], NSA_G * NSA_D)
    w_tok = jnp.concatenate([cq, ckv, x1, x2, x2, x1, kv_cmp, flat(kv_slc[:, :, 0]), flat(kv_win[:, :, 0])], axis=1)
    zeros = jnp.zeros((w_in.shape[0], FM_ROWS - ROW_GATE - gates.shape[1]), w_in.dtype)
    w_fea = jnp.concatenate([q, flat(kv_slc[:, :, 1]), flat(kv_win[:, :, 1]), gates, zeros], axis=1)
    return w_tok.astype(BF16), w_fea.T.astype(BF16)


def _regroup_w_uq(w_uq):
    w = w_uq.reshape(MLA_Q_RANK, MLA_HEADS, MLA_NOPE + MLA_ROPE)
    nope = w[:, :, :MLA_NOPE]
    x1 = w[:, :, MLA_NOPE:MLA_NOPE + 32]
    x2 = w[:, :, MLA_NOPE + 32:]
    return jnp.concatenate([nope, x1, x2, x2, x1], axis=-1).reshape(MLA_Q_RANK, MLA_HEADS * MLA_QK_PAD).astype(BF16)


def kernel(x, positions, attn_norm_w, w_in, mla_q_norm_w, mla_w_uq, mla_kv_norm_w, mla_w_ukv, cmp_pos_k,
           cmp_pos_v, cmp_w1_k, cmp_w2_k, cmp_w1_v, cmp_w2_v, mla_out_norm_w, nsa_out_norm_w, w_o, mlp_norm_w,
           w_up, w_down, final_norm_w):
    B, T, _ = x.shape
    depth = w_in.shape[0]
    inv = 1.0 / (ROPE_THETA ** (jnp.arange(0, MLA_ROPE, 2, dtype=F32) / MLA_ROPE))
    inv128 = jnp.tile(inv, 4).reshape(1, LANES)
    arows, srow = _alibi_rows(NSA_TQ)
    arows_sel, srow_sel = _alibi_rows(SEL_TQ)
    nsel_pad = -(-(T // SEL_LEN) // LANES) * LANES

    outs = []
    for b in range(B):
        h = x[b]
        pos2 = positions[b].reshape(T, 1)
        for l in range(depth):
            w_tok, w_feaT = _regroup_w_in(w_in[l])
            pm, r4, ka_s, ka_w, qT_nsa, va_s, va_w, glT = _in_proj(h, attn_norm_w[l], w_tok, w_feaT, nsel_pad)
            wukv = mla_w_ukv[l].reshape(MLA_KV_RANK, MLA_HEADS, MLA_NOPE + MLA_V)
            wuk = wukv[:, :, :MLA_NOPE].reshape(MLA_KV_RANK, MLA_HEADS * MLA_NOPE).astype(BF16)
            wuvT = wukv[:, :, MLA_NOPE:].reshape(MLA_KV_RANK, MLA_HEADS * MLA_V).T.astype(BF16)
            qT, k, vT = _mla_prep(pm, pos2, mla_q_norm_w[l], mla_kv_norm_w[l], _regroup_w_uq(mla_w_uq[l]).T,
                                  wuk, wuvT, inv128)
            y_mla = _mla_attn(qT, k, vT)

            half = CMP_STRIDE * NSA_D
            pos_kv = jnp.stack([cmp_pos_k[l].reshape(2, half), cmp_pos_v[l].reshape(2, half)])
            kc, vcT = _compress(r4.reshape(2 * NSA_G, T // CMP_STRIDE, half), pos_kv,
                                jnp.stack([cmp_w1_k[l], cmp_w1_v[l]]), jnp.stack([cmp_w2_k[l], cmp_w2_v[l]]))
            o_cT, selT = _cmp_attn(qT_nsa, arows, kc, vcT, nsel_pad)
            o_sT = _sel_attn(qT_nsa, selT, arows_sel, srow_sel, ka_s, va_s)
            o_wT = _win_attn(qT_nsa, arows, srow, ka_w, va_w)

            h = _out_proj(h, y_mla, o_cT, o_sT, o_wT, glT, mla_out_norm_w[l], nsa_out_norm_w[l],
                          w_o[l].astype(BF16))
            h = _mlp(h, mlp_norm_w[l], w_up[l].astype(BF16), w_down[l].astype(BF16), final_norm_w,
                     final=(l == depth - 1))
        outs.append(h)
    return jnp.stack(outs)
```
